```python
import math
import jax, jax.numpy as jnp
from jax import lax
import numpy as np

D_MODEL = 4096
BATCH = 4
SEQ = 4096
DEPTH = 1

HEAD_DIM = 128
SB_HEADS = 16
SB_WIDTH = SB_HEADS * HEAD_DIM
DA_HEADS = 8
DA_WIDTH = DA_HEADS * 2 * HEAD_DIM
GATE_WIDTH = 2 * D_MODEL
IN_WIDTH = 3 * SB_WIDTH + 3 * DA_WIDTH + GATE_WIDTH
Q_BLOCK = 128
ROPE_THETA = 10000.0
PEER_HEADS = 8
PEER_N_KEYS = 128
PEER_EXPERTS = PEER_N_KEYS * PEER_N_KEYS
PEER_TOPK = 16
PEER_KEY_DIM = 256
PEER_HALF = PEER_KEY_DIM // 2
PEER_CHUNK = 128
LN_EPS = 1e-5
DEEPNORM_ALPHA = (2 * DEPTH) ** 0.25
DEEPNORM_BETA = (8 * DEPTH) ** -0.25

kernel_name = "hybrid_sb_diffattn_peer_deepnorm"


def layer_norm(x, g, b):
    xf = x.astype(jnp.float32)
    mu = jnp.mean(xf, axis=-1, keepdims=True)
    var = jnp.mean(jnp.square(xf - mu), axis=-1, keepdims=True)
    return ((xf - mu) * lax.rsqrt(var + LN_EPS) * g + b).astype(x.dtype)


def rms_norm(x, g):
    xf = x.astype(jnp.float32)
    return (xf * lax.rsqrt(jnp.mean(jnp.square(xf), axis=-1, keepdims=True) + LN_EPS) * g).astype(x.dtype)


def rope_tables(positions):
    inv_freq = ROPE_THETA ** (-jnp.arange(0, HEAD_DIM, 2, dtype=jnp.float32) / HEAD_DIM)
    ang = positions.astype(jnp.float32)[..., None] * inv_freq
    return jnp.cos(ang), jnp.sin(ang)


def apply_rope(x, cos, sin):
    x1, x2 = jnp.split(x.astype(jnp.float32), 2, axis=-1)
    c = cos[:, :, None, :]
    s = sin[:, :, None, :]
    return jnp.concatenate([x1 * c - x2 * s, x1 * s + x2 * c], axis=-1).astype(x.dtype)


def to_query_blocks(x):
    b, h, s, d = x.shape
    return x.reshape(b, h, s // Q_BLOCK, Q_BLOCK, d).transpose(2, 0, 1, 3, 4)


def from_query_blocks(o):
    nb, b, h, q, d = o.shape
    return o.transpose(1, 0, 3, 2, 4).reshape(b, nb * q, h * d)


def stick_breaking_attention(q, k, v):
    s = q.shape[2]
    scale = HEAD_DIM ** -0.5
    kpos = jnp.arange(s)
    starts = jnp.arange(s // Q_BLOCK) * Q_BLOCK

    def block(args):
        qb, start = args
        z = jnp.einsum('bhqd,bhkd->bhqk', qb, k).astype(jnp.float32) * scale
        qpos = start + jnp.arange(Q_BLOCK)
        valid = kpos[None, :] < qpos[:, None]
        log_fail = jnp.where(valid, jax.nn.log_sigmoid(-z), 0.0)
        incl = lax.cumsum(log_fail, axis=3, reverse=True)
        excl = jnp.concatenate([incl[..., 1:], jnp.zeros_like(incl[..., :1])], axis=-1)
        w = jnp.where(valid, jnp.exp(jax.nn.log_sigmoid(z) + excl), 0.0)
        return jnp.einsum('bhqk,bhkd->bhqd', w.astype(v.dtype), v)

    return from_query_blocks(lax.map(block, (to_query_blocks(q), starts)))


def differential_attention(q1, q2, k1, k2, v, lam):
    s = q1.shape[2]
    scale = HEAD_DIM ** -0.5
    kpos = jnp.arange(s)
    starts = jnp.arange(s // Q_BLOCK) * Q_BLOCK

    def block(args):
        q1b, q2b, start = args
        qpos = start + jnp.arange(Q_BLOCK)
        causal = kpos[None, :] <= qpos[:, None]

        def probs(qb, kk):
            sc = jnp.einsum('bhqd,bhkd->bhqk', qb, kk).astype(jnp.float32) * scale
            return jax.nn.softmax(jnp.where(causal, sc, -jnp.inf), axis=-1)

        p = probs(q1b, k1) - lam * probs(q2b, k2)
        return jnp.einsum('bhqk,bhkd->bhqd', p.astype(v.dtype), v)

    return from_query_blocks(lax.map(block, (to_query_blocks(q1), to_query_blocks(q2), starts)))


def peer_layer(h, w_q, sub_keys, u_tab, v_tab):
    b, s, d = h.shape
    t = h.reshape(b * s, d)
    n_tok = b * s
    q = (t @ w_q).reshape(n_tok, PEER_HEADS, 2, PEER_HALF)
    scores = jnp.einsum('thcd,hckd->thck', q, sub_keys).astype(jnp.float32)
    s_top, i_top = lax.top_k(scores, PEER_TOPK)
    cand = (s_top[:, :, 0, :, None] + s_top[:, :, 1, None, :]).reshape(n_tok, PEER_HEADS, PEER_TOPK * PEER_TOPK)
    cand_idx = (i_top[:, :, 0, :, None] * PEER_N_KEYS + i_top[:, :, 1, None, :]).reshape(n_tok, PEER_HEADS, PEER_TOPK * PEER_TOPK)
    best, pos = lax.top_k(cand, PEER_TOPK)
    idx = jnp.take_along_axis(cand_idx, pos, axis=-1)
    gate = jax.nn.softmax(best, axis=-1).astype(h.dtype)
    n_sel = PEER_HEADS * PEER_TOPK
    n_chunks = n_tok // PEER_CHUNK

    def chunk(args):
        tc, ic, gc = args
        act = jax.nn.gelu(jnp.einsum('cd,ced->ce', tc, u_tab[ic]), approximate=False)
        return jnp.einsum('ce,ced->cd', gc * act, v_tab[ic])

    out = lax.map(chunk, (t.reshape(n_chunks, PEER_CHUNK, d),
                          idx.reshape(n_chunks, PEER_CHUNK, n_sel),
                          gate.reshape(n_chunks, PEER_CHUNK, n_sel)))
    return out.reshape(b, s, d)


def setup_inputs(seed: int = 0) -> dict:
    key = jax.random.key(seed)
    ks = jax.random.split(key, 24)
    f32 = jnp.float32
    nrm = lambda k, shape, sc: jax.random.normal(k, shape, f32) * sc
    x = jax.random.normal(ks[0], (BATCH, SEQ, D_MODEL), f32)
    offsets = jax.random.randint(ks[1], (BATCH, 1), 0, 1024, dtype=jnp.int32)
    positions = offsets + jnp.arange(SEQ, dtype=jnp.int32)[None, :]
    col_scale = jnp.concatenate([
        jnp.ones((2 * SB_WIDTH,), f32), jnp.full((SB_WIDTH,), DEEPNORM_BETA, f32),
        jnp.ones((2 * DA_WIDTH,), f32), jnp.full((DA_WIDTH,), DEEPNORM_BETA, f32),
        jnp.ones((GATE_WIDTH,), f32)])
    w_in = nrm(ks[2], (DEPTH, D_MODEL, IN_WIDTH), D_MODEL ** -0.5) * col_scale
    b_gate = nrm(ks[3], (DEPTH, GATE_WIDTH), 0.02)
    lambda_q1 = nrm(ks[4], (DEPTH, HEAD_DIM), 0.1)
    lambda_k1 = nrm(ks[5], (DEPTH, HEAD_DIM), 0.1)
    lambda_q2 = nrm(ks[6], (DEPTH, HEAD_DIM), 0.1)
    lambda_k2 = nrm(ks[7], (DEPTH, HEAD_DIM), 0.1)
    subln_g = 1.0 + nrm(ks[8], (DEPTH, 2 * HEAD_DIM), 0.02)
    w_sb_branch = nrm(ks[9], (DEPTH, SB_WIDTH, D_MODEL), SB_WIDTH ** -0.5 * DEEPNORM_BETA)
    w_da_branch = nrm(ks[10], (DEPTH, DA_WIDTH, D_MODEL), DA_WIDTH ** -0.5 * DEEPNORM_BETA)
    w_out = nrm(ks[11], (DEPTH, D_MODEL, D_MODEL), D_MODEL ** -0.5 * DEEPNORM_BETA)
    ln1_g = 1.0 + nrm(ks[12], (DEPTH, D_MODEL), 0.02)
    ln1_b = nrm(ks[13], (DEPTH, D_MODEL), 0.02)
    peer_w_q = nrm(ks[14], (DEPTH, D_MODEL, PEER_HEADS * PEER_KEY_DIM), D_MODEL ** -0.5)
    peer_sub_keys = nrm(ks[15], (DEPTH, PEER_HEADS, 2, PEER_N_KEYS, PEER_HALF), PEER_HALF ** -0.5)
    peer_u = nrm(ks[16], (DEPTH, PEER_EXPERTS, D_MODEL), D_MODEL ** -0.5)
    peer_v = nrm(ks[17], (DEPTH, PEER_EXPERTS, D_MODEL), (PEER_HEADS * PEER_TOPK) ** -0.5 * DEEPNORM_BETA)
    ln2_g = 1.0 + nrm(ks[18], (DEPTH, D_MODEL), 0.02)
    ln2_b = nrm(ks[19], (DEPTH, D_MODEL), 0.02)
    return {"x": x, "positions": positions, "w_in": w_in, "b_gate": b_gate,
            "lambda_q1": lambda_q1, "lambda_k1": lambda_k1, "lambda_q2": lambda_q2, "lambda_k2": lambda_k2,
            "subln_g": subln_g, "w_sb_branch": w_sb_branch, "w_da_branch": w_da_branch, "w_out": w_out,
            "ln1_g": ln1_g, "ln1_b": ln1_b, "peer_w_q": peer_w_q, "peer_sub_keys": peer_sub_keys,
            "peer_u": peer_u, "peer_v": peer_v, "ln2_g": ln2_g, "ln2_b": ln2_b}


def reference(x, positions, w_in, b_gate, lambda_q1, lambda_k1, lambda_q2, lambda_k2, subln_g,
              w_sb_branch, w_da_branch, w_out, ln1_g, ln1_b, peer_w_q, peer_sub_keys, peer_u, peer_v,
              ln2_g, ln2_b):
    b, s, _ = x.shape
    cos, sin = rope_tables(positions)
    splits = [SB_WIDTH, 2 * SB_WIDTH, 3 * SB_WIDTH,
              3 * SB_WIDTH + DA_WIDTH, 3 * SB_WIDTH + 2 * DA_WIDTH, 3 * SB_WIDTH + 3 * DA_WIDTH]
    h = x
    for l in range(DEPTH):
        lam_init = 0.8 - 0.6 * math.exp(-0.3 * l)
        proj = h @ w_in[l]
        sb_q, sb_k, sb_v, da_q, da_k, da_v, gate_pre = jnp.split(proj, splits, axis=-1)

        heads = lambda t: t.reshape(b, s, SB_HEADS, HEAD_DIM).transpose(0, 2, 1, 3)
        o_sb = stick_breaking_attention(heads(sb_q), heads(sb_k), heads(sb_v))

        def qk_pair(t):
            t = apply_rope(t.reshape(b, s, 2 * DA_HEADS, HEAD_DIM), cos, sin)
            t = t.reshape(b, s, DA_HEADS, 2, HEAD_DIM).transpose(0, 2, 3, 1, 4)
            return t[:, :, 0], t[:, :, 1]
        q1, q2 = qk_pair(da_q)
        k1, k2 = qk_pair(da_k)
        v_da = da_v.reshape(b, s, DA_HEADS, 2 * HEAD_DIM).transpose(0, 2, 1, 3)
        lam = (jnp.exp(jnp.sum(lambda_q1[l].astype(jnp.float32) * lambda_k1[l].astype(jnp.float32)))
               - jnp.exp(jnp.sum(lambda_q2[l].astype(jnp.float32) * lambda_k2[l].astype(jnp.float32)))
               + lam_init)
        o_da = differential_attention(q1, q2, k1, k2, v_da, lam)
        o_da = (rms_norm(o_da.reshape(b, s, DA_HEADS, 2 * HEAD_DIM), subln_g[l]) * (1.0 - lam_init)).reshape(b, s, DA_WIDTH)

        g_sb, g_da = jnp.split(jax.nn.sigmoid(gate_pre + b_gate[l]), 2, axis=-1)
        merged = g_sb * (o_sb @ w_sb_branch[l]) + g_da * (o_da @ w_da_branch[l])
        mix = merged @ w_out[l]
        h = layer_norm(DEEPNORM_ALPHA * h + mix, ln1_g[l], ln1_b[l])

        ffn = peer_layer(h, peer_w_q[l], peer_sub_keys[l], peer_u[l], peer_v[l])
        h = layer_norm(DEEPNORM_ALPHA * h + ffn, ln2_g[l], ln2_b[l])
    return h
```

```python
import functools
import math

import numpy as np
import jax
import jax.numpy as jnp
from jax import lax
from jax.experimental import pallas as pl
from jax.experimental.pallas import tpu as pltpu

LN_EPS = 1e-5
ROPE_THETA = 10000.0
PEER_TOPK = 16
V7X_VMEM_LIMIT_BYTES = 56 * 1024 * 1024
NEG_INF = float("-inf")


def _params(*sem):
    return pltpu.CompilerParams(dimension_semantics=sem, vmem_limit_bytes=V7X_VMEM_LIMIT_BYTES)


def _blk(n, pref):
    b = min(n, pref)
    while n % b:
        b //= 2
    return b


def _rope_kernel(pos_ref, freq_ref, cos_ref, sin_ref):
    ang = pos_ref[...].astype(jnp.float32) * freq_ref[...]
    hd = freq_ref.shape[1]
    lane = lax.broadcasted_iota(jnp.int32, ang.shape, 1)
    cos_ref[...] = jnp.cos(ang)
    s = jnp.sin(ang)
    sin_ref[...] = jnp.where(lane < hd // 2, -s, s)


def _rope_tables(positions, hd):
    t = positions.size
    bm = _blk(t, 2048)
    inv = ROPE_THETA ** (-np.arange(0, hd, 2, dtype=np.float32) / np.float32(hd))
    freq = jnp.asarray(np.concatenate([inv, inv]).astype(np.float32).reshape(1, hd))
    out = jax.ShapeDtypeStruct((t, hd), jnp.float32)
    return pl.pallas_call(
        _rope_kernel, out_shape=(out, out), grid=(t // bm,),
        in_specs=[pl.BlockSpec((bm, 1), lambda i: (i, 0)), pl.BlockSpec((1, hd), lambda i: (0, 0))],
        out_specs=(pl.BlockSpec((bm, hd), lambda i: (i, 0)), pl.BlockSpec((bm, hd), lambda i: (i, 0))),
        compiler_params=_params("parallel"), name="rope_tables",
    )(positions.reshape(t, 1), freq)


def _qkv_kernel(x_ref, w_ref, cos_ref, sin_ref, o_ref, *, rope_lo, rope_hi, hd):
    acc = jnp.dot(x_ref[...], w_ref[...], preferred_element_type=jnp.float32)
    j = pl.program_id(1)
    is_rope = jnp.logical_and(j >= rope_lo, j < rope_hi)

    @pl.when(is_rope)
    def _():
        c = cos_ref[...]
        s = sin_ref[...]
        for t in range(acc.shape[1] // hd):
            a = acc[:, t * hd:(t + 1) * hd]
            o_ref[:, t * hd:(t + 1) * hd] = (a * c + pltpu.roll(a, hd // 2, 1) * s).astype(o_ref.dtype)

    @pl.when(jnp.logical_not(is_rope))
    def _():
        o_ref[...] = acc.astype(o_ref.dtype)


def _qkv_proj(x, w, cos, sin, n_out, rope_cols, hd):
    m, k = x.shape
    bm, bn = _blk(m, 1024), _blk(math.gcd(n_out, rope_cols[0], rope_cols[1]), 1024)
    kern = functools.partial(_qkv_kernel, rope_lo=rope_cols[0] // bn, rope_hi=rope_cols[1] // bn, hd=hd)
    return pl.pallas_call(
        kern, out_shape=jax.ShapeDtypeStruct((m, n_out), jnp.bfloat16), grid=(m // bm, n_out // bn),
        in_specs=[pl.BlockSpec((bm, k), lambda i, j: (i, 0)), pl.BlockSpec((k, bn), lambda i, j: (0, j)),
                  pl.BlockSpec((bm, hd), lambda i, j: (i, 0)), pl.BlockSpec((bm, hd), lambda i, j: (i, 0))],
        out_specs=pl.BlockSpec((bm, bn), lambda i, j: (i, j)),
        compiler_params=_params("parallel", "parallel"), name="qkv_proj",
    )(x, w, cos, sin)


def _gate_kernel(x_ref, w_ref, b_ref, o_ref):
    acc = jnp.dot(x_ref[...], w_ref[...], preferred_element_type=jnp.float32)
    o_ref[...] = jax.nn.sigmoid(acc + b_ref[...]).astype(o_ref.dtype)


def _gate_proj(x, w, bias, col0, n_out):
    m, k = x.shape
    bm, bn = _blk(m, 1024), _blk(math.gcd(n_out, col0), 1024)
    off = col0 // bn
    return pl.pallas_call(
        _gate_kernel, out_shape=jax.ShapeDtypeStruct((m, n_out), jnp.float32), grid=(m // bm, n_out // bn),
        in_specs=[pl.BlockSpec((bm, k), lambda i, j: (i, 0)), pl.BlockSpec((k, bn), lambda i, j: (0, j + off)),
                  pl.BlockSpec((1, bn), lambda i, j: (0, j))],
        out_specs=pl.BlockSpec((bm, bn), lambda i, j: (i, j)),
        compiler_params=_params("parallel", "parallel"), name="gate_proj",
    )(x, w, bias)


def _mm_kernel(a_ref, b_ref, o_ref):
    o_ref[...] = jnp.dot(a_ref[...], b_ref[...], preferred_element_type=jnp.float32).astype(o_ref.dtype)


def _matmul(a, b, out_dtype, name):
    m, k = a.shape
    n = b.shape[1]
    bm, bn = _blk(m, 1024), _blk(n, 1024)
    return pl.pallas_call(
        _mm_kernel, out_shape=jax.ShapeDtypeStruct((m, n), out_dtype), grid=(m // bm, n // bn),
        in_specs=[pl.BlockSpec((bm, k), lambda i, j: (i, 0)), pl.BlockSpec((k, bn), lambda i, j: (0, j))],
        out_specs=pl.BlockSpec((bm, bn), lambda i, j: (i, j)),
        compiler_params=_params("parallel", "parallel"), name=name,
    )(a, b)


def _merge_kernel(osb_ref, oda_ref, wsb_ref, wda_ref, gsb_ref, gda_ref, o_ref):
    ysb = jnp.dot(osb_ref[...], wsb_ref[...], preferred_element_type=jnp.float32)
    yda = jnp.dot(oda_ref[...], wda_ref[...], preferred_element_type=jnp.float32)
    o_ref[...] = (gsb_ref[...] * ysb + gda_ref[...] * yda).astype(o_ref.dtype)


def _merge(o_sb, o_da, w_sb, w_da, gate):
    m = o_sb.shape[0]
    d = w_sb.shape[1]
    bm, bn = _blk(m, 1024), _blk(d, 512)
    goff = d // bn
    return pl.pallas_call(
        _merge_kernel, out_shape=jax.ShapeDtypeStruct((m, d), jnp.bfloat16), grid=(m // bm, d // bn),
        in_specs=[pl.BlockSpec((bm, o_sb.shape[1]), lambda i, j: (i, 0)),
                  pl.BlockSpec((bm, o_da.shape[1]), lambda i, j: (i, 0)),
                  pl.BlockSpec((w_sb.shape[0], bn), lambda i, j: (0, j)),
                  pl.BlockSpec((w_da.shape[0], bn), lambda i, j: (0, j)),
                  pl.BlockSpec((bm, bn), lambda i, j: (i, j)),
                  pl.BlockSpec((bm, bn), lambda i, j: (i, j + goff))],
        out_specs=pl.BlockSpec((bm, bn), lambda i, j: (i, j)),
        compiler_params=_params("parallel", "parallel"), name="branch_merge",
    )(o_sb, o_da, w_sb, w_da, gate, gate)


def _ln_kernel(res_ref, y_ref, g_ref, b_ref, o_ref, obf_ref, *, alpha):
    z = alpha * res_ref[...] + y_ref[...]
    mu = jnp.mean(z, axis=-1, keepdims=True)
    zc = z - mu
    var = jnp.mean(zc * zc, axis=-1, keepdims=True)
    out = zc * lax.rsqrt(var + LN_EPS) * g_ref[...] + b_ref[...]
    o_ref[...] = out
    obf_ref[...] = out.astype(obf_ref.dtype)


def _residual_ln(res, y, g, b, alpha):
    m, d = res.shape
    bm = _blk(m, 256)
    row = pl.BlockSpec((bm, d), lambda i: (i, 0))
    vec = pl.BlockSpec((1, d), lambda i: (0, 0))
    return pl.pallas_call(
        functools.partial(_ln_kernel, alpha=alpha),
        out_shape=(jax.ShapeDtypeStruct((m, d), jnp.float32), jax.ShapeDtypeStruct((m, d), jnp.bfloat16)),
        grid=(m // bm,), in_specs=[row, row, vec, vec], out_specs=(row, row),
        compiler_params=_params("parallel"), name="residual_layernorm",
    )(res, y, g.reshape(1, d), b.reshape(1, d))


def _sb_kernel(q_ref, k_ref, v_ref, tri_ref, o_ref, c_ref, acc_ref, *, scale, blk):
    qi = pl.program_id(2)
    q = q_ref[...]
    row = lax.broadcasted_iota(jnp.int32, (blk, blk), 0)
    col = lax.broadcasted_iota(jnp.int32, (blk, blk), 1)
    valid_diag = col < row
    tri = tri_ref[...]

    def visit(j, valid):
        start = pl.multiple_of(j * blk, blk)
        k = k_ref[pl.ds(start, blk), :]
        v = v_ref[pl.ds(start, blk), :]
        z = lax.dot_general(q, k, (((1,), (1,)), ((), ())), preferred_element_type=jnp.float32) * scale
        log_fail = -(jnp.maximum(z, 0.0) + jnp.log1p(jnp.exp(-jnp.abs(z))))
        if valid is not None:
            log_fail = jnp.where(valid, log_fail, 0.0)
        hi = log_fail.astype(jnp.bfloat16)
        lo = (log_fail - hi.astype(jnp.float32)).astype(jnp.bfloat16)
        sums = (jnp.dot(hi, tri, preferred_element_type=jnp.float32)
                + jnp.dot(lo, tri, preferred_element_type=jnp.float32))
        c = c_ref[...]
        w = jnp.exp(z + log_fail + sums[:, :blk] + c)
        if valid is not None:
            w = jnp.where(valid, w, 0.0)
        acc_ref[...] += jnp.dot(w.astype(v.dtype), v, preferred_element_type=jnp.float32)
        c_ref[...] = c + sums[:, blk:]

    c_ref[...] = jnp.zeros_like(c_ref)
    acc_ref[...] = jnp.zeros_like(acc_ref)
    visit(qi, valid_diag)

    def body(t, carry):
        visit(qi - 1 - t, None)
        return carry

    lax.fori_loop(0, qi, body, 0)
    o_ref[...] = acc_ref[...].astype(o_ref.dtype)


def _sb_attention(qkv, batch, seq, heads, hd):
    blk = _blk(seq, 256)
    nq = seq // blk
    tri = np.concatenate([np.tril(np.ones((blk, blk), np.float32), -1), np.ones((blk, blk), np.float32)], axis=1)
    kern = functools.partial(_sb_kernel, scale=hd ** -0.5, blk=blk)
    return pl.pallas_call(
        kern, out_shape=jax.ShapeDtypeStruct((batch * seq, heads * hd), jnp.bfloat16), grid=(batch, heads, nq),
        in_specs=[pl.BlockSpec((blk, hd), lambda b, h, i: (b * nq + i, h)),
                  pl.BlockSpec((seq, hd), lambda b, h, i: (b, heads + h)),
                  pl.BlockSpec((seq, hd), lambda b, h, i: (b, 2 * heads + h)),
                  pl.BlockSpec((blk, 2 * blk), lambda b, h, i: (0, 0))],
        out_specs=pl.BlockSpec((blk, hd), lambda b, h, i: (b * nq + i, h)),
        scratch_shapes=[pltpu.VMEM((blk, blk), jnp.float32), pltpu.VMEM((blk, hd), jnp.float32)],
        compiler_params=_params("parallel", "parallel", "parallel"), name="stick_breaking_attention",
    )(qkv, qkv, qkv, jnp.asarray(tri, jnp.bfloat16))


def _da_kernel(lq1_ref, lk1_ref, lq2_ref, lk2_ref, g_ref, q1_ref, q2_ref, k1_ref, k2_ref, v_ref, o_ref,
               m_ref, l_ref, acc_ref, *, scale, blk, lam_init):
    qi = pl.program_id(2)
    row = lax.broadcasted_iota(jnp.int32, (blk, blk), 0)
    col = lax.broadcasted_iota(jnp.int32, (blk, blk), 1)
    causal_diag = col <= row
    qs = (q1_ref[...], q2_ref[...])
    ks = (k1_ref, k2_ref)

    m_ref[...] = jnp.full_like(m_ref, NEG_INF)
    l_ref[...] = jnp.zeros_like(l_ref)
    acc_ref[...] = jnp.zeros_like(acc_ref)

    def visit(j, mask):
        start = pl.multiple_of(j * blk, blk)
        v = v_ref[pl.ds(start, blk), :]
        for s in range(2):
            k = ks[s][pl.ds(start, blk), :]
            sc = lax.dot_general(qs[s], k, (((1,), (1,)), ((), ())), preferred_element_type=jnp.float32) * scale
            if mask is not None:
                sc = jnp.where(mask, sc, NEG_INF)
            m_old = m_ref[s]
            m_new = jnp.maximum(m_old, jnp.max(sc, axis=-1, keepdims=True))
            p = jnp.exp(sc - m_new)
            corr = jnp.exp(m_old - m_new)
            l_ref[s] = corr * l_ref[s] + jnp.sum(p, axis=-1, keepdims=True)
            acc_ref[s] = corr * acc_ref[s] + jnp.dot(p.astype(v.dtype), v, preferred_element_type=jnp.float32)
            m_ref[s] = m_new

    def body(j, carry):
        visit(j, None)
        return carry

    lax.fori_loop(0, qi, body, 0)
    visit(qi, causal_diag)

    lam = (jnp.exp(jnp.sum(lq1_ref[...] * lk1_ref[...], axis=-1, keepdims=True))
           - jnp.exp(jnp.sum(lq2_ref[...] * lk2_ref[...], axis=-1, keepdims=True)) + lam_init)
    o = acc_ref[0] / l_ref[0] - lam * (acc_ref[1] / l_ref[1])
    o = o * lax.rsqrt(jnp.mean(o * o, axis=-1, keepdims=True) + LN_EPS) * g_ref[...] * (1.0 - lam_init)
    o_ref[...] = o.astype(o_ref.dtype)


def _da_attention(qkv, lambdas, subln_g, batch, seq, heads, hd, col0, lam_init):
    blk = _blk(seq, 256)
    nq = seq // blk
    qb, kb, vb = col0 // hd, col0 // hd + 2 * heads, col0 // (2 * hd) + 2 * heads
    vec = pl.BlockSpec((1, hd), lambda b, h, i: (0, 0))
    kern = functools.partial(_da_kernel, scale=hd ** -0.5, blk=blk, lam_init=lam_init)
    return pl.pallas_call(
        kern, out_shape=jax.ShapeDtypeStruct((batch * seq, heads * 2 * hd), jnp.bfloat16), grid=(batch, heads, nq),
        in_specs=[vec, vec, vec, vec, pl.BlockSpec((1, 2 * hd), lambda b, h, i: (0, 0)),
                  pl.BlockSpec((blk, hd), lambda b, h, i: (b * nq + i, qb + 2 * h)),
                  pl.BlockSpec((blk, hd), lambda b, h, i: (b * nq + i, qb + 2 * h + 1)),
                  pl.BlockSpec((seq, hd), lambda b, h, i: (b, kb + 2 * h)),
                  pl.BlockSpec((seq, hd), lambda b, h, i: (b, kb + 2 * h + 1)),
                  pl.BlockSpec((seq, 2 * hd), lambda b, h, i: (b, vb + h))],
        out_specs=pl.BlockSpec((blk, 2 * hd), lambda b, h, i: (b * nq + i, h)),
        scratch_shapes=[pltpu.VMEM((2, blk, 1), jnp.float32), pltpu.VMEM((2, blk, 1), jnp.float32),
                        pltpu.VMEM((2, blk, 2 * hd), jnp.float32)],
        compiler_params=_params("parallel", "parallel", "parallel"), name="differential_attention",
    )(*[l.reshape(1, hd) for l in lambdas], subln_g.reshape(1, 2 * hd), qkv, qkv, qkv, qkv, qkv)


def _extract_top(x, n):
    rows = lax.broadcasted_iota(jnp.int32, x.shape, 0)
    tops = []
    for _ in range(n):
        m = jnp.max(x, axis=0, keepdims=True)
        tops.append(m)
        first = jnp.min(jnp.where(x == m, rows, x.shape[0]), axis=0, keepdims=True)
        x = jnp.where(rows == first, NEG_INF, x)
    return tops


def _peer_topk_kernel(q_ref, keys_ref, s_ref, e_ref, tau_ref, *, n_keys, topk):
    half = q_ref.shape[1] // 2
    scores, tops = [], []
    for c in range(2):
        s = lax.dot_general(keys_ref[0, c], q_ref[:, c * half:(c + 1) * half], (((1,), (1,)), ((), ())),
                            preferred_element_type=jnp.float32)
        scores.append(s)
        tops.append(_extract_top(s, topk))
    top_b = jnp.concatenate(tops[1], axis=0)
    exp_b = jnp.exp(top_b - tops[1][0])
    cand = jnp.concatenate([tops[0][p] + top_b for p in range(topk)], axis=0)
    cand_e = jnp.concatenate([jnp.exp(tops[0][p] - tops[0][0]) * exp_b for p in range(topk)], axis=0)
    tau = _extract_top(cand, topk)[-1]
    z = jnp.sum(jnp.where(cand >= tau, cand_e, 0.0), axis=0, keepdims=True)
    s_ref[...] = jnp.concatenate(scores, axis=0)
    e_ref[...] = jnp.concatenate([jnp.exp(scores[0] - tops[0][0]) / z, jnp.exp(scores[1] - tops[1][0])], axis=0)
    tau_ref[0] = tau


def _peer_topk(hq, keys):
    t = hq.shape[0]
    heads, _, n_keys, half = keys.shape
    tn = _blk(t, 256)
    kern = functools.partial(_peer_topk_kernel, n_keys=n_keys, topk=PEER_TOPK)
    st = jax.ShapeDtypeStruct((heads * 2 * n_keys, t), jnp.float32)
    blk = pl.BlockSpec((2 * n_keys, tn), lambda i, h: (h, i))
    return pl.pallas_call(
        kern, out_shape=(st, st, jax.ShapeDtypeStruct((heads, 1, t), jnp.float32)), grid=(t // tn, heads),
        in_specs=[pl.BlockSpec((tn, 2 * half), lambda i, h: (i, h)),
                  pl.BlockSpec((1, 2, n_keys, half), lambda i, h: (h, 0, 0, 0))],
        out_specs=(blk, blk, pl.BlockSpec((1, 1, tn), lambda i, h: (h, 0, i))),
        compiler_params=_params("parallel", "parallel"), name="peer_topk",
    )(hq, keys)


def _peer_dense_kernel(h_ref, u_ref, v_ref, s_ref, e_ref, tau_ref, o_ref, *, heads, n_keys):
    j = pl.program_id(1)
    te = u_ref.shape[0]
    n_i = te // n_keys
    pre = lax.dot_general(u_ref[...], h_ref[...], (((1,), (1,)), ((), ())), preferred_element_type=jnp.float32)
    act = 0.5 * pre * (1.0 + lax.erf(pre * (2.0 ** -0.5)))
    parts = []
    for il in range(n_i):
        i_row = j * n_i + il
        g = None
        for h in range(heads):
            base = h * 2 * n_keys
            a_row = s_ref[pl.ds(base + i_row, 1), :]
            ea_row = e_ref[pl.ds(base + i_row, 1), :]
            b_tile = s_ref[base + n_keys:base + 2 * n_keys, :]
            eb_tile = e_ref[base + n_keys:base + 2 * n_keys, :]
            term = jnp.where(a_row + b_tile >= tau_ref[h], ea_row * eb_tile, 0.0)
            g = term if g is None else g + term
        parts.append(g)
    gate = jnp.concatenate(parts, axis=0) if n_i > 1 else parts[0]
    ga = (gate * act).astype(v_ref.dtype)
    contrib = lax.dot_general(ga, v_ref[...], (((0,), (0,)), ((), ())), preferred_element_type=jnp.float32)

    @pl.when(j == 0)
    def _():
        o_ref[...] = contrib

    @pl.when(j > 0)
    def _():
        o_ref[...] += contrib


def _peer_dense(h_bf, u_bf, v_bf, scores_t, exps_t, tau, heads, n_keys):
    t, d = h_bf.shape
    e = u_bf.shape[0]
    tm, te = _blk(t, 512), _blk(e, 512)
    kern = functools.partial(_peer_dense_kernel, heads=heads, n_keys=n_keys)
    rows = scores_t.shape[0]
    once = pl.Buffered(1)
    return pl.pallas_call(
        kern, out_shape=jax.ShapeDtypeStruct((t, d), jnp.float32), grid=(t // tm, e // te),
        in_specs=[pl.BlockSpec((tm, d), lambda i, j: (i, 0), pipeline_mode=once),
                  pl.BlockSpec((te, d), lambda i, j: (j, 0)),
                  pl.BlockSpec((te, d), lambda i, j: (j, 0)),
                  pl.BlockSpec((rows, tm), lambda i, j: (0, i), pipeline_mode=once),
                  pl.BlockSpec((rows, tm), lambda i, j: (0, i), pipeline_mode=once),
                  pl.BlockSpec((heads, 1, tm), lambda i, j: (0, 0, i))],
        out_specs=pl.BlockSpec((tm, d), lambda i, j: (i, 0)),
        compiler_params=_params("parallel", "arbitrary"), name="peer_dense",
    )(h_bf, u_bf, v_bf, scores_t, exps_t, tau)


def kernel(x, positions, w_in, b_gate, lambda_q1, lambda_k1, lambda_q2, lambda_k2, subln_g, w_sb_branch,
           w_da_branch, w_out, ln1_g, ln1_b, peer_w_q, peer_sub_keys, peer_u, peer_v, ln2_g, ln2_b):
    batch, seq, d = x.shape
    depth = w_in.shape[0]
    hd = lambda_q1.shape[-1]
    sb_w, da_w = w_sb_branch.shape[1], w_da_branch.shape[1]
    sb_heads, da_heads = sb_w // hd, da_w // (2 * hd)
    n_qkv = 3 * sb_w + 3 * da_w
    alpha = (2 * depth) ** 0.25
    bf = jnp.bfloat16
    t = batch * seq

    cos, sin = _rope_tables(positions, hd)
    h = x.reshape(t, d)
    h_bf = h.astype(bf)
    for l in range(depth):
        lam_init = 0.8 - 0.6 * math.exp(-0.3 * l)
        w_in_bf = w_in[l].astype(bf)
        qkv = _qkv_proj(h_bf, w_in_bf, cos, sin, n_qkv, (3 * sb_w, 3 * sb_w + 2 * da_w), hd)
        gate = _gate_proj(h_bf, w_in_bf, b_gate[l].reshape(1, -1), n_qkv, 2 * d)
        o_sb = _sb_attention(qkv, batch, seq, sb_heads, hd)
        o_da = _da_attention(qkv, (lambda_q1[l], lambda_k1[l], lambda_q2[l], lambda_k2[l]), subln_g[l],
                             batch, seq, da_heads, hd, 3 * sb_w, lam_init)
        merged = _merge(o_sb, o_da, w_sb_branch[l].astype(bf), w_da_branch[l].astype(bf), gate)
        mix = _matmul(merged, w_out[l].astype(bf), jnp.float32, "out_proj")
        h, h_bf = _residual_ln(h, mix, ln1_g[l], ln1_b[l], alpha)

        heads, _, n_keys, half = peer_sub_keys[l].shape
        hq = _matmul(h_bf, peer_w_q[l].astype(bf), bf, "peer_query")
        scores_t, exps_t, tau = _peer_topk(hq, peer_sub_keys[l].astype(bf))
        ffn = _peer_dense(h_bf, peer_u[l].astype(bf), peer_v[l].astype(bf), scores_t, exps_t, tau, heads, n_keys)
        h, h_bf = _residual_ln(h, ffn, ln2_g[l], ln2_b[l], alpha)
    return h.reshape(batch, seq, d)
```

```python
import functools
import math

import numpy as np
import jax
import jax.numpy as jnp
from jax import lax
from jax.experimental import pallas as pl
from jax.experimental.pallas import tpu as pltpu

LN_EPS = 1e-5
ROPE_THETA = 10000.0
PEER_TOPK = 16
V7X_VMEM_LIMIT_BYTES = 56 * 1024 * 1024
NEG_INF = float("-inf")


def _params(*sem):
    return pltpu.CompilerParams(dimension_semantics=sem, vmem_limit_bytes=V7X_VMEM_LIMIT_BYTES)


def _blk(n, pref):
    b = min(n, pref)
    while n % b:
        b //= 2
    return b


def _rope_kernel(pos_ref, freq_ref, cos_ref, sin_ref):
    ang = pos_ref[...].astype(jnp.float32) * freq_ref[...]
    hd = freq_ref.shape[1]
    lane = lax.broadcasted_iota(jnp.int32, ang.shape, 1)
    cos_ref[...] = jnp.cos(ang)
    s = jnp.sin(ang)
    sin_ref[...] = jnp.where(lane < hd // 2, -s, s)


def _rope_tables(positions, hd):
    t = positions.size
    bm = _blk(t, 2048)
    inv = ROPE_THETA ** (-np.arange(0, hd, 2, dtype=np.float32) / np.float32(hd))
    freq = jnp.asarray(np.concatenate([inv, inv]).astype(np.float32).reshape(1, hd))
    out = jax.ShapeDtypeStruct((t, hd), jnp.float32)
    return pl.pallas_call(
        _rope_kernel, out_shape=(out, out), grid=(t // bm,),
        in_specs=[pl.BlockSpec((bm, 1), lambda i: (i, 0)), pl.BlockSpec((1, hd), lambda i: (0, 0))],
        out_specs=(pl.BlockSpec((bm, hd), lambda i: (i, 0)), pl.BlockSpec((bm, hd), lambda i: (i, 0))),
        compiler_params=_params("parallel"), name="rope_tables",
    )(positions.reshape(t, 1), freq)


def _qkv_kernel(x_ref, w_ref, cs_ref, cos_ref, sin_ref, o_ref, *, rope_lo, rope_hi, hd):
    acc = jnp.dot(x_ref[...], w_ref[...], preferred_element_type=jnp.float32) * cs_ref[...]
    j = pl.program_id(1)
    is_rope = jnp.logical_and(j >= rope_lo, j < rope_hi)

    @pl.when(is_rope)
    def _():
        c = cos_ref[...]
        s = sin_ref[...]
        for t in range(acc.shape[1] // hd):
            a = acc[:, t * hd:(t + 1) * hd]
            o_ref[:, t * hd:(t + 1) * hd] = (a * c + pltpu.roll(a, hd // 2, 1) * s).astype(o_ref.dtype)

    @pl.when(jnp.logical_not(is_rope))
    def _():
        o_ref[...] = acc.astype(o_ref.dtype)


def _qkv_proj(x, w, col_scale, cos, sin, n_out, rope_cols, hd):
    m, k = x.shape
    bm, bn = _blk(m, 1024), _blk(math.gcd(n_out, rope_cols[0], rope_cols[1]), 1024)
    kern = functools.partial(_qkv_kernel, rope_lo=rope_cols[0] // bn, rope_hi=rope_cols[1] // bn, hd=hd)
    return pl.pallas_call(
        kern, out_shape=jax.ShapeDtypeStruct((m, n_out), jnp.bfloat16), grid=(m // bm, n_out // bn),
        in_specs=[pl.BlockSpec((bm, k), lambda i, j: (i, 0)), pl.BlockSpec((k, bn), lambda i, j: (0, j)),
                  pl.BlockSpec((1, bn), lambda i, j: (0, j)),
                  pl.BlockSpec((bm, hd), lambda i, j: (i, 0)), pl.BlockSpec((bm, hd), lambda i, j: (i, 0))],
        out_specs=pl.BlockSpec((bm, bn), lambda i, j: (i, j)),
        compiler_params=_params("parallel", "parallel"), name="qkv_proj",
    )(x, w, col_scale, cos, sin)


def _gate_kernel(x_ref, w_ref, b_ref, o_ref):
    acc = jnp.dot(x_ref[...], w_ref[...], preferred_element_type=jnp.float32)
    o_ref[...] = jax.nn.sigmoid(acc + b_ref[...]).astype(o_ref.dtype)


def _gate_proj(x, w, bias, col0, n_out):
    m, k = x.shape
    bm, bn = _blk(m, 1024), _blk(math.gcd(n_out, col0), 1024)
    off = col0 // bn
    return pl.pallas_call(
        _gate_kernel, out_shape=jax.ShapeDtypeStruct((m, n_out), jnp.float32), grid=(m // bm, n_out // bn),
        in_specs=[pl.BlockSpec((bm, k), lambda i, j: (i, 0)), pl.BlockSpec((k, bn), lambda i, j: (0, j + off)),
                  pl.BlockSpec((1, bn), lambda i, j: (0, j))],
        out_specs=pl.BlockSpec((bm, bn), lambda i, j: (i, j)),
        compiler_params=_params("parallel", "parallel"), name="gate_proj",
    )(x, w, bias)


def _mm_kernel(a_ref, b_ref, o_ref):
    o_ref[...] = jnp.dot(a_ref[...], b_ref[...], preferred_element_type=jnp.float32).astype(o_ref.dtype)


def _matmul(a, b, out_dtype, name):
    m, k = a.shape
    n = b.shape[1]
    bm, bn = _blk(m, 1024), _blk(n, 1024)
    return pl.pallas_call(
        _mm_kernel, out_shape=jax.ShapeDtypeStruct((m, n), out_dtype), grid=(m // bm, n // bn),
        in_specs=[pl.BlockSpec((bm, k), lambda i, j: (i, 0)), pl.BlockSpec((k, bn), lambda i, j: (0, j))],
        out_specs=pl.BlockSpec((bm, bn), lambda i, j: (i, j)),
        compiler_params=_params("parallel", "parallel"), name=name,
    )(a, b)


def _merge_kernel(osb_ref, oda_ref, wsb_ref, wda_ref, gsb_ref, gda_ref, o_ref):
    ysb = jnp.dot(osb_ref[...], wsb_ref[...], preferred_element_type=jnp.float32)
    yda = jnp.dot(oda_ref[...], wda_ref[...], preferred_element_type=jnp.float32)
    o_ref[...] = (gsb_ref[...] * ysb + gda_ref[...] * yda).astype(o_ref.dtype)


def _merge(o_sb, o_da, w_sb, w_da, gate):
    m = o_sb.shape[0]
    d = w_sb.shape[1]
    bm, bn = _blk(m, 1024), _blk(d, 512)
    goff = d // bn
    return pl.pallas_call(
        _merge_kernel, out_shape=jax.ShapeDtypeStruct((m, d), jnp.bfloat16), grid=(m // bm, d // bn),
        in_specs=[pl.BlockSpec((bm, o_sb.shape[1]), lambda i, j: (i, 0)),
                  pl.BlockSpec((bm, o_da.shape[1]), lambda i, j: (i, 0)),
                  pl.BlockSpec((w_sb.shape[0], bn), lambda i, j: (0, j)),
                  pl.BlockSpec((w_da.shape[0], bn), lambda i, j: (0, j)),
                  pl.BlockSpec((bm, bn), lambda i, j: (i, j)),
                  pl.BlockSpec((bm, bn), lambda i, j: (i, j + goff))],
        out_specs=pl.BlockSpec((bm, bn), lambda i, j: (i, j)),
        compiler_params=_params("parallel", "parallel"), name="branch_merge",
    )(o_sb, o_da, w_sb, w_da, gate, gate)


def _ln_kernel(res_ref, y_ref, g_ref, b_ref, o_ref, obf_ref, *, alpha):
    z = alpha * res_ref[...] + y_ref[...]
    mu = jnp.mean(z, axis=-1, keepdims=True)
    zc = z - mu
    var = jnp.mean(zc * zc, axis=-1, keepdims=True)
    out = zc * lax.rsqrt(var + LN_EPS) * g_ref[...] + b_ref[...]
    o_ref[...] = out
    obf_ref[...] = out.astype(obf_ref.dtype)


def _residual_ln(res, y, g, b, alpha):
    m, d = res.shape
    bm = _blk(m, 256)
    row = pl.BlockSpec((bm, d), lambda i: (i, 0))
    vec = pl.BlockSpec((1, d), lambda i: (0, 0))
    return pl.pallas_call(
        functools.partial(_ln_kernel, alpha=alpha),
        out_shape=(jax.ShapeDtypeStruct((m, d), jnp.float32), jax.ShapeDtypeStruct((m, d), jnp.bfloat16)),
        grid=(m // bm,), in_specs=[row, row, vec, vec], out_specs=(row, row),
        compiler_params=_params("parallel"), name="residual_layernorm",
    )(res, y, g.reshape(1, d), b.reshape(1, d))


SB_SKIP_LOG = -100.0


def _sb_kernel(q_ref, k_ref, v_ref, tri_ref, o_ref, c_ref, acc_ref, *, blk, hd, hps):
    qi = pl.program_id(2)
    row = lax.broadcasted_iota(jnp.int32, (blk, blk), 0)
    col = lax.broadcasted_iota(jnp.int32, (blk, blk), 1)
    valid_diag = col < row
    tri = tri_ref[...]

    def visit(j, valid):
        start = pl.multiple_of(j * blk, blk)
        for g in range(hps):
            q = q_ref[:, g * hd:(g + 1) * hd]
            k = k_ref[pl.ds(start, blk), g * hd:(g + 1) * hd]
            v = v_ref[pl.ds(start, blk), g * hd:(g + 1) * hd]
            z = lax.dot_general(q, k, (((1,), (1,)), ((), ())), preferred_element_type=jnp.float32)
            log_fail = -(jnp.maximum(z, 0.0) + jnp.log(1.0 + jnp.exp(-jnp.abs(z))))
            if valid is not None:
                log_fail = jnp.where(valid, log_fail, 0.0)
            hi = log_fail.astype(jnp.bfloat16)
            lo = (log_fail - hi.astype(jnp.float32)).astype(jnp.bfloat16)
            sums = (jnp.dot(hi, tri, preferred_element_type=jnp.float32)
                    + jnp.dot(lo, tri, preferred_element_type=jnp.float32))
            c = c_ref[g]
            w = jnp.exp(z + log_fail + sums[:, :blk] + c)
            if valid is not None:
                w = jnp.where(valid, w, 0.0)
            acc_ref[g] += jnp.dot(w.astype(v.dtype), v, preferred_element_type=jnp.float32)
            c_ref[g] = c + sums[:, blk:]

    def bound():
        return jnp.max(c_ref[:, :, :hd])

    c_ref[...] = jnp.zeros_like(c_ref)
    acc_ref[...] = jnp.zeros_like(acc_ref)
    visit(qi, valid_diag)

    def cond(state):
        j, worst = state
        return jnp.logical_and(j >= 0, worst > SB_SKIP_LOG)

    def body(state):
        j, _ = state
        visit(j, None)
        return j - 1, bound()

    lax.while_loop(cond, body, (qi - 1, bound()))
    for g in range(hps):
        o_ref[:, g * hd:(g + 1) * hd] = acc_ref[g].astype(o_ref.dtype)


def _sb_attention(qkv, batch, seq, heads, hd):
    blk = _blk(seq, 256)
    nq = seq // blk
    hps = 2 if heads % 2 == 0 else 1
    hg, w = heads // hps, hps * hd
    tri = np.concatenate([np.tril(np.ones((blk, blk), np.float32), -1), np.ones((blk, blk), np.float32)], axis=1)
    kern = functools.partial(_sb_kernel, blk=blk, hd=hd, hps=hps)
    return pl.pallas_call(
        kern, out_shape=jax.ShapeDtypeStruct((batch * seq, heads * hd), jnp.bfloat16), grid=(batch, hg, nq),
        in_specs=[pl.BlockSpec((blk, w), lambda b, h, i: (b * nq + i, h)),
                  pl.BlockSpec((seq, w), lambda b, h, i: (b, hg + h)),
                  pl.BlockSpec((seq, w), lambda b, h, i: (b, 2 * hg + h)),
                  pl.BlockSpec((blk, 2 * blk), lambda b, h, i: (0, 0))],
        out_specs=pl.BlockSpec((blk, w), lambda b, h, i: (b * nq + i, h)),
        scratch_shapes=[pltpu.VMEM((hps, blk, blk), jnp.float32), pltpu.VMEM((hps, blk, hd), jnp.float32)],
        compiler_params=_params("parallel", "parallel", "parallel"), name="stick_breaking_attention",
    )(qkv, qkv, qkv, jnp.asarray(tri, jnp.bfloat16))


def _da_kernel(lq1_ref, lk1_ref, lq2_ref, lk2_ref, g_ref, q1_ref, q2_ref, k1_ref, k2_ref, v_ref, o_ref,
               m_ref, l_ref, acc_ref, *, blk, hd, lam_init):
    qi = pl.program_id(2)
    row = lax.broadcasted_iota(jnp.int32, (blk, blk), 0)
    col = lax.broadcasted_iota(jnp.int32, (blk, blk), 1)
    causal_diag = col <= row
    qs = (q1_ref[...], q2_ref[...])
    ks = (k1_ref, k2_ref)

    m_ref[...] = jnp.full_like(m_ref, NEG_INF)
    l_ref[...] = jnp.zeros_like(l_ref)
    acc_ref[...] = jnp.zeros_like(acc_ref)

    def visit(j, mask):
        start = pl.multiple_of(j * blk, blk)
        v = v_ref[pl.ds(start, blk), :]
        for s in range(2):
            k = ks[s][pl.ds(start, blk), :]
            sc = lax.dot_general(qs[s], k, (((1,), (1,)), ((), ())), preferred_element_type=jnp.float32)
            if mask is not None:
                sc = jnp.where(mask, sc, NEG_INF)
            m_old = m_ref[s]
            m_new = jnp.maximum(m_old, jnp.max(sc, axis=-1, keepdims=True))
            p = jnp.exp2(sc - jnp.concatenate([m_new] * (blk // hd), axis=1))
            corr = jnp.exp2(m_old - m_new)
            l_ref[s] = corr * l_ref[s] + jnp.sum(p, axis=-1, keepdims=True)
            acc_ref[s] = (jnp.concatenate([corr, corr], axis=1) * acc_ref[s]
                          + jnp.dot(p.astype(v.dtype), v, preferred_element_type=jnp.float32))
            m_ref[s] = m_new

    def body(j, carry):
        visit(j, None)
        return carry

    lax.fori_loop(0, qi, body, 0)
    visit(qi, causal_diag)

    lam = (jnp.exp(jnp.sum(lq1_ref[...] * lk1_ref[...], axis=-1, keepdims=True))
           - jnp.exp(jnp.sum(lq2_ref[...] * lk2_ref[...], axis=-1, keepdims=True)) + lam_init)
    l1 = jnp.concatenate([l_ref[0], l_ref[0]], axis=1)
    l2 = jnp.concatenate([l_ref[1], l_ref[1]], axis=1)
    o = acc_ref[0] / l1 - lam * (acc_ref[1] / l2)
    o = o * lax.rsqrt(jnp.mean(o * o, axis=-1, keepdims=True) + LN_EPS) * g_ref[...] * (1.0 - lam_init)
    o_ref[...] = o.astype(o_ref.dtype)


def _da_attention(qkv, lambdas, subln_g, batch, seq, heads, hd, col0, lam_init):
    blk = _blk(seq, 512)
    nq = seq // blk
    qb, kb, vb = col0 // hd, col0 // hd + 2 * heads, col0 // (2 * hd) + 2 * heads
    vec = pl.BlockSpec((1, hd), lambda b, h, i: (0, 0))
    kern = functools.partial(_da_kernel, blk=blk, hd=hd, lam_init=lam_init)
    return pl.pallas_call(
        kern, out_shape=jax.ShapeDtypeStruct((batch * seq, heads * 2 * hd), jnp.bfloat16), grid=(batch, heads, nq),
        in_specs=[vec, vec, vec, vec, pl.BlockSpec((1, 2 * hd), lambda b, h, i: (0, 0)),
                  pl.BlockSpec((blk, hd), lambda b, h, i: (b * nq + i, qb + 2 * h)),
                  pl.BlockSpec((blk, hd), lambda b, h, i: (b * nq + i, qb + 2 * h + 1)),
                  pl.BlockSpec((seq, hd), lambda b, h, i: (b, kb + 2 * h)),
                  pl.BlockSpec((seq, hd), lambda b, h, i: (b, kb + 2 * h + 1)),
                  pl.BlockSpec((seq, 2 * hd), lambda b, h, i: (b, vb + h))],
        out_specs=pl.BlockSpec((blk, 2 * hd), lambda b, h, i: (b * nq + i, h)),
        scratch_shapes=[pltpu.VMEM((2, blk, hd), jnp.float32), pltpu.VMEM((2, blk, hd), jnp.float32),
                        pltpu.VMEM((2, blk, 2 * hd), jnp.float32)],
        compiler_params=_params("parallel", "parallel", "parallel"), name="differential_attention",
    )(*[l.reshape(1, hd) for l in lambdas], subln_g.reshape(1, 2 * hd), qkv, qkv, qkv, qkv, qkv)


def _extract_top(x, n):
    rows = lax.broadcasted_iota(jnp.int32, x.shape, 0)
    tops = []
    for _ in range(n):
        m = jnp.max(x, axis=0, keepdims=True)
        tops.append(m)
        first = jnp.min(jnp.where(x == m, rows, x.shape[0]), axis=0, keepdims=True)
        x = jnp.where(rows == first, NEG_INF, x)
    return tops


def _peer_topk_kernel(q_ref, keys_ref, s_ref, e_ref, tau_ref, *, n_keys, topk):
    half = q_ref.shape[1] // 2
    scores, tops = [], []
    for c in range(2):
        s = lax.dot_general(keys_ref[0, c], q_ref[:, c * half:(c + 1) * half], (((1,), (1,)), ((), ())),
                            preferred_element_type=jnp.float32)
        scores.append(s)
        tops.append(_extract_top(s, topk))
    top_b = jnp.concatenate(tops[1], axis=0)
    exp_b = jnp.exp(top_b - tops[1][0])
    cand = jnp.concatenate([tops[0][p] + top_b for p in range(topk)], axis=0)
    cand_e = jnp.concatenate([jnp.exp(tops[0][p] - tops[0][0]) * exp_b for p in range(topk)], axis=0)
    tau = _extract_top(cand, topk)[-1]
    z = jnp.sum(jnp.where(cand >= tau, cand_e, 0.0), axis=0, keepdims=True)
    s_ref[...] = jnp.concatenate(scores, axis=0)
    e_ref[...] = jnp.concatenate([jnp.exp(scores[0] - tops[0][0]) / z, jnp.exp(scores[1] - tops[1][0])], axis=0)
    tau_ref[0] = tau


def _peer_topk(hq, keys):
    t = hq.shape[0]
    heads, _, n_keys, half = keys.shape
    tn = _blk(t, 256)
    kern = functools.partial(_peer_topk_kernel, n_keys=n_keys, topk=PEER_TOPK)
    st = jax.ShapeDtypeStruct((heads * 2 * n_keys, t), jnp.float32)
    blk = pl.BlockSpec((2 * n_keys, tn), lambda i, h: (h, i))
    return pl.pallas_call(
        kern, out_shape=(st, st, jax.ShapeDtypeStruct((heads, 1, t), jnp.float32)), grid=(t // tn, heads),
        in_specs=[pl.BlockSpec((tn, 2 * half), lambda i, h: (i, h)),
                  pl.BlockSpec((1, 2, n_keys, half), lambda i, h: (h, 0, 0, 0))],
        out_specs=(blk, blk, pl.BlockSpec((1, 1, tn), lambda i, h: (h, 0, i))),
        compiler_params=_params("parallel", "parallel"), name="peer_topk",
    )(hq, keys)


def _peer_dense_kernel(h_ref, u_ref, v_ref, s_ref, e_ref, tau_ref, o_ref, ga_ref, *, heads, n_keys, n_blocks):
    j = pl.program_id(1)
    te = u_ref.shape[0]
    n_i = te // n_keys

    @pl.when(j == 0)
    def _():
        ga_ref[1] = jnp.zeros(ga_ref.shape[1:], ga_ref.dtype)
        o_ref[...] = jnp.zeros_like(o_ref)

    jb = jnp.minimum(j, n_blocks - 1)
    pre = lax.dot_general(u_ref[...], h_ref[...], (((1,), (1,)), ((), ())), preferred_element_type=jnp.float32)
    act = 0.5 * pre * (1.0 + lax.erf(pre * (2.0 ** -0.5)))
    parts = []
    for il in range(n_i):
        i_row = jb * n_i + il
        g = None
        for h in range(heads):
            base = h * 2 * n_keys
            a_row = s_ref[pl.ds(base + i_row, 1), :]
            ea_row = e_ref[pl.ds(base + i_row, 1), :]
            b_tile = s_ref[base + n_keys:base + 2 * n_keys, :]
            eb_tile = e_ref[base + n_keys:base + 2 * n_keys, :]
            term = jnp.where(a_row + b_tile >= tau_ref[h], ea_row * eb_tile, 0.0)
            g = term if g is None else g + term
        parts.append(g)
    gate = jnp.concatenate(parts, axis=0) if n_i > 1 else parts[0]
    prev = ga_ref[(j + 1) % 2]
    o_ref[...] += lax.dot_general(prev, v_ref[...], (((0,), (0,)), ((), ())), preferred_element_type=jnp.float32)
    ga_ref[j % 2] = (gate * act).astype(ga_ref.dtype)


def _peer_dense(h_bf, u_bf, v_bf, scores_t, exps_t, tau, heads, n_keys):
    t, d = h_bf.shape
    e = u_bf.shape[0]
    tm, te = _blk(t, 512), _blk(e, 512)
    nb = e // te
    kern = functools.partial(_peer_dense_kernel, heads=heads, n_keys=n_keys, n_blocks=nb)
    rows = scores_t.shape[0]
    once = pl.Buffered(1)
    return pl.pallas_call(
        kern, out_shape=jax.ShapeDtypeStruct((t, d), jnp.float32), grid=(t // tm, nb + 1),
        in_specs=[pl.BlockSpec((tm, d), lambda i, j: (i, 0), pipeline_mode=once),
                  pl.BlockSpec((te, d), lambda i, j: (jnp.minimum(j, nb - 1), 0)),
                  pl.BlockSpec((te, d), lambda i, j: (jnp.maximum(j - 1, 0), 0)),
                  pl.BlockSpec((rows, tm), lambda i, j: (0, i), pipeline_mode=once),
                  pl.BlockSpec((rows, tm), lambda i, j: (0, i), pipeline_mode=once),
                  pl.BlockSpec((heads, 1, tm), lambda i, j: (0, 0, i))],
        out_specs=pl.BlockSpec((tm, d), lambda i, j: (i, 0)),
        scratch_shapes=[pltpu.VMEM((2, te, tm), jnp.bfloat16)],
        compiler_params=_params("parallel", "arbitrary"), name="peer_dense",
    )(h_bf, u_bf, v_bf, scores_t, exps_t, tau)


def kernel(x, positions, w_in, b_gate, lambda_q1, lambda_k1, lambda_q2, lambda_k2, subln_g, w_sb_branch,
           w_da_branch, w_out, ln1_g, ln1_b, peer_w_q, peer_sub_keys, peer_u, peer_v, ln2_g, ln2_b):
    batch, seq, d = x.shape
    depth = w_in.shape[0]
    hd = lambda_q1.shape[-1]
    sb_w, da_w = w_sb_branch.shape[1], w_da_branch.shape[1]
    sb_heads, da_heads = sb_w // hd, da_w // (2 * hd)
    n_qkv = 3 * sb_w + 3 * da_w
    alpha = (2 * depth) ** 0.25
    bf = jnp.bfloat16
    t = batch * seq

    cos, sin = _rope_tables(positions, hd)
    col_scale = np.ones((1, n_qkv), np.float32)
    col_scale[:, :sb_w] = hd ** -0.5
    col_scale[:, 3 * sb_w:3 * sb_w + da_w] = hd ** -0.5 * math.log2(math.e)
    col_scale = jnp.asarray(col_scale)
    h = x.reshape(t, d)
    h_bf = h.astype(bf)
    for l in range(depth):
        lam_init = 0.8 - 0.6 * math.exp(-0.3 * l)
        w_in_bf = w_in[l].astype(bf)
        qkv = _qkv_proj(h_bf, w_in_bf, col_scale, cos, sin, n_qkv, (3 * sb_w, 3 * sb_w + 2 * da_w), hd)
        gate = _gate_proj(h_bf, w_in_bf, b_gate[l].reshape(1, -1), n_qkv, 2 * d)
        o_sb = _sb_attention(qkv, batch, seq, sb_heads, hd)
        o_da = _da_attention(qkv, (lambda_q1[l], lambda_k1[l], lambda_q2[l], lambda_k2[l]), subln_g[l],
                             batch, seq, da_heads, hd, 3 * sb_w, lam_init)
        merged = _merge(o_sb, o_da, w_sb_branch[l].astype(bf), w_da_branch[l].astype(bf), gate)
        mix = _matmul(merged, w_out[l].astype(bf), jnp.float32, "out_proj")
        h, h_bf = _residual_ln(h, mix, ln1_g[l], ln1_b[l], alpha)

        heads, _, n_keys, half = peer_sub_keys[l].shape
        hq = _matmul(h_bf, peer_w_q[l].astype(bf), bf, "peer_query")
        scores_t, exps_t, tau = _peer_topk(hq, peer_sub_keys[l].astype(bf))
        ffn = _peer_dense(h_bf, peer_u[l].astype(bf), peer_v[l].astype(bf), scores_t, exps_t, tau, heads, n_keys)
        h, h_bf = _residual_ln(h, ffn, ln2_g[l], ln2_b[l], alpha)
    return h.reshape(batch, seq, d)
```

```python
import functools
import math

import numpy as np
import jax
import jax.numpy as jnp
from jax import lax
from jax.experimental import pallas as pl
from jax.experimental.pallas import tpu as pltpu

LN_EPS = 1e-5
ROPE_THETA = 10000.0
PEER_TOPK = 16
V7X_VMEM_LIMIT_BYTES = 56 * 1024 * 1024
LANES = 128
MXU_TILE = 256
PEER_LINK_ROLLS = 1
NEG_INF = float("-inf")


def _params(*sem):
    return pltpu.CompilerParams(dimension_semantics=sem, vmem_limit_bytes=V7X_VMEM_LIMIT_BYTES)


def _blk(n, pref):
    b = min(n, pref)
    while n % b:
        b //= 2
    return b


def _rope_kernel(pos_ref, freq_ref, cos_ref, sin_ref):
    ang = pos_ref[...].astype(jnp.float32) * freq_ref[...]
    hd = freq_ref.shape[1]
    lane = lax.broadcasted_iota(jnp.int32, ang.shape, 1)
    cos_ref[...] = jnp.cos(ang)
    s = jnp.sin(ang)
    sin_ref[...] = jnp.where(lane < hd // 2, -s, s)


def _rope_tables(positions, hd):
    t = positions.size
    bm = _blk(t, 2048)
    inv = ROPE_THETA ** (-np.arange(0, hd, 2, dtype=np.float32) / np.float32(hd))
    freq = jnp.asarray(np.concatenate([inv, inv]).astype(np.float32).reshape(1, hd))
    out = jax.ShapeDtypeStruct((t, hd), jnp.float32)
    return pl.pallas_call(
        _rope_kernel, out_shape=(out, out), grid=(t // bm,),
        in_specs=[pl.BlockSpec((bm, 1), lambda i: (i, 0)), pl.BlockSpec((1, hd), lambda i: (0, 0))],
        out_specs=(pl.BlockSpec((bm, hd), lambda i: (i, 0)), pl.BlockSpec((bm, hd), lambda i: (i, 0))),
        compiler_params=_params("parallel"), name="rope_tables",
    )(positions.reshape(t, 1), freq)


def _qkv_kernel(x_ref, w_ref, cs_ref, cos_ref, sin_ref, o_ref, *, rope_lo, rope_hi, hd):
    acc = jnp.dot(x_ref[...], w_ref[...], preferred_element_type=jnp.float32) * cs_ref[...]
    j = pl.program_id(1)
    is_rope = jnp.logical_and(j >= rope_lo, j < rope_hi)

    @pl.when(is_rope)
    def _():
        c = cos_ref[...]
        s = sin_ref[...]
        for t in range(acc.shape[1] // hd):
            a = acc[:, t * hd:(t + 1) * hd]
            o_ref[:, t * hd:(t + 1) * hd] = (a * c + pltpu.roll(a, hd // 2, 1) * s).astype(o_ref.dtype)

    @pl.when(jnp.logical_not(is_rope))
    def _():
        o_ref[...] = acc.astype(o_ref.dtype)


def _qkv_proj(x, w, col_scale, cos, sin, n_out, rope_cols, hd):
    m, k = x.shape
    bm, bn = _blk(m, 1024), _blk(math.gcd(n_out, rope_cols[0], rope_cols[1]), 1024)
    kern = functools.partial(_qkv_kernel, rope_lo=rope_cols[0] // bn, rope_hi=rope_cols[1] // bn, hd=hd)
    return pl.pallas_call(
        kern, out_shape=jax.ShapeDtypeStruct((m, n_out), jnp.bfloat16), grid=(m // bm, n_out // bn),
        in_specs=[pl.BlockSpec((bm, k), lambda i, j: (i, 0)), pl.BlockSpec((k, bn), lambda i, j: (0, j)),
                  pl.BlockSpec((1, bn), lambda i, j: (0, j)),
                  pl.BlockSpec((bm, hd), lambda i, j: (i, 0)), pl.BlockSpec((bm, hd), lambda i, j: (i, 0))],
        out_specs=pl.BlockSpec((bm, bn), lambda i, j: (i, j)),
        compiler_params=_params("parallel", "parallel"), name="qkv_proj",
    )(x, w, col_scale, cos, sin)


def _gate_kernel(x_ref, w_ref, b_ref, o_ref):
    acc = jnp.dot(x_ref[...], w_ref[...], preferred_element_type=jnp.float32)
    o_ref[...] = jax.nn.sigmoid(acc + b_ref[...]).astype(o_ref.dtype)


def _gate_proj(x, w, bias, col0, n_out):
    m, k = x.shape
    bm, bn = _blk(m, 1024), _blk(math.gcd(n_out, col0), 1024)
    off = col0 // bn
    return pl.pallas_call(
        _gate_kernel, out_shape=jax.ShapeDtypeStruct((m, n_out), jnp.float32), grid=(m // bm, n_out // bn),
        in_specs=[pl.BlockSpec((bm, k), lambda i, j: (i, 0)), pl.BlockSpec((k, bn), lambda i, j: (0, j + off)),
                  pl.BlockSpec((1, bn), lambda i, j: (0, j))],
        out_specs=pl.BlockSpec((bm, bn), lambda i, j: (i, j)),
        compiler_params=_params("parallel", "parallel"), name="gate_proj",
    )(x, w, bias)


def _mm_kernel(a_ref, b_ref, o_ref):
    o_ref[...] = jnp.dot(a_ref[...], b_ref[...], preferred_element_type=jnp.float32).astype(o_ref.dtype)


def _matmul(a, b, out_dtype, name):
    m, k = a.shape
    n = b.shape[1]
    bm, bn = _blk(m, 1024), _blk(n, 1024)
    return pl.pallas_call(
        _mm_kernel, out_shape=jax.ShapeDtypeStruct((m, n), out_dtype), grid=(m // bm, n // bn),
        in_specs=[pl.BlockSpec((bm, k), lambda i, j: (i, 0)), pl.BlockSpec((k, bn), lambda i, j: (0, j))],
        out_specs=pl.BlockSpec((bm, bn), lambda i, j: (i, j)),
        compiler_params=_params("parallel", "parallel"), name=name,
    )(a, b)


def _merge_kernel(osb_ref, oda_ref, wsb_ref, wda_ref, gsb_ref, gda_ref, o_ref):
    ysb = jnp.dot(osb_ref[...], wsb_ref[...], preferred_element_type=jnp.float32)
    yda = jnp.dot(oda_ref[...], wda_ref[...], preferred_element_type=jnp.float32)
    o_ref[...] = (gsb_ref[...] * ysb + gda_ref[...] * yda).astype(o_ref.dtype)


def _merge(o_sb, o_da, w_sb, w_da, gate):
    m = o_sb.shape[0]
    d = w_sb.shape[1]
    bm, bn = _blk(m, 1024), _blk(d, 512)
    goff = d // bn
    return pl.pallas_call(
        _merge_kernel, out_shape=jax.ShapeDtypeStruct((m, d), jnp.bfloat16), grid=(m // bm, d // bn),
        in_specs=[pl.BlockSpec((bm, o_sb.shape[1]), lambda i, j: (i, 0)),
                  pl.BlockSpec((bm, o_da.shape[1]), lambda i, j: (i, 0)),
                  pl.BlockSpec((w_sb.shape[0], bn), lambda i, j: (0, j)),
                  pl.BlockSpec((w_da.shape[0], bn), lambda i, j: (0, j)),
                  pl.BlockSpec((bm, bn), lambda i, j: (i, j)),
                  pl.BlockSpec((bm, bn), lambda i, j: (i, j + goff))],
        out_specs=pl.BlockSpec((bm, bn), lambda i, j: (i, j)),
        compiler_params=_params("parallel", "parallel"), name="branch_merge",
    )(o_sb, o_da, w_sb, w_da, gate, gate)


def _ln_kernel(res_ref, y_ref, g_ref, b_ref, o_ref, obf_ref, *, alpha):
    z = alpha * res_ref[...] + y_ref[...]
    mu = jnp.mean(z, axis=-1, keepdims=True)
    zc = z - mu
    var = jnp.mean(zc * zc, axis=-1, keepdims=True)
    out = zc * lax.rsqrt(var + LN_EPS) * g_ref[...] + b_ref[...]
    o_ref[...] = out
    obf_ref[...] = out.astype(obf_ref.dtype)


def _residual_ln(res, y, g, b, alpha):
    m, d = res.shape
    bm = _blk(m, 256)
    row = pl.BlockSpec((bm, d), lambda i: (i, 0))
    vec = pl.BlockSpec((1, d), lambda i: (0, 0))
    return pl.pallas_call(
        functools.partial(_ln_kernel, alpha=alpha),
        out_shape=(jax.ShapeDtypeStruct((m, d), jnp.float32), jax.ShapeDtypeStruct((m, d), jnp.bfloat16)),
        grid=(m // bm,), in_specs=[row, row, vec, vec], out_specs=(row, row),
        compiler_params=_params("parallel"), name="residual_layernorm",
    )(res, y, g.reshape(1, d), b.reshape(1, d))


SB_SKIP_LOG = -100.0


def _sb_kernel(q_ref, k_ref, v_ref, tri_ref, o_ref, c_ref, acc_ref, *, blk, hd, hps):
    qi = pl.program_id(2)
    row = lax.broadcasted_iota(jnp.int32, (blk, blk), 0)
    col = lax.broadcasted_iota(jnp.int32, (blk, blk), 1)
    valid_diag = col < row
    tri = tri_ref[...]

    def visit(j, valid):
        start = pl.multiple_of(j * blk, blk)
        for g in range(hps):
            q = q_ref[:, g * hd:(g + 1) * hd]
            k = k_ref[pl.ds(start, blk), g * hd:(g + 1) * hd]
            v = v_ref[pl.ds(start, blk), g * hd:(g + 1) * hd]
            z = lax.dot_general(q, k, (((1,), (1,)), ((), ())), preferred_element_type=jnp.float32)
            log_fail = -(jnp.maximum(z, 0.0) + jnp.log(1.0 + jnp.exp(-jnp.abs(z))))
            if valid is not None:
                log_fail = jnp.where(valid, log_fail, 0.0)
            hi = log_fail.astype(jnp.bfloat16)
            lo = (log_fail - hi.astype(jnp.float32)).astype(jnp.bfloat16)
            sums = (jnp.dot(hi, tri, preferred_element_type=jnp.float32)
                    + jnp.dot(lo, tri, preferred_element_type=jnp.float32))
            c = c_ref[g]
            w = jnp.exp(z + log_fail + sums[:, :blk] + c)
            if valid is not None:
                w = jnp.where(valid, w, 0.0)
            acc_ref[g] += jnp.dot(w.astype(v.dtype), v, preferred_element_type=jnp.float32)
            c_ref[g] = c + sums[:, blk:]

    def bound():
        return jnp.max(c_ref[:, :, :hd])

    c_ref[...] = jnp.zeros_like(c_ref)
    acc_ref[...] = jnp.zeros_like(acc_ref)
    visit(qi, valid_diag)

    def cond(state):
        j, worst = state
        return jnp.logical_and(j >= 0, worst > SB_SKIP_LOG)

    def body(state):
        j, _ = state
        visit(j, None)
        return j - 1, bound()

    lax.while_loop(cond, body, (qi - 1, bound()))
    for g in range(hps):
        o_ref[:, g * hd:(g + 1) * hd] = acc_ref[g].astype(o_ref.dtype)


def _sb_attention(qkv, batch, seq, heads, hd):
    blk = _blk(seq, 256)
    nq = seq // blk
    hps = 2 if heads % 2 == 0 else 1
    hg, w = heads // hps, hps * hd
    tri = np.concatenate([np.tril(np.ones((blk, blk), np.float32), -1), np.ones((blk, blk), np.float32)], axis=1)
    kern = functools.partial(_sb_kernel, blk=blk, hd=hd, hps=hps)
    return pl.pallas_call(
        kern, out_shape=jax.ShapeDtypeStruct((batch * seq, heads * hd), jnp.bfloat16), grid=(batch, hg, nq),
        in_specs=[pl.BlockSpec((blk, w), lambda b, h, i: (b * nq + i, h)),
                  pl.BlockSpec((seq, w), lambda b, h, i: (b, hg + h)),
                  pl.BlockSpec((seq, w), lambda b, h, i: (b, 2 * hg + h)),
                  pl.BlockSpec((blk, 2 * blk), lambda b, h, i: (0, 0))],
        out_specs=pl.BlockSpec((blk, w), lambda b, h, i: (b * nq + i, h)),
        scratch_shapes=[pltpu.VMEM((hps, blk, blk), jnp.float32), pltpu.VMEM((hps, blk, hd), jnp.float32)],
        compiler_params=_params("parallel", "parallel", "parallel"), name="stick_breaking_attention",
    )(qkv, qkv, qkv, jnp.asarray(tri, jnp.bfloat16))


def _da_kernel(lq1_ref, lk1_ref, lq2_ref, lk2_ref, g_ref, q1_ref, q2_ref, k1_ref, k2_ref, v_ref, o_ref,
               m_ref, l_ref, acc_ref, *, blk, hd, lam_init):
    qi = pl.program_id(2)
    row = lax.broadcasted_iota(jnp.int32, (blk, blk), 0)
    col = lax.broadcasted_iota(jnp.int32, (blk, blk), 1)
    causal_diag = col <= row
    qs = (q1_ref[...], q2_ref[...])
    ks = (k1_ref, k2_ref)

    m_ref[...] = jnp.full_like(m_ref, NEG_INF)
    l_ref[...] = jnp.zeros_like(l_ref)
    acc_ref[...] = jnp.zeros_like(acc_ref)

    def visit(j, mask):
        start = pl.multiple_of(j * blk, blk)
        v = v_ref[pl.ds(start, blk), :]
        for s in range(2):
            k = ks[s][pl.ds(start, blk), :]
            sc = lax.dot_general(qs[s], k, (((1,), (1,)), ((), ())), preferred_element_type=jnp.float32)
            if mask is not None:
                sc = jnp.where(mask, sc, NEG_INF)
            m_old = m_ref[s]
            m_new = jnp.maximum(m_old, jnp.max(sc, axis=-1, keepdims=True))
            p = jnp.exp2(sc - jnp.concatenate([m_new] * (blk // hd), axis=1))
            corr = jnp.exp2(m_old - m_new)
            l_ref[s] = corr * l_ref[s] + jnp.sum(p, axis=-1, keepdims=True)
            acc_ref[s] = (jnp.concatenate([corr, corr], axis=1) * acc_ref[s]
                          + jnp.dot(p.astype(v.dtype), v, preferred_element_type=jnp.float32))
            m_ref[s] = m_new

    def body(j, carry):
        visit(j, None)
        return carry

    lax.fori_loop(0, qi, body, 0)
    visit(qi, causal_diag)

    lam = (jnp.exp(jnp.sum(lq1_ref[...] * lk1_ref[...], axis=-1, keepdims=True))
           - jnp.exp(jnp.sum(lq2_ref[...] * lk2_ref[...], axis=-1, keepdims=True)) + lam_init)
    l1 = jnp.concatenate([l_ref[0], l_ref[0]], axis=1)
    l2 = jnp.concatenate([l_ref[1], l_ref[1]], axis=1)
    o = acc_ref[0] / l1 - lam * (acc_ref[1] / l2)
    o = o * lax.rsqrt(jnp.mean(o * o, axis=-1, keepdims=True) + LN_EPS) * g_ref[...] * (1.0 - lam_init)
    o_ref[...] = o.astype(o_ref.dtype)


def _da_attention(qkv, lambdas, subln_g, batch, seq, heads, hd, col0, lam_init):
    blk = _blk(seq, 512)
    nq = seq // blk
    qb, kb, vb = col0 // hd, col0 // hd + 2 * heads, col0 // (2 * hd) + 2 * heads
    vec = pl.BlockSpec((1, hd), lambda b, h, i: (0, 0))
    kern = functools.partial(_da_kernel, blk=blk, hd=hd, lam_init=lam_init)
    return pl.pallas_call(
        kern, out_shape=jax.ShapeDtypeStruct((batch * seq, heads * 2 * hd), jnp.bfloat16), grid=(batch, heads, nq),
        in_specs=[vec, vec, vec, vec, pl.BlockSpec((1, 2 * hd), lambda b, h, i: (0, 0)),
                  pl.BlockSpec((blk, hd), lambda b, h, i: (b * nq + i, qb + 2 * h)),
                  pl.BlockSpec((blk, hd), lambda b, h, i: (b * nq + i, qb + 2 * h + 1)),
                  pl.BlockSpec((seq, hd), lambda b, h, i: (b, kb + 2 * h)),
                  pl.BlockSpec((seq, hd), lambda b, h, i: (b, kb + 2 * h + 1)),
                  pl.BlockSpec((seq, 2 * hd), lambda b, h, i: (b, vb + h))],
        out_specs=pl.BlockSpec((blk, 2 * hd), lambda b, h, i: (b * nq + i, h)),
        scratch_shapes=[pltpu.VMEM((2, blk, hd), jnp.float32), pltpu.VMEM((2, blk, hd), jnp.float32),
                        pltpu.VMEM((2, blk, 2 * hd), jnp.float32)],
        compiler_params=_params("parallel", "parallel", "parallel"), name="differential_attention",
    )(*[l.reshape(1, hd) for l in lambdas], subln_g.reshape(1, 2 * hd), qkv, qkv, qkv, qkv, qkv)


def _extract_top(x, n):
    rows = lax.broadcasted_iota(jnp.int32, x.shape, 0)
    tops = []
    for _ in range(n):
        m = jnp.max(x, axis=0, keepdims=True)
        tops.append(m)
        first = jnp.min(jnp.where(x == m, rows, x.shape[0]), axis=0, keepdims=True)
        x = jnp.where(rows == first, NEG_INF, x)
    return tops


def _peer_topk_kernel(q_ref, keys_ref, s_ref, e_ref, tau_ref, *, n_keys, topk):
    half = q_ref.shape[1] // 2
    scores, tops = [], []
    for c in range(2):
        s = lax.dot_general(keys_ref[0, c], q_ref[:, c * half:(c + 1) * half], (((1,), (1,)), ((), ())),
                            preferred_element_type=jnp.float32)
        scores.append(s)
        tops.append(_extract_top(s, topk))
    top_b = jnp.concatenate(tops[1], axis=0)
    exp_b = jnp.exp(top_b - tops[1][0])
    cand = jnp.concatenate([tops[0][p] + top_b for p in range(topk)], axis=0)
    cand_e = jnp.concatenate([jnp.exp(tops[0][p] - tops[0][0]) * exp_b for p in range(topk)], axis=0)
    tau = _extract_top(cand, topk)[-1]
    z = jnp.sum(jnp.where(cand >= tau, cand_e, 0.0), axis=0, keepdims=True)
    s_ref[...] = jnp.concatenate(scores, axis=0)
    e_ref[...] = jnp.concatenate([jnp.exp(scores[0] - tops[0][0]) / z, jnp.exp(scores[1] - tops[1][0])], axis=0)
    tau_ref[0] = tau


def _peer_topk(hq, keys):
    t = hq.shape[0]
    heads, _, n_keys, half = keys.shape
    tn = _blk(t, 256)
    kern = functools.partial(_peer_topk_kernel, n_keys=n_keys, topk=PEER_TOPK)
    st = jax.ShapeDtypeStruct((heads * 2 * n_keys, t), jnp.float32)
    blk = pl.BlockSpec((2 * n_keys, tn), lambda i, h: (h, i))
    return pl.pallas_call(
        kern, out_shape=(st, st, jax.ShapeDtypeStruct((heads, 1, t), jnp.float32)), grid=(t // tn, heads),
        in_specs=[pl.BlockSpec((tn, 2 * half), lambda i, h: (i, h)),
                  pl.BlockSpec((1, 2, n_keys, half), lambda i, h: (h, 0, 0, 0))],
        out_specs=(blk, blk, pl.BlockSpec((1, 1, tn), lambda i, h: (h, 0, i))),
        compiler_params=_params("parallel", "parallel"), name="peer_topk",
    )(hq, keys)


def _peer_dense_kernel(h_ref, u_ref, v_ref, s_ref, e_ref, tau_ref, o_ref, ga_ref, *, heads, n_keys, n_blocks):
    j = pl.program_id(1)

    @pl.when(j == 0)
    def _():
        ga_ref[1] = jnp.zeros(ga_ref.shape[1:], ga_ref.dtype)
        o_ref[...] = jnp.zeros_like(o_ref)

    for parity in range(2):
        pl.when(j % 2 == parity)(functools.partial(
            _peer_dense_step, h_ref, u_ref, v_ref, s_ref, e_ref, tau_ref, o_ref, ga_ref,
            jnp.minimum(j, n_blocks - 1), parity, heads, n_keys))


def _peer_dense_step(h_ref, u_ref, v_ref, s_ref, e_ref, tau_ref, o_ref, ga_ref, jb, slot_now, heads, n_keys):
    tm, te = ga_ref.shape[1:]
    d = o_ref.shape[1]
    n_i = te // n_keys
    mxu = MXU_TILE
    n_tok_tiles = tm // mxu

    def link_from(x):
        bits = pltpu.bitcast(x[0:8, 0:LANES], jnp.uint32)
        zero = lax.shift_right_logical(lax.shift_right_logical(bits, jnp.uint32(16)), jnp.uint32(16))
        return pltpu.bitcast(zero, jnp.float32)

    pre_tiles = {}

    def pre_tile(p):
        mi, ni = divmod(p, n_tok_tiles)
        pre_tiles[mi, ni] = lax.dot_general(
            u_ref[mi * mxu:(mi + 1) * mxu, :], h_ref[ni * mxu:(ni + 1) * mxu, :], (((1,), (1,)), ((), ())),
            preferred_element_type=jnp.float32)
        return link_from(pre_tiles[mi, ni])

    def value_chunk(c):
        cols = slice(c * mxu, (c + 1) * mxu)
        acc = o_ref[:, cols] + jnp.dot(ga_ref[1 - slot_now], v_ref[:, cols], preferred_element_type=jnp.float32)
        o_ref[:, cols] = acc
        return link_from(acc)

    n_pre = (te // mxu) * n_tok_tiles
    chunks = [functools.partial(pre_tile, p) for p in range(n_pre)]
    chunks += [functools.partial(value_chunk, c) for c in range(d // mxu)]
    results = []

    def finalize(il):
        mi, off = divmod(il * n_keys, mxu)
        while any((mi, ni) not in pre_tiles for ni in range(n_tok_tiles)):
            results.append(chunks.pop(0)())
        p = jnp.concatenate([pre_tiles[mi, ni][off:off + n_keys, :] for ni in range(n_tok_tiles)], axis=1)
        act = 0.5 * p * (1.0 + lax.erf(p * (2.0 ** -0.5)))
        ga_ref[slot_now, :, il * n_keys:(il + 1) * n_keys] = (gates.pop(il) * act).astype(ga_ref.dtype).T

    pieces = [(il, h) for il in range(n_i) for h in range(heads)]
    costs = [mxu * d * mxu] * n_pre + [tm * te * mxu] * (d // mxu)
    ends = [sum(costs[:c + 1]) for c in range(len(costs))]
    slot = ends[-1] / len(pieces)
    linked = 0
    gates = {}
    link = link_from(s_ref[0:8, 0:LANES])
    for k, (il, h) in enumerate(pieces):
        while len(results) < len(costs) and ends[len(results)] - costs[len(results)] < (k + 1) * slot:
            results.append(chunks.pop(0)())
        while linked < len(results) and ends[linked] <= k * slot:
            link, linked = link + results[linked], linked + 1
        for _ in range(PEER_LINK_ROLLS):
            link = pltpu.roll(link, 1, 1)
        base = h * 2 * n_keys
        a_row = s_ref[pl.ds(base + jb * n_i + il, 1), :] + jnp.concatenate([link[0:1, :]] * (tm // LANES), axis=1)
        ea_row = e_ref[pl.ds(base + jb * n_i + il, 1), :]
        b_tile = s_ref[base + n_keys:base + 2 * n_keys, :]
        eb_tile = e_ref[base + n_keys:base + 2 * n_keys, :]
        term = jnp.where(a_row + b_tile >= tau_ref[h], ea_row * eb_tile, 0.0)
        gates[il] = term if h == 0 else gates[il] + term
        if h == heads - 1:
            finalize(il)
    while chunks:
        chunks.pop(0)()


def _peer_dense(h_bf, u_bf, v_bf, scores_t, exps_t, tau, heads, n_keys):
    t, d = h_bf.shape
    e = u_bf.shape[0]
    tm, te = _blk(t, 512), _blk(e, 512)
    nb = e // te
    kern = functools.partial(_peer_dense_kernel, heads=heads, n_keys=n_keys, n_blocks=nb)
    rows = scores_t.shape[0]
    once = pl.Buffered(1)
    return pl.pallas_call(
        kern, out_shape=jax.ShapeDtypeStruct((t, d), jnp.float32), grid=(t // tm, nb + 1),
        in_specs=[pl.BlockSpec((tm, d), lambda i, j: (i, 0), pipeline_mode=once),
                  pl.BlockSpec((te, d), lambda i, j: (jnp.minimum(j, nb - 1), 0)),
                  pl.BlockSpec((te, d), lambda i, j: (jnp.maximum(j - 1, 0), 0)),
                  pl.BlockSpec((rows, tm), lambda i, j: (0, i), pipeline_mode=once),
                  pl.BlockSpec((rows, tm), lambda i, j: (0, i), pipeline_mode=once),
                  pl.BlockSpec((heads, 1, tm), lambda i, j: (0, 0, i))],
        out_specs=pl.BlockSpec((tm, d), lambda i, j: (i, 0)),
        scratch_shapes=[pltpu.VMEM((2, tm, te), jnp.bfloat16)],
        compiler_params=_params("parallel", "arbitrary"), name="peer_dense",
    )(h_bf, u_bf, v_bf, scores_t, exps_t, tau)


def kernel(x, positions, w_in, b_gate, lambda_q1, lambda_k1, lambda_q2, lambda_k2, subln_g, w_sb_branch,
           w_da_branch, w_out, ln1_g, ln1_b, peer_w_q, peer_sub_keys, peer_u, peer_v, ln2_g, ln2_b):
    batch, seq, d = x.shape
    depth = w_in.shape[0]
    hd = lambda_q1.shape[-1]
    sb_w, da_w = w_sb_branch.shape[1], w_da_branch.shape[1]
    sb_heads, da_heads = sb_w // hd, da_w // (2 * hd)
    n_qkv = 3 * sb_w + 3 * da_w
    alpha = (2 * depth) ** 0.25
    bf = jnp.bfloat16
    t = batch * seq

    cos, sin = _rope_tables(positions, hd)
    col_scale = np.ones((1, n_qkv), np.float32)
    col_scale[:, :sb_w] = hd ** -0.5
    col_scale[:, 3 * sb_w:3 * sb_w + da_w] = hd ** -0.5 * math.log2(math.e)
    col_scale = jnp.asarray(col_scale)
    h = x.reshape(t, d)
    h_bf = h.astype(bf)
    for l in range(depth):
        lam_init = 0.8 - 0.6 * math.exp(-0.3 * l)
        w_in_bf = w_in[l].astype(bf)
        qkv = _qkv_proj(h_bf, w_in_bf, col_scale, cos, sin, n_qkv, (3 * sb_w, 3 * sb_w + 2 * da_w), hd)
        gate = _gate_proj(h_bf, w_in_bf, b_gate[l].reshape(1, -1), n_qkv, 2 * d)
        o_sb = _sb_attention(qkv, batch, seq, sb_heads, hd)
        o_da = _da_attention(qkv, (lambda_q1[l], lambda_k1[l], lambda_q2[l], lambda_k2[l]), subln_g[l],
                             batch, seq, da_heads, hd, 3 * sb_w, lam_init)
        merged = _merge(o_sb, o_da, w_sb_branch[l].astype(bf), w_da_branch[l].astype(bf), gate)
        mix = _matmul(merged, w_out[l].astype(bf), jnp.float32, "out_proj")
        h, h_bf = _residual_ln(h, mix, ln1_g[l], ln1_b[l], alpha)

        heads, _, n_keys, half = peer_sub_keys[l].shape
        hq = _matmul(h_bf, peer_w_q[l].astype(bf), bf, "peer_query")
        scores_t, exps_t, tau = _peer_topk(hq, peer_sub_keys[l].astype(bf))
        ffn = _peer_dense(h_bf, peer_u[l].astype(bf), peer_v[l].astype(bf), scores_t, exps_t, tau, heads, n_keys)
        h, h_bf = _residual_ln(h, ffn, ln2_g[l], ln2_b[l], alpha)
    return h.reshape(batch, seq, d)
```

```python
import functools
import math

import numpy as np
import jax
import jax.numpy as jnp
from jax import lax
from jax.experimental import pallas as pl
from jax.experimental.pallas import tpu as pltpu

LN_EPS = 1e-5
ROPE_THETA = 10000.0
PEER_TOPK = 16
V7X_VMEM_LIMIT_BYTES = 56 * 1024 * 1024
LANES = 128
MXU_TILE = 256
DA_ROW_SPLIT = 2
PEER_KEY_SPLIT = 4
PEER_LINK_ROLLS = 1
NEG_INF = float("-inf")


def _params(*sem):
    return pltpu.CompilerParams(dimension_semantics=sem, vmem_limit_bytes=V7X_VMEM_LIMIT_BYTES)


def _blk(n, pref):
    b = min(n, pref)
    while n % b:
        b //= 2
    return b


def _rope_kernel(pos_ref, freq_ref, cos_ref, sin_ref):
    ang = pos_ref[...].astype(jnp.float32) * freq_ref[...]
    hd = freq_ref.shape[1]
    lane = lax.broadcasted_iota(jnp.int32, ang.shape, 1)
    cos_ref[...] = jnp.cos(ang)
    s = jnp.sin(ang)
    sin_ref[...] = jnp.where(lane < hd // 2, -s, s)


def _rope_tables(positions, hd):
    t = positions.size
    bm = _blk(t, 2048)
    inv = ROPE_THETA ** (-np.arange(0, hd, 2, dtype=np.float32) / np.float32(hd))
    freq = jnp.asarray(np.concatenate([inv, inv]).astype(np.float32).reshape(1, hd))
    out = jax.ShapeDtypeStruct((t, hd), jnp.float32)
    return pl.pallas_call(
        _rope_kernel, out_shape=(out, out), grid=(t // bm,),
        in_specs=[pl.BlockSpec((bm, 1), lambda i: (i, 0)), pl.BlockSpec((1, hd), lambda i: (0, 0))],
        out_specs=(pl.BlockSpec((bm, hd), lambda i: (i, 0)), pl.BlockSpec((bm, hd), lambda i: (i, 0))),
        compiler_params=_params("parallel"), name="rope_tables",
    )(positions.reshape(t, 1), freq)


def _qkv_kernel(x_ref, w_ref, cs_ref, cos_ref, sin_ref, o_ref, *, rope_lo, rope_hi, hd):
    acc = jnp.dot(x_ref[...], w_ref[...], preferred_element_type=jnp.float32) * cs_ref[...]
    j = pl.program_id(1)
    is_rope = jnp.logical_and(j >= rope_lo, j < rope_hi)

    @pl.when(is_rope)
    def _():
        c = cos_ref[...]
        s = sin_ref[...]
        for t in range(acc.shape[1] // hd):
            a = acc[:, t * hd:(t + 1) * hd]
            o_ref[:, t * hd:(t + 1) * hd] = (a * c + pltpu.roll(a, hd // 2, 1) * s).astype(o_ref.dtype)

    @pl.when(jnp.logical_not(is_rope))
    def _():
        o_ref[...] = acc.astype(o_ref.dtype)


def _qkv_proj(x, w, col_scale, cos, sin, n_out, rope_cols, hd):
    m, k = x.shape
    bm, bn = _blk(m, 1024), _blk(math.gcd(n_out, rope_cols[0], rope_cols[1]), 1024)
    kern = functools.partial(_qkv_kernel, rope_lo=rope_cols[0] // bn, rope_hi=rope_cols[1] // bn, hd=hd)
    return pl.pallas_call(
        kern, out_shape=jax.ShapeDtypeStruct((m, n_out), jnp.bfloat16), grid=(m // bm, n_out // bn),
        in_specs=[pl.BlockSpec((bm, k), lambda i, j: (i, 0)), pl.BlockSpec((k, bn), lambda i, j: (0, j)),
                  pl.BlockSpec((1, bn), lambda i, j: (0, j)),
                  pl.BlockSpec((bm, hd), lambda i, j: (i, 0)), pl.BlockSpec((bm, hd), lambda i, j: (i, 0))],
        out_specs=pl.BlockSpec((bm, bn), lambda i, j: (i, j)),
        compiler_params=_params("parallel", "parallel"), name="qkv_proj",
    )(x, w, col_scale, cos, sin)


def _gate_kernel(x_ref, w_ref, b_ref, o_ref):
    acc = jnp.dot(x_ref[...], w_ref[...], preferred_element_type=jnp.float32)
    o_ref[...] = jax.nn.sigmoid(acc + b_ref[...]).astype(o_ref.dtype)


def _gate_proj(x, w, bias, col0, n_out):
    m, k = x.shape
    bm, bn = _blk(m, 1024), _blk(math.gcd(n_out, col0), 1024)
    off = col0 // bn
    return pl.pallas_call(
        _gate_kernel, out_shape=jax.ShapeDtypeStruct((m, n_out), jnp.float32), grid=(m // bm, n_out // bn),
        in_specs=[pl.BlockSpec((bm, k), lambda i, j: (i, 0)), pl.BlockSpec((k, bn), lambda i, j: (0, j + off)),
                  pl.BlockSpec((1, bn), lambda i, j: (0, j))],
        out_specs=pl.BlockSpec((bm, bn), lambda i, j: (i, j)),
        compiler_params=_params("parallel", "parallel"), name="gate_proj",
    )(x, w, bias)


def _mm_kernel(a_ref, b_ref, o_ref):
    o_ref[...] = jnp.dot(a_ref[...], b_ref[...], preferred_element_type=jnp.float32).astype(o_ref.dtype)


def _matmul(a, b, out_dtype, name):
    m, k = a.shape
    n = b.shape[1]
    bm, bn = _blk(m, 1024), _blk(n, 1024)
    return pl.pallas_call(
        _mm_kernel, out_shape=jax.ShapeDtypeStruct((m, n), out_dtype), grid=(m // bm, n // bn),
        in_specs=[pl.BlockSpec((bm, k), lambda i, j: (i, 0)), pl.BlockSpec((k, bn), lambda i, j: (0, j))],
        out_specs=pl.BlockSpec((bm, bn), lambda i, j: (i, j)),
        compiler_params=_params("parallel", "parallel"), name=name,
    )(a, b)


def _merge_kernel(osb_ref, oda_ref, wsb_ref, wda_ref, gsb_ref, gda_ref, o_ref):
    ysb = jnp.dot(osb_ref[...], wsb_ref[...], preferred_element_type=jnp.float32)
    yda = jnp.dot(oda_ref[...], wda_ref[...], preferred_element_type=jnp.float32)
    o_ref[...] = (gsb_ref[...] * ysb + gda_ref[...] * yda).astype(o_ref.dtype)


def _merge(o_sb, o_da, w_sb, w_da, gate):
    m = o_sb.shape[0]
    d = w_sb.shape[1]
    bm, bn = _blk(m, 1024), _blk(d, 512)
    goff = d // bn
    return pl.pallas_call(
        _merge_kernel, out_shape=jax.ShapeDtypeStruct((m, d), jnp.bfloat16), grid=(m // bm, d // bn),
        in_specs=[pl.BlockSpec((bm, o_sb.shape[1]), lambda i, j: (i, 0)),
                  pl.BlockSpec((bm, o_da.shape[1]), lambda i, j: (i, 0)),
                  pl.BlockSpec((w_sb.shape[0], bn), lambda i, j: (0, j)),
                  pl.BlockSpec((w_da.shape[0], bn), lambda i, j: (0, j)),
                  pl.BlockSpec((bm, bn), lambda i, j: (i, j)),
                  pl.BlockSpec((bm, bn), lambda i, j: (i, j + goff))],
        out_specs=pl.BlockSpec((bm, bn), lambda i, j: (i, j)),
        compiler_params=_params("parallel", "parallel"), name="branch_merge",
    )(o_sb, o_da, w_sb, w_da, gate, gate)


def _ln_kernel(res_ref, y_ref, g_ref, b_ref, o_ref, obf_ref, *, alpha):
    z = alpha * res_ref[...] + y_ref[...]
    mu = jnp.mean(z, axis=-1, keepdims=True)
    zc = z - mu
    var = jnp.mean(zc * zc, axis=-1, keepdims=True)
    out = zc * lax.rsqrt(var + LN_EPS) * g_ref[...] + b_ref[...]
    o_ref[...] = out
    obf_ref[...] = out.astype(obf_ref.dtype)


def _residual_ln(res, y, g, b, alpha):
    m, d = res.shape
    bm = _blk(m, 256)
    row = pl.BlockSpec((bm, d), lambda i: (i, 0))
    vec = pl.BlockSpec((1, d), lambda i: (0, 0))
    return pl.pallas_call(
        functools.partial(_ln_kernel, alpha=alpha),
        out_shape=(jax.ShapeDtypeStruct((m, d), jnp.float32), jax.ShapeDtypeStruct((m, d), jnp.bfloat16)),
        grid=(m // bm,), in_specs=[row, row, vec, vec], out_specs=(row, row),
        compiler_params=_params("parallel"), name="residual_layernorm",
    )(res, y, g.reshape(1, d), b.reshape(1, d))


SB_SKIP_LOG = -100.0


def _sb_kernel(q_ref, k_ref, v_ref, tri_ref, o_ref, c_ref, acc_ref, *, blk, hd, hps):
    qi = pl.program_id(2)
    row = lax.broadcasted_iota(jnp.int32, (blk, blk), 0)
    col = lax.broadcasted_iota(jnp.int32, (blk, blk), 1)
    valid_diag = col < row
    tri = tri_ref[...]

    def visit(j, valid):
        start = pl.multiple_of(j * blk, blk)
        for g in range(hps):
            q = q_ref[:, g * hd:(g + 1) * hd]
            k = k_ref[pl.ds(start, blk), g * hd:(g + 1) * hd]
            v = v_ref[pl.ds(start, blk), g * hd:(g + 1) * hd]
            z = lax.dot_general(q, k, (((1,), (1,)), ((), ())), preferred_element_type=jnp.float32)
            log_fail = -(jnp.maximum(z, 0.0) + jnp.log(1.0 + jnp.exp(-jnp.abs(z))))
            if valid is not None:
                log_fail = jnp.where(valid, log_fail, 0.0)
            hi = log_fail.astype(jnp.bfloat16)
            lo = (log_fail - hi.astype(jnp.float32)).astype(jnp.bfloat16)
            sums = (jnp.dot(hi, tri, preferred_element_type=jnp.float32)
                    + jnp.dot(lo, tri, preferred_element_type=jnp.float32))
            c = c_ref[g]
            w = jnp.exp(z + log_fail + sums[:, :blk] + c)
            if valid is not None:
                w = jnp.where(valid, w, 0.0)
            acc_ref[g] += jnp.dot(w.astype(v.dtype), v, preferred_element_type=jnp.float32)
            c_ref[g] = c + sums[:, blk:]

    def bound():
        return jnp.max(c_ref[:, :, :hd])

    c_ref[...] = jnp.zeros_like(c_ref)
    acc_ref[...] = jnp.zeros_like(acc_ref)
    visit(qi, valid_diag)

    def cond(state):
        j, worst = state
        return jnp.logical_and(j >= 0, worst > SB_SKIP_LOG)

    def body(state):
        j, _ = state
        visit(j, None)
        return j - 1, bound()

    lax.while_loop(cond, body, (qi - 1, bound()))
    for g in range(hps):
        o_ref[:, g * hd:(g + 1) * hd] = acc_ref[g].astype(o_ref.dtype)


def _sb_attention(qkv, batch, seq, heads, hd):
    blk = _blk(seq, 256)
    nq = seq // blk
    hps = 2 if heads % 2 == 0 else 1
    hg, w = heads // hps, hps * hd
    tri = np.concatenate([np.tril(np.ones((blk, blk), np.float32), -1), np.ones((blk, blk), np.float32)], axis=1)
    kern = functools.partial(_sb_kernel, blk=blk, hd=hd, hps=hps)
    return pl.pallas_call(
        kern, out_shape=jax.ShapeDtypeStruct((batch * seq, heads * hd), jnp.bfloat16), grid=(batch, hg, nq),
        in_specs=[pl.BlockSpec((blk, w), lambda b, h, i: (b * nq + i, h)),
                  pl.BlockSpec((seq, w), lambda b, h, i: (b, hg + h)),
                  pl.BlockSpec((seq, w), lambda b, h, i: (b, 2 * hg + h)),
                  pl.BlockSpec((blk, 2 * blk), lambda b, h, i: (0, 0))],
        out_specs=pl.BlockSpec((blk, w), lambda b, h, i: (b * nq + i, h)),
        scratch_shapes=[pltpu.VMEM((hps, blk, blk), jnp.float32), pltpu.VMEM((hps, blk, hd), jnp.float32)],
        compiler_params=_params("parallel", "parallel", "parallel"), name="stick_breaking_attention",
    )(qkv, qkv, qkv, jnp.asarray(tri, jnp.bfloat16))


def _da_kernel(lq1_ref, lk1_ref, lq2_ref, lk2_ref, g_ref, q1_ref, q2_ref, k1_ref, k2_ref, v_ref, o_ref,
               m_ref, l_ref, acc_ref, *, blk, hd, lam_init):
    qi = pl.program_id(2)
    row = lax.broadcasted_iota(jnp.int32, (blk, blk), 0)
    col = lax.broadcasted_iota(jnp.int32, (blk, blk), 1)
    causal_diag = col <= row
    qs = (q1_ref[...], q2_ref[...])
    ks = (k1_ref, k2_ref)

    m_ref[...] = jnp.full_like(m_ref, NEG_INF)
    l_ref[...] = jnp.zeros_like(l_ref)
    acc_ref[...] = jnp.zeros_like(acc_ref)

    rows_per = blk // DA_ROW_SPLIT

    def visit(j, mask):
        start = pl.multiple_of(j * blk, blk)
        v = v_ref[pl.ds(start, blk), :]
        link = None
        for r in range(DA_ROW_SPLIT):
            rows = slice(r * rows_per, (r + 1) * rows_per)
            for s in range(2):
                k = ks[s][pl.ds(start, blk), :]
                sc = lax.dot_general(qs[s][rows], k, (((1,), (1,)), ((), ())), preferred_element_type=jnp.float32)
                if mask is not None:
                    sc = jnp.where(mask[rows], sc, NEG_INF)
                m_old = m_ref[s, rows]
                if link is not None:
                    m_old = m_old + link
                m_new = jnp.maximum(m_old, jnp.max(sc, axis=-1, keepdims=True))
                p = jnp.exp2(sc - jnp.concatenate([m_new] * (blk // hd), axis=1))
                corr = jnp.exp2(m_old - m_new)
                l_ref[s, rows] = corr * l_ref[s, rows] + jnp.sum(p, axis=-1, keepdims=True)
                acc_ref[s, rows] = (jnp.concatenate([corr, corr], axis=1) * acc_ref[s, rows]
                                    + jnp.dot(p.astype(v.dtype), v, preferred_element_type=jnp.float32))
                m_ref[s, rows] = m_new
                bits = pltpu.bitcast(p[0:8, 0:hd], jnp.uint32)
                zero = lax.shift_right_logical(lax.shift_right_logical(bits, jnp.uint32(16)), jnp.uint32(16))
                link = pltpu.bitcast(zero, jnp.float32)[0:1, :]

    def body(j, carry):
        visit(j, None)
        return carry

    lax.fori_loop(0, qi, body, 0)
    visit(qi, causal_diag)

    lam = (jnp.exp(jnp.sum(lq1_ref[...] * lk1_ref[...], axis=-1, keepdims=True))
           - jnp.exp(jnp.sum(lq2_ref[...] * lk2_ref[...], axis=-1, keepdims=True)) + lam_init)
    l1 = jnp.concatenate([l_ref[0], l_ref[0]], axis=1)
    l2 = jnp.concatenate([l_ref[1], l_ref[1]], axis=1)
    o = acc_ref[0] / l1 - lam * (acc_ref[1] / l2)
    o = o * lax.rsqrt(jnp.mean(o * o, axis=-1, keepdims=True) + LN_EPS) * g_ref[...] * (1.0 - lam_init)
    o_ref[...] = o.astype(o_ref.dtype)


def _da_attention(qkv, lambdas, subln_g, batch, seq, heads, hd, col0, lam_init):
    blk = _blk(seq, 512)
    nq = seq // blk
    qb, kb, vb = col0 // hd, col0 // hd + 2 * heads, col0 // (2 * hd) + 2 * heads
    vec = pl.BlockSpec((1, hd), lambda b, h, i: (0, 0))
    kern = functools.partial(_da_kernel, blk=blk, hd=hd, lam_init=lam_init)
    return pl.pallas_call(
        kern, out_shape=jax.ShapeDtypeStruct((batch * seq, heads * 2 * hd), jnp.bfloat16), grid=(batch, heads, nq),
        in_specs=[vec, vec, vec, vec, pl.BlockSpec((1, 2 * hd), lambda b, h, i: (0, 0)),
                  pl.BlockSpec((blk, hd), lambda b, h, i: (b * nq + i, qb + 2 * h)),
                  pl.BlockSpec((blk, hd), lambda b, h, i: (b * nq + i, qb + 2 * h + 1)),
                  pl.BlockSpec((seq, hd), lambda b, h, i: (b, kb + 2 * h)),
                  pl.BlockSpec((seq, hd), lambda b, h, i: (b, kb + 2 * h + 1)),
                  pl.BlockSpec((seq, 2 * hd), lambda b, h, i: (b, vb + h))],
        out_specs=pl.BlockSpec((blk, 2 * hd), lambda b, h, i: (b * nq + i, h)),
        scratch_shapes=[pltpu.VMEM((2, blk, hd), jnp.float32), pltpu.VMEM((2, blk, hd), jnp.float32),
                        pltpu.VMEM((2, blk, 2 * hd), jnp.float32)],
        compiler_params=_params("parallel", "parallel", "parallel"), name="differential_attention",
    )(*[l.reshape(1, hd) for l in lambdas], subln_g.reshape(1, 2 * hd), qkv, qkv, qkv, qkv, qkv)


def _extract_top(x, n):
    rows = lax.broadcasted_iota(jnp.int32, x.shape, 0).astype(jnp.float32)
    tops = []
    for _ in range(n):
        m = jnp.max(x, axis=0, keepdims=True)
        tops.append(m)
        first = jnp.min(jnp.where(x == m, rows, float(x.shape[0])), axis=0, keepdims=True)
        x = jnp.where(rows == first, NEG_INF, x)
    return tops


def _pair_candidates(top_a, top_b, combine):
    k = len(top_a)
    sub = 8
    b_all = jnp.concatenate(top_b, axis=0)
    pieces, masks, singles = [], [], []
    for p in range(k):
        n_q = k // (p + 1)
        if n_q == 1:
            singles.append(top_a[p])
            continue
        rows = -(-n_q // sub) * sub
        pieces.append(combine(top_a[p], b_all[0:rows]))
        masks.append(lax.broadcasted_iota(jnp.int32, (rows, 1), 0) < n_q)
    if singles:
        pieces.append(combine(jnp.concatenate(singles, axis=0), top_b[0]))
        masks.append(None)
    return pieces, masks


def _peer_topk_kernel(q_ref, keys_ref, s_ref, e_ref, tau_ref, *, n_keys, topk):
    half = q_ref.shape[1] // 2
    scores, tops = [], []
    for c in range(2):
        s = lax.dot_general(keys_ref[0, c], q_ref[:, c * half:(c + 1) * half], (((1,), (1,)), ((), ())),
                            preferred_element_type=jnp.float32)
        scores.append(s)
        tops.append(_extract_top(s, topk))
    exps = [[jnp.exp(t - top[0]) for t in top] for top in tops]
    sums, masks = _pair_candidates(tops[0], tops[1], jnp.add)
    prods, _ = _pair_candidates(exps[0], exps[1], jnp.multiply)
    cand = jnp.concatenate([s if m is None else jnp.where(m, s, NEG_INF) for s, m in zip(sums, masks)], axis=0)
    cand_e = jnp.concatenate(prods, axis=0)
    tau = _extract_top(cand, topk)[-1]
    z = jnp.sum(jnp.where(cand >= tau, cand_e, 0.0), axis=0, keepdims=True)
    s_ref[...] = jnp.concatenate(scores, axis=0)
    e_ref[...] = jnp.concatenate([jnp.exp(scores[0] - tops[0][0]) / z, jnp.exp(scores[1] - tops[1][0])], axis=0)
    tau_ref[0] = tau


def _peer_topk(hq, keys):
    t = hq.shape[0]
    heads, _, n_keys, half = keys.shape
    tn = _blk(t, 512)
    kern = functools.partial(_peer_topk_kernel, n_keys=n_keys, topk=PEER_TOPK)
    st = jax.ShapeDtypeStruct((heads * 2 * n_keys, t), jnp.float32)
    blk = pl.BlockSpec((2 * n_keys, tn), lambda i, h: (h, i))
    return pl.pallas_call(
        kern, out_shape=(st, st, jax.ShapeDtypeStruct((heads, 1, t), jnp.float32)), grid=(t // tn, heads),
        in_specs=[pl.BlockSpec((tn, 2 * half), lambda i, h: (i, h)),
                  pl.BlockSpec((1, 2, n_keys, half), lambda i, h: (h, 0, 0, 0))],
        out_specs=(blk, blk, pl.BlockSpec((1, 1, tn), lambda i, h: (h, 0, i))),
        compiler_params=_params("parallel", "parallel"), name="peer_topk",
    )(hq, keys)


def _peer_dense_kernel(h_ref, u_ref, v_ref, s_ref, e_ref, tau_ref, o_ref, ga_ref, *, heads, n_keys, n_blocks):
    j = pl.program_id(1)

    @pl.when(j == 0)
    def _():
        ga_ref[1] = jnp.zeros(ga_ref.shape[1:], ga_ref.dtype)
        o_ref[...] = jnp.zeros_like(o_ref)

    for parity in range(2):
        pl.when(j % 2 == parity)(functools.partial(
            _peer_dense_step, h_ref, u_ref, v_ref, s_ref, e_ref, tau_ref, o_ref, ga_ref,
            jnp.minimum(j, n_blocks - 1), parity, heads, n_keys))


def _peer_dense_step(h_ref, u_ref, v_ref, s_ref, e_ref, tau_ref, o_ref, ga_ref, jb, slot_now, heads, n_keys):
    tm, te = ga_ref.shape[1:]
    d = o_ref.shape[1]
    n_i = te // n_keys
    mxu = MXU_TILE
    n_tok_tiles = tm // mxu

    def link_from(x):
        bits = pltpu.bitcast(x[0:8, 0:LANES], jnp.uint32)
        zero = lax.shift_right_logical(lax.shift_right_logical(bits, jnp.uint32(16)), jnp.uint32(16))
        return pltpu.bitcast(zero, jnp.float32)

    pre_tiles = {}

    k_parts = PEER_KEY_SPLIT
    kw = d // k_parts
    done_tiles = set()

    def pre_tile(p, kk):
        mi, ni = divmod(p, n_tok_tiles)
        part = lax.dot_general(
            u_ref[mi * mxu:(mi + 1) * mxu, kk * kw:(kk + 1) * kw], h_ref[ni * mxu:(ni + 1) * mxu, kk * kw:(kk + 1) * kw],
            (((1,), (1,)), ((), ())), preferred_element_type=jnp.float32)
        pre_tiles[mi, ni] = part if kk == 0 else pre_tiles[mi, ni] + part
        if kk == k_parts - 1:
            done_tiles.add((mi, ni))
        return link_from(part)

    def value_chunk(c):
        cols = slice(c * mxu, (c + 1) * mxu)
        acc = o_ref[:, cols] + jnp.dot(ga_ref[1 - slot_now], v_ref[:, cols], preferred_element_type=jnp.float32)
        o_ref[:, cols] = acc
        return link_from(acc)

    n_pre = (te // mxu) * n_tok_tiles
    chunks = [functools.partial(pre_tile, p, kk) for p in range(n_pre) for kk in range(k_parts)]
    chunks += [functools.partial(value_chunk, c) for c in range(d // mxu)]
    results = []

    def finalize(il):
        mi, off = divmod(il * n_keys, mxu)
        while any((mi, ni) not in done_tiles for ni in range(n_tok_tiles)):
            results.append(chunks.pop(0)())
        p = jnp.concatenate([pre_tiles[mi, ni][off:off + n_keys, :] for ni in range(n_tok_tiles)], axis=1)
        act = 0.5 * p * (1.0 + lax.erf(p * (2.0 ** -0.5)))
        ga_ref[slot_now, :, il * n_keys:(il + 1) * n_keys] = (gates.pop(il) * act).astype(ga_ref.dtype).T

    pieces = [(il, h) for il in range(n_i) for h in range(heads)]
    costs = [mxu * kw * mxu] * (n_pre * k_parts) + [tm * te * mxu] * (d // mxu)
    ends = [sum(costs[:c + 1]) for c in range(len(costs))]
    slot = ends[-1] / len(pieces)
    linked = 0
    gates = {}
    link = link_from(s_ref[0:8, 0:LANES])
    for k, (il, h) in enumerate(pieces):
        while len(results) < len(costs) and ends[len(results)] - costs[len(results)] < (k + 1) * slot:
            results.append(chunks.pop(0)())
        while linked < len(results) and ends[linked] <= k * slot:
            link, linked = link + results[linked], linked + 1
        for _ in range(PEER_LINK_ROLLS):
            link = pltpu.roll(link, 1, 1)
        base = h * 2 * n_keys
        a_row = s_ref[pl.ds(base + jb * n_i + il, 1), :] + jnp.concatenate([link[0:1, :]] * (tm // LANES), axis=1)
        ea_row = e_ref[pl.ds(base + jb * n_i + il, 1), :]
        b_tile = s_ref[base + n_keys:base + 2 * n_keys, :]
        eb_tile = e_ref[base + n_keys:base + 2 * n_keys, :]
        term = jnp.where(a_row + b_tile >= tau_ref[h], ea_row * eb_tile, 0.0)
        gates[il] = term if h == 0 else gates[il] + term
        if h == heads - 1:
            finalize(il)
    while chunks:
        chunks.pop(0)()


def _peer_dense(h_bf, u_bf, v_bf, scores_t, exps_t, tau, heads, n_keys):
    t, d = h_bf.shape
    e = u_bf.shape[0]
    tm, te = _blk(t, 512), _blk(e, 512)
    nb = e // te
    kern = functools.partial(_peer_dense_kernel, heads=heads, n_keys=n_keys, n_blocks=nb)
    rows = scores_t.shape[0]
    once = pl.Buffered(1)
    return pl.pallas_call(
        kern, out_shape=jax.ShapeDtypeStruct((t, d), jnp.float32), grid=(t // tm, nb + 1),
        in_specs=[pl.BlockSpec((tm, d), lambda i, j: (i, 0), pipeline_mode=once),
                  pl.BlockSpec((te, d), lambda i, j: (jnp.minimum(j, nb - 1), 0)),
                  pl.BlockSpec((te, d), lambda i, j: (jnp.maximum(j - 1, 0), 0)),
                  pl.BlockSpec((rows, tm), lambda i, j: (0, i), pipeline_mode=once),
                  pl.BlockSpec((rows, tm), lambda i, j: (0, i), pipeline_mode=once),
                  pl.BlockSpec((heads, 1, tm), lambda i, j: (0, 0, i))],
        out_specs=pl.BlockSpec((tm, d), lambda i, j: (i, 0)),
        scratch_shapes=[pltpu.VMEM((2, tm, te), jnp.bfloat16)],
        compiler_params=_params("parallel", "arbitrary"), name="peer_dense",
    )(h_bf, u_bf, v_bf, scores_t, exps_t, tau)


def kernel(x, positions, w_in, b_gate, lambda_q1, lambda_k1, lambda_q2, lambda_k2, subln_g, w_sb_branch,
           w_da_branch, w_out, ln1_g, ln1_b, peer_w_q, peer_sub_keys, peer_u, peer_v, ln2_g, ln2_b):
    batch, seq, d = x.shape
    depth = w_in.shape[0]
    hd = lambda_q1.shape[-1]
    sb_w, da_w = w_sb_branch.shape[1], w_da_branch.shape[1]
    sb_heads, da_heads = sb_w // hd, da_w // (2 * hd)
    n_qkv = 3 * sb_w + 3 * da_w
    alpha = (2 * depth) ** 0.25
    bf = jnp.bfloat16
    t = batch * seq

    cos, sin = _rope_tables(positions, hd)
    col_scale = np.ones((1, n_qkv), np.float32)
    col_scale[:, :sb_w] = hd ** -0.5
    col_scale[:, 3 * sb_w:3 * sb_w + da_w] = hd ** -0.5 * math.log2(math.e)
    col_scale = jnp.asarray(col_scale)
    h = x.reshape(t, d)
    h_bf = h.astype(bf)
    for l in range(depth):
        lam_init = 0.8 - 0.6 * math.exp(-0.3 * l)
        w_in_bf = w_in[l].astype(bf)
        qkv = _qkv_proj(h_bf, w_in_bf, col_scale, cos, sin, n_qkv, (3 * sb_w, 3 * sb_w + 2 * da_w), hd)
        gate = _gate_proj(h_bf, w_in_bf, b_gate[l].reshape(1, -1), n_qkv, 2 * d)
        o_sb = _sb_attention(qkv, batch, seq, sb_heads, hd)
        o_da = _da_attention(qkv, (lambda_q1[l], lambda_k1[l], lambda_q2[l], lambda_k2[l]), subln_g[l],
                             batch, seq, da_heads, hd, 3 * sb_w, lam_init)
        merged = _merge(o_sb, o_da, w_sb_branch[l].astype(bf), w_da_branch[l].astype(bf), gate)
        mix = _matmul(merged, w_out[l].astype(bf), jnp.float32, "out_proj")
        h, h_bf = _residual_ln(h, mix, ln1_g[l], ln1_b[l], alpha)

        heads, _, n_keys, half = peer_sub_keys[l].shape
        hq = _matmul(h_bf, peer_w_q[l].astype(bf), bf, "peer_query")
        scores_t, exps_t, tau = _peer_topk(hq, peer_sub_keys[l].astype(bf))
        ffn = _peer_dense(h_bf, peer_u[l].astype(bf), peer_v[l].astype(bf), scores_t, exps_t, tau, heads, n_keys)
        h, h_bf = _residual_ln(h, ffn, ln2_g[l], ln2_b[l], alpha)
    return h.reshape(batch, seq, d)
```

```python
import functools
import math

import numpy as np
import jax
import jax.numpy as jnp
from jax import lax
from jax.experimental import pallas as pl
from jax.experimental.pallas import tpu as pltpu

LN_EPS = 1e-5
ROPE_THETA = 10000.0
PEER_TOPK = 16
V7X_VMEM_LIMIT_BYTES = 56 * 1024 * 1024
LANES = 128
MXU_TILE = 256
PEER_KEY_SPLIT = 4
PEER_LINK_ROLLS = 1
NEG_INF = float("-inf")


def _params(*sem):
    return pltpu.CompilerParams(dimension_semantics=sem, vmem_limit_bytes=V7X_VMEM_LIMIT_BYTES)


def _blk(n, pref):
    b = min(n, pref)
    while n % b:
        b //= 2
    return b


def _rope_kernel(pos_ref, freq_ref, cos_ref, sin_ref):
    ang = pos_ref[...].astype(jnp.float32) * freq_ref[...]
    hd = freq_ref.shape[1]
    lane = lax.broadcasted_iota(jnp.int32, ang.shape, 1)
    cos_ref[...] = jnp.cos(ang)
    s = jnp.sin(ang)
    sin_ref[...] = jnp.where(lane < hd // 2, -s, s)


def _rope_tables(positions, hd):
    t = positions.size
    bm = _blk(t, 2048)
    inv = ROPE_THETA ** (-np.arange(0, hd, 2, dtype=np.float32) / np.float32(hd))
    freq = jnp.asarray(np.concatenate([inv, inv]).astype(np.float32).reshape(1, hd))
    out = jax.ShapeDtypeStruct((t, hd), jnp.float32)
    return pl.pallas_call(
        _rope_kernel, out_shape=(out, out), grid=(t // bm,),
        in_specs=[pl.BlockSpec((bm, 1), lambda i: (i, 0)), pl.BlockSpec((1, hd), lambda i: (0, 0))],
        out_specs=(pl.BlockSpec((bm, hd), lambda i: (i, 0)), pl.BlockSpec((bm, hd), lambda i: (i, 0))),
        compiler_params=_params("parallel"), name="rope_tables",
    )(positions.reshape(t, 1), freq)


def _qkv_kernel(x_ref, w_ref, cs_ref, cos_ref, sin_ref, o_ref, *, rope_lo, rope_hi, hd):
    acc = jnp.dot(x_ref[...], w_ref[...], preferred_element_type=jnp.float32) * cs_ref[...]
    j = pl.program_id(1)
    is_rope = jnp.logical_and(j >= rope_lo, j < rope_hi)

    @pl.when(is_rope)
    def _():
        c = cos_ref[...]
        s = sin_ref[...]
        for t in range(acc.shape[1] // hd):
            a = acc[:, t * hd:(t + 1) * hd]
            o_ref[:, t * hd:(t + 1) * hd] = (a * c + pltpu.roll(a, hd // 2, 1) * s).astype(o_ref.dtype)

    @pl.when(jnp.logical_not(is_rope))
    def _():
        o_ref[...] = acc.astype(o_ref.dtype)


def _qkv_proj(x, w, col_scale, cos, sin, n_out, rope_cols, hd):
    m, k = x.shape
    bm, bn = _blk(m, 1024), _blk(math.gcd(n_out, rope_cols[0], rope_cols[1]), 1024)
    kern = functools.partial(_qkv_kernel, rope_lo=rope_cols[0] // bn, rope_hi=rope_cols[1] // bn, hd=hd)
    return pl.pallas_call(
        kern, out_shape=jax.ShapeDtypeStruct((m, n_out), jnp.bfloat16), grid=(m // bm, n_out // bn),
        in_specs=[pl.BlockSpec((bm, k), lambda i, j: (i, 0)), pl.BlockSpec((k, bn), lambda i, j: (0, j)),
                  pl.BlockSpec((1, bn), lambda i, j: (0, j)),
                  pl.BlockSpec((bm, hd), lambda i, j: (i, 0)), pl.BlockSpec((bm, hd), lambda i, j: (i, 0))],
        out_specs=pl.BlockSpec((bm, bn), lambda i, j: (i, j)),
        compiler_params=_params("parallel", "parallel"), name="qkv_proj",
    )(x, w, col_scale, cos, sin)


def _gate_kernel(x_ref, w_ref, b_ref, o_ref):
    acc = jnp.dot(x_ref[...], w_ref[...], preferred_element_type=jnp.float32)
    o_ref[...] = jax.nn.sigmoid(acc + b_ref[...]).astype(o_ref.dtype)


def _gate_proj(x, w, bias, col0, n_out):
    m, k = x.shape
    bm, bn = _blk(m, 1024), _blk(math.gcd(n_out, col0), 1024)
    off = col0 // bn
    return pl.pallas_call(
        _gate_kernel, out_shape=jax.ShapeDtypeStruct((m, n_out), jnp.float32), grid=(m // bm, n_out // bn),
        in_specs=[pl.BlockSpec((bm, k), lambda i, j: (i, 0)), pl.BlockSpec((k, bn), lambda i, j: (0, j + off)),
                  pl.BlockSpec((1, bn), lambda i, j: (0, j))],
        out_specs=pl.BlockSpec((bm, bn), lambda i, j: (i, j)),
        compiler_params=_params("parallel", "parallel"), name="gate_proj",
    )(x, w, bias)


def _mm_kernel(a_ref, b_ref, o_ref):
    o_ref[...] = jnp.dot(a_ref[...], b_ref[...], preferred_element_type=jnp.float32).astype(o_ref.dtype)


def _matmul(a, b, out_dtype, name):
    m, k = a.shape
    n = b.shape[1]
    bm, bn = _blk(m, 1024), _blk(n, 1024)
    return pl.pallas_call(
        _mm_kernel, out_shape=jax.ShapeDtypeStruct((m, n), out_dtype), grid=(m // bm, n // bn),
        in_specs=[pl.BlockSpec((bm, k), lambda i, j: (i, 0)), pl.BlockSpec((k, bn), lambda i, j: (0, j))],
        out_specs=pl.BlockSpec((bm, bn), lambda i, j: (i, j)),
        compiler_params=_params("parallel", "parallel"), name=name,
    )(a, b)


def _merge_kernel(osb_ref, oda_ref, wsb_ref, wda_ref, gsb_ref, gda_ref, o_ref):
    ysb = jnp.dot(osb_ref[...], wsb_ref[...], preferred_element_type=jnp.float32)
    yda = jnp.dot(oda_ref[...], wda_ref[...], preferred_element_type=jnp.float32)
    o_ref[...] = (gsb_ref[...] * ysb + gda_ref[...] * yda).astype(o_ref.dtype)


def _merge(o_sb, o_da, w_sb, w_da, gate):
    m = o_sb.shape[0]
    d = w_sb.shape[1]
    bm, bn = _blk(m, 1024), _blk(d, 512)
    goff = d // bn
    return pl.pallas_call(
        _merge_kernel, out_shape=jax.ShapeDtypeStruct((m, d), jnp.bfloat16), grid=(m // bm, d // bn),
        in_specs=[pl.BlockSpec((bm, o_sb.shape[1]), lambda i, j: (i, 0)),
                  pl.BlockSpec((bm, o_da.shape[1]), lambda i, j: (i, 0)),
                  pl.BlockSpec((w_sb.shape[0], bn), lambda i, j: (0, j)),
                  pl.BlockSpec((w_da.shape[0], bn), lambda i, j: (0, j)),
                  pl.BlockSpec((bm, bn), lambda i, j: (i, j)),
                  pl.BlockSpec((bm, bn), lambda i, j: (i, j + goff))],
        out_specs=pl.BlockSpec((bm, bn), lambda i, j: (i, j)),
        compiler_params=_params("parallel", "parallel"), name="branch_merge",
    )(o_sb, o_da, w_sb, w_da, gate, gate)


def _ln_kernel(res_ref, y_ref, g_ref, b_ref, o_ref, obf_ref, *, alpha):
    z = alpha * res_ref[...] + y_ref[...]
    mu = jnp.mean(z, axis=-1, keepdims=True)
    zc = z - mu
    var = jnp.mean(zc * zc, axis=-1, keepdims=True)
    out = zc * lax.rsqrt(var + LN_EPS) * g_ref[...] + b_ref[...]
    o_ref[...] = out
    obf_ref[...] = out.astype(obf_ref.dtype)


def _residual_ln(res, y, g, b, alpha):
    m, d = res.shape
    bm = _blk(m, 256)
    row = pl.BlockSpec((bm, d), lambda i: (i, 0))
    vec = pl.BlockSpec((1, d), lambda i: (0, 0))
    return pl.pallas_call(
        functools.partial(_ln_kernel, alpha=alpha),
        out_shape=(jax.ShapeDtypeStruct((m, d), jnp.float32), jax.ShapeDtypeStruct((m, d), jnp.bfloat16)),
        grid=(m // bm,), in_specs=[row, row, vec, vec], out_specs=(row, row),
        compiler_params=_params("parallel"), name="residual_layernorm",
    )(res, y, g.reshape(1, d), b.reshape(1, d))


SB_SKIP_LOG2 = -100.0 * math.log2(math.e)


def _sb_kernel(q_ref, k_ref, v_ref, tri_ref, o_ref, c_ref, acc_ref, *, blk, hd, hps):
    qi = pl.program_id(2)
    row = lax.broadcasted_iota(jnp.int32, (blk, blk), 0)
    col = lax.broadcasted_iota(jnp.int32, (blk, blk), 1)
    valid_diag = col < row
    tri = tri_ref[...]

    def visit(j, valid):
        start = pl.multiple_of(j * blk, blk)
        zs, log_fails = [], []
        for g in range(hps):
            q = q_ref[:, g * hd:(g + 1) * hd]
            k = k_ref[pl.ds(start, blk), g * hd:(g + 1) * hd]
            z = lax.dot_general(q, k, (((1,), (1,)), ((), ())), preferred_element_type=jnp.float32)
            log_fail = -(jnp.maximum(z, 0.0) + jnp.log2(1.0 + jnp.exp2(-jnp.abs(z))))
            if valid is not None:
                log_fail = jnp.where(valid, log_fail, 0.0)
            zs.append(z)
            log_fails.append(log_fail)
        sums_all = jnp.dot(jnp.concatenate([lf.astype(jnp.bfloat16) for lf in log_fails], axis=0), tri,
                           preferred_element_type=jnp.float32)
        for g in range(hps):
            v = v_ref[pl.ds(start, blk), g * hd:(g + 1) * hd]
            sums = sums_all[g * blk:(g + 1) * blk]
            c = c_ref[g]
            w = jnp.exp2(zs[g] + log_fails[g] + sums[:, :blk] + c)
            if valid is not None:
                w = jnp.where(valid, w, 0.0)
            acc_ref[g] += jnp.dot(w.astype(v.dtype), v, preferred_element_type=jnp.float32)
            c_ref[g] = c + sums[:, blk:]

    def bound():
        return jnp.max(c_ref[:, :, :hd])

    c_ref[...] = jnp.zeros_like(c_ref)
    acc_ref[...] = jnp.zeros_like(acc_ref)
    visit(qi, valid_diag)

    def cond(state):
        j, worst = state
        return jnp.logical_and(j >= 0, worst > SB_SKIP_LOG2)

    def body(state):
        j, _ = state
        visit(j, None)
        return j - 1, bound()

    lax.while_loop(cond, body, (qi - 1, bound()))
    for g in range(hps):
        o_ref[:, g * hd:(g + 1) * hd] = acc_ref[g].astype(o_ref.dtype)


def _sb_attention(qkv, batch, seq, heads, hd):
    blk = _blk(seq, 256)
    nq = seq // blk
    hps = max(h for h in (4, 2, 1) if heads % h == 0)
    hg, w = heads // hps, hps * hd
    tri = np.concatenate([np.tril(np.ones((blk, blk), np.float32), -1), np.ones((blk, blk), np.float32)], axis=1)
    kern = functools.partial(_sb_kernel, blk=blk, hd=hd, hps=hps)
    return pl.pallas_call(
        kern, out_shape=jax.ShapeDtypeStruct((batch * seq, heads * hd), jnp.bfloat16), grid=(batch, hg, nq),
        in_specs=[pl.BlockSpec((blk, w), lambda b, h, i: (b * nq + i, h)),
                  pl.BlockSpec((seq, w), lambda b, h, i: (b, hg + h)),
                  pl.BlockSpec((seq, w), lambda b, h, i: (b, 2 * hg + h)),
                  pl.BlockSpec((blk, 2 * blk), lambda b, h, i: (0, 0))],
        out_specs=pl.BlockSpec((blk, w), lambda b, h, i: (b * nq + i, h)),
        scratch_shapes=[pltpu.VMEM((hps, blk, blk), jnp.float32), pltpu.VMEM((hps, blk, hd), jnp.float32)],
        compiler_params=_params("parallel", "parallel", "parallel"), name="stick_breaking_attention",
    )(qkv, qkv, qkv, jnp.asarray(tri, jnp.bfloat16))


def _da_kernel(lq1_ref, lk1_ref, lq2_ref, lk2_ref, g_ref, q1_ref, q2_ref, k1_ref, k2_ref, v_ref, o_ref,
               m_ref, l_ref, acc_ref, *, blk, hd, lam_init):
    qi = pl.program_id(2)
    row = lax.broadcasted_iota(jnp.int32, (blk, blk), 0)
    col = lax.broadcasted_iota(jnp.int32, (blk, blk), 1)
    causal_diag = col <= row
    qs = (q1_ref[...], q2_ref[...])
    ks = (k1_ref, k2_ref)

    m_ref[...] = jnp.full_like(m_ref, NEG_INF)
    l_ref[...] = jnp.zeros_like(l_ref)
    acc_ref[...] = jnp.zeros_like(acc_ref)

    def visit(j, mask):
        start = pl.multiple_of(j * blk, blk)
        v = v_ref[pl.ds(start, blk), :]
        probs, corrs = [], []
        for s in range(2):
            k = ks[s][pl.ds(start, blk), :]
            sc = lax.dot_general(qs[s], k, (((1,), (1,)), ((), ())), preferred_element_type=jnp.float32)
            if mask is not None:
                sc = jnp.where(mask, sc, NEG_INF)
            m_old = m_ref[s]
            m_new = jnp.maximum(m_old, jnp.max(sc, axis=-1, keepdims=True))
            p = jnp.exp2(sc - jnp.concatenate([m_new] * (blk // hd), axis=1))
            corr = jnp.exp2(m_old - m_new)
            l_ref[s] = corr * l_ref[s] + jnp.sum(p, axis=-1, keepdims=True)
            m_ref[s] = m_new
            probs.append(p.astype(v.dtype))
            corrs.append(jnp.concatenate([corr, corr], axis=1))
        pv = jnp.dot(jnp.concatenate(probs, axis=0), v, preferred_element_type=jnp.float32)
        for s in range(2):
            acc_ref[s] = corrs[s] * acc_ref[s] + pv[s * blk:(s + 1) * blk]

    def body(j, carry):
        visit(j, None)
        return carry

    lax.fori_loop(0, qi, body, 0)
    visit(qi, causal_diag)

    lam = (jnp.exp(jnp.sum(lq1_ref[...] * lk1_ref[...], axis=-1, keepdims=True))
           - jnp.exp(jnp.sum(lq2_ref[...] * lk2_ref[...], axis=-1, keepdims=True)) + lam_init)
    l1 = jnp.concatenate([l_ref[0], l_ref[0]], axis=1)
    l2 = jnp.concatenate([l_ref[1], l_ref[1]], axis=1)
    o = acc_ref[0] / l1 - lam * (acc_ref[1] / l2)
    o = o * lax.rsqrt(jnp.mean(o * o, axis=-1, keepdims=True) + LN_EPS) * g_ref[...] * (1.0 - lam_init)
    o_ref[...] = o.astype(o_ref.dtype)


def _da_attention(qkv, lambdas, subln_g, batch, seq, heads, hd, col0, lam_init):
    blk = _blk(seq, 512)
    nq = seq // blk
    qb, kb, vb = col0 // hd, col0 // hd + 2 * heads, col0 // (2 * hd) + 2 * heads
    vec = pl.BlockSpec((1, hd), lambda b, h, i: (0, 0))
    kern = functools.partial(_da_kernel, blk=blk, hd=hd, lam_init=lam_init)
    return pl.pallas_call(
        kern, out_shape=jax.ShapeDtypeStruct((batch * seq, heads * 2 * hd), jnp.bfloat16), grid=(batch, heads, nq),
        in_specs=[vec, vec, vec, vec, pl.BlockSpec((1, 2 * hd), lambda b, h, i: (0, 0)),
                  pl.BlockSpec((blk, hd), lambda b, h, i: (b * nq + i, qb + 2 * h)),
                  pl.BlockSpec((blk, hd), lambda b, h, i: (b * nq + i, qb + 2 * h + 1)),
                  pl.BlockSpec((seq, hd), lambda b, h, i: (b, kb + 2 * h)),
                  pl.BlockSpec((seq, hd), lambda b, h, i: (b, kb + 2 * h + 1)),
                  pl.BlockSpec((seq, 2 * hd), lambda b, h, i: (b, vb + h))],
        out_specs=pl.BlockSpec((blk, 2 * hd), lambda b, h, i: (b * nq + i, h)),
        scratch_shapes=[pltpu.VMEM((2, blk, hd), jnp.float32), pltpu.VMEM((2, blk, hd), jnp.float32),
                        pltpu.VMEM((2, blk, 2 * hd), jnp.float32)],
        compiler_params=_params("parallel", "parallel", "parallel"), name="differential_attention",
    )(*[l.reshape(1, hd) for l in lambdas], subln_g.reshape(1, 2 * hd), qkv, qkv, qkv, qkv, qkv)


def _extract_top(x, n):
    rows = lax.broadcasted_iota(jnp.int32, x.shape, 0).astype(jnp.float32)
    tops = []
    for _ in range(n):
        m = jnp.max(x, axis=0, keepdims=True)
        tops.append(m)
        first = jnp.min(jnp.where(x == m, rows, float(x.shape[0])), axis=0, keepdims=True)
        x = jnp.where(rows == first, NEG_INF, x)
    return tops


def _pair_candidates(top_a, top_b, combine):
    k = len(top_a)
    sub = 8
    b_all = jnp.concatenate(top_b, axis=0)
    pieces, masks, singles = [], [], []
    for p in range(k):
        n_q = k // (p + 1)
        if n_q == 1:
            singles.append(top_a[p])
            continue
        rows = -(-n_q // sub) * sub
        pieces.append(combine(top_a[p], b_all[0:rows]))
        masks.append(lax.broadcasted_iota(jnp.int32, (rows, 1), 0) < n_q)
    if singles:
        pieces.append(combine(jnp.concatenate(singles, axis=0), top_b[0]))
        masks.append(None)
    return pieces, masks


def _peer_topk_kernel(q_ref, keys_ref, s_ref, e_ref, tau_ref, *, n_keys, topk):
    half = q_ref.shape[1] // 2
    scores, tops = [], []
    for c in range(2):
        s = lax.dot_general(keys_ref[0, c], q_ref[:, c * half:(c + 1) * half], (((1,), (1,)), ((), ())),
                            preferred_element_type=jnp.float32)
        scores.append(s)
        tops.append(_extract_top(s, topk))
    exps = [[jnp.exp(t - top[0]) for t in top] for top in tops]
    sums, masks = _pair_candidates(tops[0], tops[1], jnp.add)
    prods, _ = _pair_candidates(exps[0], exps[1], jnp.multiply)
    cand = jnp.concatenate([s if m is None else jnp.where(m, s, NEG_INF) for s, m in zip(sums, masks)], axis=0)
    cand_e = jnp.concatenate(prods, axis=0)
    tau = _extract_top(cand, topk)[-1]
    z = jnp.sum(jnp.where(cand >= tau, cand_e, 0.0), axis=0, keepdims=True)
    s_ref[...] = jnp.concatenate(scores, axis=0)
    e_ref[...] = jnp.concatenate([jnp.exp(scores[0] - tops[0][0]) / z, jnp.exp(scores[1] - tops[1][0])], axis=0)
    tau_ref[0] = tau


def _peer_topk(hq, keys):
    t = hq.shape[0]
    heads, _, n_keys, half = keys.shape
    tn = _blk(t, 512)
    kern = functools.partial(_peer_topk_kernel, n_keys=n_keys, topk=PEER_TOPK)
    st = jax.ShapeDtypeStruct((heads * 2 * n_keys, t), jnp.float32)
    blk = pl.BlockSpec((2 * n_keys, tn), lambda i, h: (h, i))
    return pl.pallas_call(
        kern, out_shape=(st, st, jax.ShapeDtypeStruct((heads, 1, t), jnp.float32)), grid=(t // tn, heads),
        in_specs=[pl.BlockSpec((tn, 2 * half), lambda i, h: (i, h)),
                  pl.BlockSpec((1, 2, n_keys, half), lambda i, h: (h, 0, 0, 0))],
        out_specs=(blk, blk, pl.BlockSpec((1, 1, tn), lambda i, h: (h, 0, i))),
        compiler_params=_params("parallel", "parallel"), name="peer_topk",
    )(hq, keys)


def _peer_dense_kernel(h_ref, u_ref, v_ref, s_ref, e_ref, tau_ref, o_ref, ga_ref, *, heads, n_keys, n_blocks):
    j = pl.program_id(1)

    @pl.when(j == 0)
    def _():
        ga_ref[1] = jnp.zeros(ga_ref.shape[1:], ga_ref.dtype)
        o_ref[...] = jnp.zeros_like(o_ref)

    for parity in range(2):
        pl.when(j % 2 == parity)(functools.partial(
            _peer_dense_step, h_ref, u_ref, v_ref, s_ref, e_ref, tau_ref, o_ref, ga_ref,
            jnp.minimum(j, n_blocks - 1), parity, heads, n_keys))


def _peer_dense_step(h_ref, u_ref, v_ref, s_ref, e_ref, tau_ref, o_ref, ga_ref, jb, slot_now, heads, n_keys):
    tm, te = ga_ref.shape[1:]
    d = o_ref.shape[1]
    n_i = te // n_keys
    mxu = MXU_TILE
    n_tok_tiles = tm // mxu

    def link_from(x):
        bits = pltpu.bitcast(x[0:8, 0:LANES], jnp.uint32)
        zero = lax.shift_right_logical(lax.shift_right_logical(bits, jnp.uint32(16)), jnp.uint32(16))
        return pltpu.bitcast(zero, jnp.float32)

    pre_tiles = {}

    k_parts = PEER_KEY_SPLIT
    kw = d // k_parts
    done_tiles = set()

    def pre_tile(p, kk):
        mi, ni = divmod(p, n_tok_tiles)
        part = lax.dot_general(
            u_ref[mi * mxu:(mi + 1) * mxu, kk * kw:(kk + 1) * kw], h_ref[ni * mxu:(ni + 1) * mxu, kk * kw:(kk + 1) * kw],
            (((1,), (1,)), ((), ())), preferred_element_type=jnp.float32)
        pre_tiles[mi, ni] = part if kk == 0 else pre_tiles[mi, ni] + part
        if kk == k_parts - 1:
            done_tiles.add((mi, ni))
        return link_from(part)

    def value_chunk(c):
        cols = slice(c * mxu, (c + 1) * mxu)
        acc = o_ref[:, cols] + jnp.dot(ga_ref[1 - slot_now], v_ref[:, cols], preferred_element_type=jnp.float32)
        o_ref[:, cols] = acc
        return link_from(acc)

    n_pre = (te // mxu) * n_tok_tiles
    chunks = [functools.partial(pre_tile, p, kk) for p in range(n_pre) for kk in range(k_parts)]
    chunks += [functools.partial(value_chunk, c) for c in range(d // mxu)]
    results = []

    def finalize(il):
        mi, off = divmod(il * n_keys, mxu)
        while any((mi, ni) not in done_tiles for ni in range(n_tok_tiles)):
            results.append(chunks.pop(0)())
        p = jnp.concatenate([pre_tiles[mi, ni][off:off + n_keys, :] for ni in range(n_tok_tiles)], axis=1)
        act = 0.5 * p * (1.0 + lax.erf(p * (2.0 ** -0.5)))
        ga_ref[slot_now, :, il * n_keys:(il + 1) * n_keys] = (gates.pop(il) * act).astype(ga_ref.dtype).T

    pieces = [(il, h) for il in range(n_i) for h in range(heads)]
    costs = [mxu * kw * mxu] * (n_pre * k_parts) + [tm * te * mxu] * (d // mxu)
    ends = [sum(costs[:c + 1]) for c in range(len(costs))]
    slot = ends[-1] / len(pieces)
    linked = 0
    gates = {}
    link = link_from(s_ref[0:8, 0:LANES])
    for k, (il, h) in enumerate(pieces):
        while len(results) < len(costs) and ends[len(results)] - costs[len(results)] < (k + 1) * slot:
            results.append(chunks.pop(0)())
        while linked < len(results) and ends[linked] <= k * slot:
            link, linked = link + results[linked], linked + 1
        for _ in range(PEER_LINK_ROLLS):
            link = pltpu.roll(link, 1, 1)
        base = h * 2 * n_keys
        a_row = s_ref[pl.ds(base + jb * n_i + il, 1), :] + jnp.concatenate([link[0:1, :]] * (tm // LANES), axis=1)
        ea_row = e_ref[pl.ds(base + jb * n_i + il, 1), :]
        b_tile = s_ref[base + n_keys:base + 2 * n_keys, :]
        eb_tile = e_ref[base + n_keys:base + 2 * n_keys, :]
        term = jnp.where(a_row + b_tile >= tau_ref[h], ea_row * eb_tile, 0.0)
        gates[il] = term if h == 0 else gates[il] + term
        if h == heads - 1:
            finalize(il)
    while chunks:
        chunks.pop(0)()


def _peer_dense(h_bf, u_bf, v_bf, scores_t, exps_t, tau, heads, n_keys):
    t, d = h_bf.shape
    e = u_bf.shape[0]
    tm, te = _blk(t, 512), _blk(e, 512)
    nb = e // te
    kern = functools.partial(_peer_dense_kernel, heads=heads, n_keys=n_keys, n_blocks=nb)
    rows = scores_t.shape[0]
    once = pl.Buffered(1)
    return pl.pallas_call(
        kern, out_shape=jax.ShapeDtypeStruct((t, d), jnp.float32), grid=(t // tm, nb + 1),
        in_specs=[pl.BlockSpec((tm, d), lambda i, j: (i, 0), pipeline_mode=once),
                  pl.BlockSpec((te, d), lambda i, j: (jnp.minimum(j, nb - 1), 0)),
                  pl.BlockSpec((te, d), lambda i, j: (jnp.maximum(j - 1, 0), 0)),
                  pl.BlockSpec((rows, tm), lambda i, j: (0, i), pipeline_mode=once),
                  pl.BlockSpec((rows, tm), lambda i, j: (0, i), pipeline_mode=once),
                  pl.BlockSpec((heads, 1, tm), lambda i, j: (0, 0, i))],
        out_specs=pl.BlockSpec((tm, d), lambda i, j: (i, 0)),
        scratch_shapes=[pltpu.VMEM((2, tm, te), jnp.bfloat16)],
        compiler_params=_params("parallel", "arbitrary"), name="peer_dense",
    )(h_bf, u_bf, v_bf, scores_t, exps_t, tau)


def kernel(x, positions, w_in, b_gate, lambda_q1, lambda_k1, lambda_q2, lambda_k2, subln_g, w_sb_branch,
           w_da_branch, w_out, ln1_g, ln1_b, peer_w_q, peer_sub_keys, peer_u, peer_v, ln2_g, ln2_b):
    batch, seq, d = x.shape
    depth = w_in.shape[0]
    hd = lambda_q1.shape[-1]
    sb_w, da_w = w_sb_branch.shape[1], w_da_branch.shape[1]
    sb_heads, da_heads = sb_w // hd, da_w // (2 * hd)
    n_qkv = 3 * sb_w + 3 * da_w
    alpha = (2 * depth) ** 0.25
    bf = jnp.bfloat16
    t = batch * seq

    cos, sin = _rope_tables(positions, hd)
    col_scale = np.ones((1, n_qkv), np.float32)
    col_scale[:, :sb_w] = hd ** -0.5 * math.log2(math.e)
    col_scale[:, 3 * sb_w:3 * sb_w + da_w] = hd ** -0.5 * math.log2(math.e)
    col_scale = jnp.asarray(col_scale)
    h = x.reshape(t, d)
    h_bf = h.astype(bf)
    for l in range(depth):
        lam_init = 0.8 - 0.6 * math.exp(-0.3 * l)
        w_in_bf = w_in[l].astype(bf)
        qkv = _qkv_proj(h_bf, w_in_bf, col_scale, cos, sin, n_qkv, (3 * sb_w, 3 * sb_w + 2 * da_w), hd)
        gate = _gate_proj(h_bf, w_in_bf, b_gate[l].reshape(1, -1), n_qkv, 2 * d)
        o_sb = _sb_attention(qkv, batch, seq, sb_heads, hd)
        o_da = _da_attention(qkv, (lambda_q1[l], lambda_k1[l], lambda_q2[l], lambda_k2[l]), subln_g[l],
                             batch, seq, da_heads, hd, 3 * sb_w, lam_init)
        merged = _merge(o_sb, o_da, w_sb_branch[l].astype(bf), w_da_branch[l].astype(bf), gate)
        mix = _matmul(merged, w_out[l].astype(bf), jnp.float32, "out_proj")
        h, h_bf = _residual_ln(h, mix, ln1_g[l], ln1_b[l], alpha)

        heads, _, n_keys, half = peer_sub_keys[l].shape
        hq = _matmul(h_bf, peer_w_q[l].astype(bf), bf, "peer_query")
        scores_t, exps_t, tau = _peer_topk(hq, peer_sub_keys[l].astype(bf))
        ffn = _peer_dense(h_bf, peer_u[l].astype(bf), peer_v[l].astype(bf), scores_t, exps_t, tau, heads, n_keys)
        h, h_bf = _residual_ln(h, ffn, ln2_g[l], ln2_b[l], alpha)
    return h.reshape(batch, seq, d)
```

```python
import functools
import math

import numpy as np
import jax
import jax.numpy as jnp
from jax import lax
from jax.experimental import pallas as pl
from jax.experimental.pallas import tpu as pltpu

LN_EPS = 1e-5
ROPE_THETA = 10000.0
PEER_TOPK = 16
V7X_VMEM_LIMIT_BYTES = 56 * 1024 * 1024
LANES = 128
MXU_TILE = 256
PEER_KEY_SPLIT = 4
PEER_LINK_ROLLS = 1
NEG_INF = float("-inf")


def _params(*sem):
    return pltpu.CompilerParams(dimension_semantics=sem, vmem_limit_bytes=V7X_VMEM_LIMIT_BYTES)


def _blk(n, pref):
    b = min(n, pref)
    while n % b:
        b //= 2
    return b


def _rope_kernel(pos_ref, freq_ref, cos_ref, sin_ref):
    ang = pos_ref[...].astype(jnp.float32) * freq_ref[...]
    hd = freq_ref.shape[1]
    lane = lax.broadcasted_iota(jnp.int32, ang.shape, 1)
    cos_ref[...] = jnp.cos(ang)
    s = jnp.sin(ang)
    sin_ref[...] = jnp.where(lane < hd // 2, -s, s)


def _rope_tables(positions, hd):
    t = positions.size
    bm = _blk(t, 2048)
    inv = ROPE_THETA ** (-np.arange(0, hd, 2, dtype=np.float32) / np.float32(hd))
    freq = jnp.asarray(np.concatenate([inv, inv]).astype(np.float32).reshape(1, hd))
    out = jax.ShapeDtypeStruct((t, hd), jnp.float32)
    return pl.pallas_call(
        _rope_kernel, out_shape=(out, out), grid=(t // bm,),
        in_specs=[pl.BlockSpec((bm, 1), lambda i: (i, 0)), pl.BlockSpec((1, hd), lambda i: (0, 0))],
        out_specs=(pl.BlockSpec((bm, hd), lambda i: (i, 0)), pl.BlockSpec((bm, hd), lambda i: (i, 0))),
        compiler_params=_params("parallel"), name="rope_tables",
    )(positions.reshape(t, 1), freq)


def _qkv_kernel(x_ref, w_ref, cs_ref, cos_ref, sin_ref, o_ref, *, rope_lo, rope_hi, hd):
    acc = jnp.dot(x_ref[...], w_ref[...], preferred_element_type=jnp.float32) * cs_ref[...]
    j = pl.program_id(1)
    is_rope = jnp.logical_and(j >= rope_lo, j < rope_hi)

    @pl.when(is_rope)
    def _():
        c = cos_ref[...]
        s = sin_ref[...]
        for t in range(acc.shape[1] // hd):
            a = acc[:, t * hd:(t + 1) * hd]
            o_ref[:, t * hd:(t + 1) * hd] = (a * c + pltpu.roll(a, hd // 2, 1) * s).astype(o_ref.dtype)

    @pl.when(jnp.logical_not(is_rope))
    def _():
        o_ref[...] = acc.astype(o_ref.dtype)


def _qkv_proj(x, w, col_scale, cos, sin, n_out, rope_cols, hd):
    m, k = x.shape
    bm, bn = _blk(m, 1024), _blk(math.gcd(n_out, rope_cols[0], rope_cols[1]), 1024)
    kern = functools.partial(_qkv_kernel, rope_lo=rope_cols[0] // bn, rope_hi=rope_cols[1] // bn, hd=hd)
    return pl.pallas_call(
        kern, out_shape=jax.ShapeDtypeStruct((m, n_out), jnp.bfloat16), grid=(m // bm, n_out // bn),
        in_specs=[pl.BlockSpec((bm, k), lambda i, j: (i, 0)), pl.BlockSpec((k, bn), lambda i, j: (0, j)),
                  pl.BlockSpec((1, bn), lambda i, j: (0, j)),
                  pl.BlockSpec((bm, hd), lambda i, j: (i, 0)), pl.BlockSpec((bm, hd), lambda i, j: (i, 0))],
        out_specs=pl.BlockSpec((bm, bn), lambda i, j: (i, j)),
        compiler_params=_params("parallel", "parallel"), name="qkv_proj",
    )(x, w, col_scale, cos, sin)


def _gate_kernel(x_ref, w_ref, b_ref, o_ref):
    acc = jnp.dot(x_ref[...], w_ref[...], preferred_element_type=jnp.float32)
    o_ref[...] = jax.nn.sigmoid(acc + b_ref[...]).astype(o_ref.dtype)


def _gate_proj(x, w, bias, col0, n_out):
    m, k = x.shape
    bm, bn = _blk(m, 1024), _blk(math.gcd(n_out, col0), 1024)
    off = col0 // bn
    return pl.pallas_call(
        _gate_kernel, out_shape=jax.ShapeDtypeStruct((m, n_out), jnp.float32), grid=(m // bm, n_out // bn),
        in_specs=[pl.BlockSpec((bm, k), lambda i, j: (i, 0)), pl.BlockSpec((k, bn), lambda i, j: (0, j + off)),
                  pl.BlockSpec((1, bn), lambda i, j: (0, j))],
        out_specs=pl.BlockSpec((bm, bn), lambda i, j: (i, j)),
        compiler_params=_params("parallel", "parallel"), name="gate_proj",
    )(x, w, bias)


def _mm_kernel(a_ref, b_ref, o_ref):
    o_ref[...] = jnp.dot(a_ref[...], b_ref[...], preferred_element_type=jnp.float32).astype(o_ref.dtype)


def _matmul(a, b, out_dtype, name):
    m, k = a.shape
    n = b.shape[1]
    bm, bn = _blk(m, 1024), _blk(n, 1024)
    return pl.pallas_call(
        _mm_kernel, out_shape=jax.ShapeDtypeStruct((m, n), out_dtype), grid=(m // bm, n // bn),
        in_specs=[pl.BlockSpec((bm, k), lambda i, j: (i, 0)), pl.BlockSpec((k, bn), lambda i, j: (0, j))],
        out_specs=pl.BlockSpec((bm, bn), lambda i, j: (i, j)),
        compiler_params=_params("parallel", "parallel"), name=name,
    )(a, b)


def _merge_kernel(osb_ref, oda_ref, wsb_ref, wda_ref, gsb_ref, gda_ref, o_ref):
    ysb = jnp.dot(osb_ref[...], wsb_ref[...], preferred_element_type=jnp.float32)
    yda = jnp.dot(oda_ref[...], wda_ref[...], preferred_element_type=jnp.float32)
    o_ref[...] = (gsb_ref[...] * ysb + gda_ref[...] * yda).astype(o_ref.dtype)


def _merge(o_sb, o_da, w_sb, w_da, gate):
    m = o_sb.shape[0]
    d = w_sb.shape[1]
    bm, bn = _blk(m, 1024), _blk(d, 512)
    goff = d // bn
    return pl.pallas_call(
        _merge_kernel, out_shape=jax.ShapeDtypeStruct((m, d), jnp.bfloat16), grid=(m // bm, d // bn),
        in_specs=[pl.BlockSpec((bm, o_sb.shape[1]), lambda i, j: (i, 0)),
                  pl.BlockSpec((bm, o_da.shape[1]), lambda i, j: (i, 0)),
                  pl.BlockSpec((w_sb.shape[0], bn), lambda i, j: (0, j)),
                  pl.BlockSpec((w_da.shape[0], bn), lambda i, j: (0, j)),
                  pl.BlockSpec((bm, bn), lambda i, j: (i, j)),
                  pl.BlockSpec((bm, bn), lambda i, j: (i, j + goff))],
        out_specs=pl.BlockSpec((bm, bn), lambda i, j: (i, j)),
        compiler_params=_params("parallel", "parallel"), name="branch_merge",
    )(o_sb, o_da, w_sb, w_da, gate, gate)


def _ln_kernel(res_ref, y_ref, g_ref, b_ref, o_ref, *low_refs, alpha):
    z = alpha * res_ref[...] + y_ref[...]
    mu = jnp.mean(z, axis=-1, keepdims=True)
    zc = z - mu
    var = jnp.mean(zc * zc, axis=-1, keepdims=True)
    out = zc * lax.rsqrt(var + LN_EPS) * g_ref[...] + b_ref[...]
    o_ref[...] = out
    if low_refs:
        low = out.astype(low_refs[0].dtype)
        low_refs[0][...] = low
        low_refs[1][...] = low.T


def _residual_ln(res, y, g, b, alpha, with_bf16):
    m, d = res.shape
    bm = _blk(m, 256)
    row = pl.BlockSpec((bm, d), lambda i: (i, 0))
    vec = pl.BlockSpec((1, d), lambda i: (0, 0))
    out_shape = [jax.ShapeDtypeStruct((m, d), jnp.float32)]
    out_specs = [row]
    if with_bf16:
        out_shape += [jax.ShapeDtypeStruct((m, d), jnp.bfloat16), jax.ShapeDtypeStruct((d, m), jnp.bfloat16)]
        out_specs += [row, pl.BlockSpec((d, bm), lambda i: (0, i))]
    return pl.pallas_call(
        functools.partial(_ln_kernel, alpha=alpha), out_shape=tuple(out_shape),
        grid=(m // bm,), in_specs=[row, row, vec, vec], out_specs=tuple(out_specs),
        compiler_params=_params("parallel"), name="residual_layernorm",
    )(res, y, g.reshape(1, d), b.reshape(1, d))


SB_SKIP_LOG2 = -100.0 * math.log2(math.e)


def _sb_kernel(q_ref, k_ref, v_ref, tri_ref, o_ref, c_ref, acc_ref, *, blk, hd, hps):
    qi = pl.program_id(2)
    row = lax.broadcasted_iota(jnp.int32, (blk, blk), 0)
    col = lax.broadcasted_iota(jnp.int32, (blk, blk), 1)
    valid_diag = col < row
    tri = tri_ref[...]

    def visit(j, valid):
        start = pl.multiple_of(j * blk, blk)
        zs, log_fails = [], []
        for g in range(hps):
            q = q_ref[:, g * hd:(g + 1) * hd]
            k = k_ref[pl.ds(start, blk), g * hd:(g + 1) * hd]
            z = lax.dot_general(q, k, (((1,), (1,)), ((), ())), preferred_element_type=jnp.float32)
            log_fail = -(jnp.maximum(z, 0.0) + jnp.log2(1.0 + jnp.exp2(-jnp.abs(z))))
            if valid is not None:
                log_fail = jnp.where(valid, log_fail, 0.0)
            zs.append(z)
            log_fails.append(log_fail)
        sums_all = jnp.dot(jnp.concatenate([lf.astype(jnp.bfloat16) for lf in log_fails], axis=0), tri,
                           preferred_element_type=jnp.float32)
        for g in range(hps):
            v = v_ref[pl.ds(start, blk), g * hd:(g + 1) * hd]
            sums = sums_all[g * blk:(g + 1) * blk]
            c = c_ref[g]
            w = jnp.exp2(zs[g] + log_fails[g] + sums[:, :blk] + c)
            if valid is not None:
                w = jnp.where(valid, w, 0.0)
            acc_ref[g] += jnp.dot(w.astype(v.dtype), v, preferred_element_type=jnp.float32)
            c_ref[g] = c + sums[:, blk:]

    def bound():
        return jnp.max(c_ref[:, :, :hd])

    c_ref[...] = jnp.zeros_like(c_ref)
    acc_ref[...] = jnp.zeros_like(acc_ref)
    visit(qi, valid_diag)

    def cond(state):
        j, worst = state
        return jnp.logical_and(j >= 0, worst > SB_SKIP_LOG2)

    def body(state):
        j, _ = state
        visit(j, None)
        return j - 1, bound()

    lax.while_loop(cond, body, (qi - 1, bound()))
    for g in range(hps):
        o_ref[:, g * hd:(g + 1) * hd] = acc_ref[g].astype(o_ref.dtype)


def _sb_attention(qkv, batch, seq, heads, hd):
    blk = _blk(seq, 256)
    nq = seq // blk
    hps = max(h for h in (4, 2, 1) if heads % h == 0)
    hg, w = heads // hps, hps * hd
    tri = np.concatenate([np.tril(np.ones((blk, blk), np.float32), -1), np.ones((blk, blk), np.float32)], axis=1)
    kern = functools.partial(_sb_kernel, blk=blk, hd=hd, hps=hps)
    return pl.pallas_call(
        kern, out_shape=jax.ShapeDtypeStruct((batch * seq, heads * hd), jnp.bfloat16), grid=(batch, hg, nq),
        in_specs=[pl.BlockSpec((blk, w), lambda b, h, i: (b * nq + i, h)),
                  pl.BlockSpec((seq, w), lambda b, h, i: (b, hg + h)),
                  pl.BlockSpec((seq, w), lambda b, h, i: (b, 2 * hg + h)),
                  pl.BlockSpec((blk, 2 * blk), lambda b, h, i: (0, 0))],
        out_specs=pl.BlockSpec((blk, w), lambda b, h, i: (b * nq + i, h)),
        scratch_shapes=[pltpu.VMEM((hps, blk, blk), jnp.float32), pltpu.VMEM((hps, blk, hd), jnp.float32)],
        compiler_params=_params("parallel", "parallel", "parallel"), name="stick_breaking_attention",
    )(qkv, qkv, qkv, jnp.asarray(tri, jnp.bfloat16))


def _da_kernel(lq1_ref, lk1_ref, lq2_ref, lk2_ref, g_ref, q1_ref, q2_ref, k1_ref, k2_ref, v_ref, o_ref,
               m_ref, l_ref, acc_ref, *, blk, hd, lam_init):
    qi = pl.program_id(2)
    row = lax.broadcasted_iota(jnp.int32, (blk, blk), 0)
    col = lax.broadcasted_iota(jnp.int32, (blk, blk), 1)
    causal_diag = col <= row
    qs = (q1_ref[...], q2_ref[...])
    ks = (k1_ref, k2_ref)

    m_ref[...] = jnp.full_like(m_ref, NEG_INF)
    l_ref[...] = jnp.zeros_like(l_ref)
    acc_ref[...] = jnp.zeros_like(acc_ref)

    def visit(j, mask):
        start = pl.multiple_of(j * blk, blk)
        v = v_ref[pl.ds(start, blk), :]
        probs, corrs = [], []
        for s in range(2):
            k = ks[s][pl.ds(start, blk), :]
            sc = lax.dot_general(qs[s], k, (((1,), (1,)), ((), ())), preferred_element_type=jnp.float32)
            if mask is not None:
                sc = jnp.where(mask, sc, NEG_INF)
            m_old = m_ref[s]
            m_new = jnp.maximum(m_old, jnp.max(sc, axis=-1, keepdims=True))
            p = jnp.exp2(sc - jnp.concatenate([m_new] * (blk // hd), axis=1))
            corr = jnp.exp2(m_old - m_new)
            l_ref[s] = corr * l_ref[s] + jnp.sum(p, axis=-1, keepdims=True)
            m_ref[s] = m_new
            probs.append(p.astype(v.dtype))
            corrs.append(jnp.concatenate([corr, corr], axis=1))
        pv = jnp.dot(jnp.concatenate(probs, axis=0), v, preferred_element_type=jnp.float32)
        for s in range(2):
            acc_ref[s] = corrs[s] * acc_ref[s] + pv[s * blk:(s + 1) * blk]

    def body(j, carry):
        visit(j, None)
        return carry

    lax.fori_loop(0, qi, body, 0)
    visit(qi, causal_diag)

    lam = (jnp.exp(jnp.sum(lq1_ref[...] * lk1_ref[...], axis=-1, keepdims=True))
           - jnp.exp(jnp.sum(lq2_ref[...] * lk2_ref[...], axis=-1, keepdims=True)) + lam_init)
    l1 = jnp.concatenate([l_ref[0], l_ref[0]], axis=1)
    l2 = jnp.concatenate([l_ref[1], l_ref[1]], axis=1)
    o = acc_ref[0] / l1 - lam * (acc_ref[1] / l2)
    o = o * lax.rsqrt(jnp.mean(o * o, axis=-1, keepdims=True) + LN_EPS) * g_ref[...] * (1.0 - lam_init)
    o_ref[...] = o.astype(o_ref.dtype)


def _da_attention(qkv, lambdas, subln_g, batch, seq, heads, hd, col0, lam_init):
    blk = _blk(seq, 512)
    nq = seq // blk
    qb, kb, vb = col0 // hd, col0 // hd + 2 * heads, col0 // (2 * hd) + 2 * heads
    vec = pl.BlockSpec((1, hd), lambda b, h, i: (0, 0))
    kern = functools.partial(_da_kernel, blk=blk, hd=hd, lam_init=lam_init)
    return pl.pallas_call(
        kern, out_shape=jax.ShapeDtypeStruct((batch * seq, heads * 2 * hd), jnp.bfloat16), grid=(batch, heads, nq),
        in_specs=[vec, vec, vec, vec, pl.BlockSpec((1, 2 * hd), lambda b, h, i: (0, 0)),
                  pl.BlockSpec((blk, hd), lambda b, h, i: (b * nq + i, qb + 2 * h)),
                  pl.BlockSpec((blk, hd), lambda b, h, i: (b * nq + i, qb + 2 * h + 1)),
                  pl.BlockSpec((seq, hd), lambda b, h, i: (b, kb + 2 * h)),
                  pl.BlockSpec((seq, hd), lambda b, h, i: (b, kb + 2 * h + 1)),
                  pl.BlockSpec((seq, 2 * hd), lambda b, h, i: (b, vb + h))],
        out_specs=pl.BlockSpec((blk, 2 * hd), lambda b, h, i: (b * nq + i, h)),
        scratch_shapes=[pltpu.VMEM((2, blk, hd), jnp.float32), pltpu.VMEM((2, blk, hd), jnp.float32),
                        pltpu.VMEM((2, blk, 2 * hd), jnp.float32)],
        compiler_params=_params("parallel", "parallel", "parallel"), name="differential_attention",
    )(*[l.reshape(1, hd) for l in lambdas], subln_g.reshape(1, 2 * hd), qkv, qkv, qkv, qkv, qkv)


def _extract_top(x, n):
    rows = lax.broadcasted_iota(jnp.int32, x.shape, 0).astype(jnp.float32)
    tops = []
    for _ in range(n):
        m = jnp.max(x, axis=0, keepdims=True)
        tops.append(m)
        first = jnp.min(jnp.where(x == m, rows, float(x.shape[0])), axis=0, keepdims=True)
        x = jnp.where(rows == first, NEG_INF, x)
    return tops


def _pair_candidates(top_a, top_b, combine):
    k = len(top_a)
    sub = 8
    b_all = jnp.concatenate(top_b, axis=0)
    pieces, masks, singles = [], [], []
    for p in range(k):
        n_q = k // (p + 1)
        if n_q == 1:
            singles.append(top_a[p])
            continue
        rows = -(-n_q // sub) * sub
        pieces.append(combine(top_a[p], b_all[0:rows]))
        masks.append(lax.broadcasted_iota(jnp.int32, (rows, 1), 0) < n_q)
    if singles:
        pieces.append(combine(jnp.concatenate(singles, axis=0), top_b[0]))
        masks.append(None)
    return pieces, masks


def _peer_topk_kernel(q_ref, keys_ref, s_ref, e_ref, tau_ref, *, n_keys, topk):
    half = q_ref.shape[1] // 2
    scores, tops = [], []
    for c in range(2):
        s = lax.dot_general(keys_ref[0, c], q_ref[:, c * half:(c + 1) * half], (((1,), (1,)), ((), ())),
                            preferred_element_type=jnp.float32)
        scores.append(s)
        tops.append(_extract_top(s, topk))
    exps = [[jnp.exp(t - top[0]) for t in top] for top in tops]
    sums, masks = _pair_candidates(tops[0], tops[1], jnp.add)
    prods, _ = _pair_candidates(exps[0], exps[1], jnp.multiply)
    cand = jnp.concatenate([s if m is None else jnp.where(m, s, NEG_INF) for s, m in zip(sums, masks)], axis=0)
    cand_e = jnp.concatenate(prods, axis=0)
    tau = _extract_top(cand, topk)[-1]
    z = jnp.sum(jnp.where(cand >= tau, cand_e, 0.0), axis=0, keepdims=True)
    s_ref[...] = jnp.concatenate(scores, axis=0)
    e_ref[...] = jnp.concatenate([jnp.exp(scores[0] - tops[0][0]) / z, jnp.exp(scores[1] - tops[1][0])], axis=0)
    tau_ref[0] = tau


def _peer_topk(hq, keys):
    t = hq.shape[0]
    heads, _, n_keys, half = keys.shape
    tn = _blk(t, 512)
    kern = functools.partial(_peer_topk_kernel, n_keys=n_keys, topk=PEER_TOPK)
    st = jax.ShapeDtypeStruct((heads * 2 * n_keys, t), jnp.float32)
    blk = pl.BlockSpec((2 * n_keys, tn), lambda i, h: (h, i))
    return pl.pallas_call(
        kern, out_shape=(st, st, jax.ShapeDtypeStruct((heads, 1, t), jnp.float32)), grid=(t // tn, heads),
        in_specs=[pl.BlockSpec((tn, 2 * half), lambda i, h: (i, h)),
                  pl.BlockSpec((1, 2, n_keys, half), lambda i, h: (h, 0, 0, 0))],
        out_specs=(blk, blk, pl.BlockSpec((1, 1, tn), lambda i, h: (h, 0, i))),
        compiler_params=_params("parallel", "parallel"), name="peer_topk",
    )(hq, keys)


def _peer_dense_kernel(h_ref, u_ref, v_ref, s_ref, e_ref, tau_ref, o_ref, ga_ref, *, heads, n_keys, n_blocks):
    j = pl.program_id(1)

    @pl.when(j == 0)
    def _():
        ga_ref[1] = jnp.zeros(ga_ref.shape[1:], ga_ref.dtype)
        o_ref[...] = jnp.zeros_like(o_ref)

    for parity in range(2):
        pl.when(j % 2 == parity)(functools.partial(
            _peer_dense_step, h_ref, u_ref, v_ref, s_ref, e_ref, tau_ref, o_ref, ga_ref,
            jnp.minimum(j, n_blocks - 1), parity, heads, n_keys))


def _peer_dense_step(h_ref, u_ref, v_ref, s_ref, e_ref, tau_ref, o_ref, ga_ref, jb, slot_now, heads, n_keys):
    tm, te = ga_ref.shape[1:]
    d = o_ref.shape[1]
    n_i = te // n_keys
    mxu = MXU_TILE
    n_tok_tiles = tm // mxu

    def link_from(x):
        bits = pltpu.bitcast(x[0:8, 0:LANES], jnp.uint32)
        zero = lax.shift_right_logical(lax.shift_right_logical(bits, jnp.uint32(16)), jnp.uint32(16))
        return pltpu.bitcast(zero, jnp.float32)

    pre_tiles = {}

    k_parts = PEER_KEY_SPLIT
    kw = d // k_parts
    done_tiles = set()

    def pre_tile(p, kk):
        mi, ni = divmod(p, n_tok_tiles)
        part = jnp.dot(u_ref[mi * mxu:(mi + 1) * mxu, kk * kw:(kk + 1) * kw],
                       h_ref[kk * kw:(kk + 1) * kw, ni * mxu:(ni + 1) * mxu], preferred_element_type=jnp.float32)
        pre_tiles[mi, ni] = part if kk == 0 else pre_tiles[mi, ni] + part
        if kk == k_parts - 1:
            done_tiles.add((mi, ni))
        return link_from(part)

    def value_chunk(c):
        cols = slice(c * mxu, (c + 1) * mxu)
        acc = o_ref[:, cols] + jnp.dot(ga_ref[1 - slot_now], v_ref[:, cols], preferred_element_type=jnp.float32)
        o_ref[:, cols] = acc
        return link_from(acc)

    n_pre = (te // mxu) * n_tok_tiles
    chunks = [functools.partial(pre_tile, p, kk) for p in range(n_pre) for kk in range(k_parts)]
    chunks += [functools.partial(value_chunk, c) for c in range(d // mxu)]
    results = []

    def finalize(il):
        mi, off = divmod(il * n_keys, mxu)
        while any((mi, ni) not in done_tiles for ni in range(n_tok_tiles)):
            results.append(chunks.pop(0)())
        p = jnp.concatenate([pre_tiles[mi, ni][off:off + n_keys, :] for ni in range(n_tok_tiles)], axis=1)
        act = 0.5 * p * (1.0 + lax.erf(p * (2.0 ** -0.5)))
        ga_ref[slot_now, :, il * n_keys:(il + 1) * n_keys] = (gates.pop(il) * act).astype(ga_ref.dtype).T

    pieces = [(il, h) for il in range(n_i) for h in range(heads)]
    costs = [mxu * kw * mxu] * (n_pre * k_parts) + [tm * te * mxu] * (d // mxu)
    ends = [sum(costs[:c + 1]) for c in range(len(costs))]
    slot = ends[-1] / len(pieces)
    linked = 0
    gates = {}
    link = link_from(s_ref[0:8, 0:LANES])
    for k, (il, h) in enumerate(pieces):
        while len(results) < len(costs) and ends[len(results)] - costs[len(results)] < (k + 1) * slot:
            results.append(chunks.pop(0)())
        while linked < len(results) and ends[linked] <= k * slot:
            link, linked = link + results[linked], linked + 1
        for _ in range(PEER_LINK_ROLLS):
            link = pltpu.roll(link, 1, 1)
        base = h * 2 * n_keys
        a_row = s_ref[pl.ds(base + jb * n_i + il, 1), :] + jnp.concatenate([link[0:1, :]] * (tm // LANES), axis=1)
        ea_row = e_ref[pl.ds(base + jb * n_i + il, 1), :]
        b_tile = s_ref[base + n_keys:base + 2 * n_keys, :]
        eb_tile = e_ref[base + n_keys:base + 2 * n_keys, :]
        term = jnp.where(a_row + b_tile >= tau_ref[h], ea_row * eb_tile, 0.0)
        gates[il] = term if h == 0 else gates[il] + term
        if h == heads - 1:
            finalize(il)
    while chunks:
        chunks.pop(0)()


def _peer_dense(h_bf_t, u_bf, v_bf, scores_t, exps_t, tau, heads, n_keys):
    d, t = h_bf_t.shape
    e = u_bf.shape[0]
    tm, te = _blk(t, 512), _blk(e, 512)
    nb = e // te
    kern = functools.partial(_peer_dense_kernel, heads=heads, n_keys=n_keys, n_blocks=nb)
    rows = scores_t.shape[0]
    once = pl.Buffered(1)
    return pl.pallas_call(
        kern, out_shape=jax.ShapeDtypeStruct((t, d), jnp.float32), grid=(t // tm, nb + 1),
        in_specs=[pl.BlockSpec((d, tm), lambda i, j: (0, i), pipeline_mode=once),
                  pl.BlockSpec((te, d), lambda i, j: (jnp.minimum(j, nb - 1), 0)),
                  pl.BlockSpec((te, d), lambda i, j: (jnp.maximum(j - 1, 0), 0)),
                  pl.BlockSpec((rows, tm), lambda i, j: (0, i), pipeline_mode=once),
                  pl.BlockSpec((rows, tm), lambda i, j: (0, i), pipeline_mode=once),
                  pl.BlockSpec((heads, 1, tm), lambda i, j: (0, 0, i))],
        out_specs=pl.BlockSpec((tm, d), lambda i, j: (i, 0)),
        scratch_shapes=[pltpu.VMEM((2, tm, te), jnp.bfloat16)],
        compiler_params=_params("parallel", "arbitrary"), name="peer_dense",
    )(h_bf_t, u_bf, v_bf, scores_t, exps_t, tau)


def kernel(x, positions, w_in, b_gate, lambda_q1, lambda_k1, lambda_q2, lambda_k2, subln_g, w_sb_branch,
           w_da_branch, w_out, ln1_g, ln1_b, peer_w_q, peer_sub_keys, peer_u, peer_v, ln2_g, ln2_b):
    batch, seq, d = x.shape
    depth = w_in.shape[0]
    hd = lambda_q1.shape[-1]
    sb_w, da_w = w_sb_branch.shape[1], w_da_branch.shape[1]
    sb_heads, da_heads = sb_w // hd, da_w // (2 * hd)
    n_qkv = 3 * sb_w + 3 * da_w
    alpha = (2 * depth) ** 0.25
    bf = jnp.bfloat16
    t = batch * seq

    cos, sin = _rope_tables(positions, hd)
    col_scale = np.ones((1, n_qkv), np.float32)
    col_scale[:, :sb_w] = hd ** -0.5 * math.log2(math.e)
    col_scale[:, 3 * sb_w:3 * sb_w + da_w] = hd ** -0.5 * math.log2(math.e)
    col_scale = jnp.asarray(col_scale)
    h = x.reshape(t, d)
    h_bf = h.astype(bf)
    for l in range(depth):
        lam_init = 0.8 - 0.6 * math.exp(-0.3 * l)
        w_in_bf = w_in[l].astype(bf)
        qkv = _qkv_proj(h_bf, w_in_bf, col_scale, cos, sin, n_qkv, (3 * sb_w, 3 * sb_w + 2 * da_w), hd)
        gate = _gate_proj(h_bf, w_in_bf, b_gate[l].reshape(1, -1), n_qkv, 2 * d)
        o_sb = _sb_attention(qkv, batch, seq, sb_heads, hd)
        o_da = _da_attention(qkv, (lambda_q1[l], lambda_k1[l], lambda_q2[l], lambda_k2[l]), subln_g[l],
                             batch, seq, da_heads, hd, 3 * sb_w, lam_init)
        merged = _merge(o_sb, o_da, w_sb_branch[l].astype(bf), w_da_branch[l].astype(bf), gate)
        mix = _matmul(merged, w_out[l].astype(bf), jnp.float32, "out_proj")
        h, h_bf, h_bf_t = _residual_ln(h, mix, ln1_g[l], ln1_b[l], alpha, True)

        heads, _, n_keys, half = peer_sub_keys[l].shape
        hq = _matmul(h_bf, peer_w_q[l].astype(bf), bf, "peer_query")
        scores_t, exps_t, tau = _peer_topk(hq, peer_sub_keys[l].astype(bf))
        ffn = _peer_dense(h_bf_t, peer_u[l].astype(bf), peer_v[l].astype(bf), scores_t, exps_t, tau, heads, n_keys)
        outs = _residual_ln(h, ffn, ln2_g[l], ln2_b[l], alpha, l + 1 < depth)
        h = outs[0]
        if l + 1 < depth:
            h_bf = outs[1]
    return h.reshape(batch, seq, d)
```

```python
import functools
import math

import numpy as np
import jax
import jax.numpy as jnp
from jax import lax
from jax.experimental import pallas as pl
from jax.experimental.pallas import tpu as pltpu

LN_EPS = 1e-5
ROPE_THETA = 10000.0
PEER_TOPK = 16
V7X_VMEM_LIMIT_BYTES = 56 * 1024 * 1024
LANES = 128
SUBLANES = 8
MXU_TILE = 256
PEER_KEY_ROWS = 256
PEER_KEY_SPLIT = 4
PEER_LINK_ROLLS = 1
NEG_INF = float("-inf")


def _params(*sem):
    return pltpu.CompilerParams(dimension_semantics=sem, vmem_limit_bytes=V7X_VMEM_LIMIT_BYTES)


def _blk(n, pref):
    b = min(n, pref)
    while n % b:
        b //= 2
    return b


def _rope_kernel(pos_ref, freq_ref, cos_ref, sin_ref):
    ang = pos_ref[...].astype(jnp.float32) * freq_ref[...]
    hd = freq_ref.shape[1]
    lane = lax.broadcasted_iota(jnp.int32, ang.shape, 1)
    cos_ref[...] = jnp.cos(ang)
    s = jnp.sin(ang)
    sin_ref[...] = jnp.where(lane < hd // 2, -s, s)


def _rope_tables(positions, hd):
    t = positions.size
    bm = _blk(t, 2048)
    inv = ROPE_THETA ** (-np.arange(0, hd, 2, dtype=np.float32) / np.float32(hd))
    freq = jnp.asarray(np.concatenate([inv, inv]).astype(np.float32).reshape(1, hd))
    out = jax.ShapeDtypeStruct((t, hd), jnp.float32)
    return pl.pallas_call(
        _rope_kernel, out_shape=(out, out), grid=(t // bm,),
        in_specs=[pl.BlockSpec((bm, 1), lambda i: (i, 0)), pl.BlockSpec((1, hd), lambda i: (0, 0))],
        out_specs=(pl.BlockSpec((bm, hd), lambda i: (i, 0)), pl.BlockSpec((bm, hd), lambda i: (i, 0))),
        compiler_params=_params("parallel"), name="rope_tables",
    )(positions.reshape(t, 1), freq)


def _qkv_kernel(x_ref, w_ref, cs_ref, cos_ref, sin_ref, o_ref, *, rope_lo, rope_hi, hd):
    acc = jnp.dot(x_ref[...], w_ref[...], preferred_element_type=jnp.float32) * cs_ref[...]
    j = pl.program_id(1)
    is_rope = jnp.logical_and(j >= rope_lo, j < rope_hi)

    @pl.when(is_rope)
    def _():
        c = cos_ref[...]
        s = sin_ref[...]
        for t in range(acc.shape[1] // hd):
            a = acc[:, t * hd:(t + 1) * hd]
            o_ref[:, t * hd:(t + 1) * hd] = (a * c + pltpu.roll(a, hd // 2, 1) * s).astype(o_ref.dtype)

    @pl.when(jnp.logical_not(is_rope))
    def _():
        o_ref[...] = acc.astype(o_ref.dtype)


def _qkv_proj(x, w, col_scale, cos, sin, n_out, rope_cols, hd):
    m, k = x.shape
    bm, bn = _blk(m, 1024), _blk(math.gcd(n_out, rope_cols[0], rope_cols[1]), 1024)
    kern = functools.partial(_qkv_kernel, rope_lo=rope_cols[0] // bn, rope_hi=rope_cols[1] // bn, hd=hd)
    return pl.pallas_call(
        kern, out_shape=jax.ShapeDtypeStruct((m, n_out), jnp.bfloat16), grid=(m // bm, n_out // bn),
        in_specs=[pl.BlockSpec((bm, k), lambda i, j: (i, 0)), pl.BlockSpec((k, bn), lambda i, j: (0, j)),
                  pl.BlockSpec((1, bn), lambda i, j: (0, j)),
                  pl.BlockSpec((bm, hd), lambda i, j: (i, 0)), pl.BlockSpec((bm, hd), lambda i, j: (i, 0))],
        out_specs=pl.BlockSpec((bm, bn), lambda i, j: (i, j)),
        compiler_params=_params("parallel", "parallel"), name="qkv_proj",
    )(x, w, col_scale, cos, sin)


def _gate_kernel(x_ref, w_ref, b_ref, o_ref):
    acc = jnp.dot(x_ref[...], w_ref[...], preferred_element_type=jnp.float32)
    o_ref[...] = jax.nn.sigmoid(acc + b_ref[...]).astype(o_ref.dtype)


def _gate_proj(x, w, bias, col0, n_out):
    m, k = x.shape
    bm, bn = _blk(m, 1024), _blk(math.gcd(n_out, col0), 1024)
    off = col0 // bn
    return pl.pallas_call(
        _gate_kernel, out_shape=jax.ShapeDtypeStruct((m, n_out), jnp.float32), grid=(m // bm, n_out // bn),
        in_specs=[pl.BlockSpec((bm, k), lambda i, j: (i, 0)), pl.BlockSpec((k, bn), lambda i, j: (0, j + off)),
                  pl.BlockSpec((1, bn), lambda i, j: (0, j))],
        out_specs=pl.BlockSpec((bm, bn), lambda i, j: (i, j)),
        compiler_params=_params("parallel", "parallel"), name="gate_proj",
    )(x, w, bias)


def _mm_kernel(a_ref, b_ref, o_ref):
    o_ref[...] = jnp.dot(a_ref[...], b_ref[...], preferred_element_type=jnp.float32).astype(o_ref.dtype)


def _matmul(a, b, out_dtype, name):
    m, k = a.shape
    n = b.shape[1]
    bm, bn = _blk(m, 1024), _blk(n, 1024)
    return pl.pallas_call(
        _mm_kernel, out_shape=jax.ShapeDtypeStruct((m, n), out_dtype), grid=(m // bm, n // bn),
        in_specs=[pl.BlockSpec((bm, k), lambda i, j: (i, 0)), pl.BlockSpec((k, bn), lambda i, j: (0, j))],
        out_specs=pl.BlockSpec((bm, bn), lambda i, j: (i, j)),
        compiler_params=_params("parallel", "parallel"), name=name,
    )(a, b)


def _merge_kernel(osb_ref, oda_ref, wsb_ref, wda_ref, gsb_ref, gda_ref, o_ref):
    ysb = jnp.dot(osb_ref[...], wsb_ref[...], preferred_element_type=jnp.float32)
    yda = jnp.dot(oda_ref[...], wda_ref[...], preferred_element_type=jnp.float32)
    o_ref[...] = (gsb_ref[...] * ysb + gda_ref[...] * yda).astype(o_ref.dtype)


def _merge(o_sb, o_da, w_sb, w_da, gate):
    m = o_sb.shape[0]
    d = w_sb.shape[1]
    bm, bn = _blk(m, 1024), _blk(d, 512)
    goff = d // bn
    return pl.pallas_call(
        _merge_kernel, out_shape=jax.ShapeDtypeStruct((m, d), jnp.bfloat16), grid=(m // bm, d // bn),
        in_specs=[pl.BlockSpec((bm, o_sb.shape[1]), lambda i, j: (i, 0)),
                  pl.BlockSpec((bm, o_da.shape[1]), lambda i, j: (i, 0)),
                  pl.BlockSpec((w_sb.shape[0], bn), lambda i, j: (0, j)),
                  pl.BlockSpec((w_da.shape[0], bn), lambda i, j: (0, j)),
                  pl.BlockSpec((bm, bn), lambda i, j: (i, j)),
                  pl.BlockSpec((bm, bn), lambda i, j: (i, j + goff))],
        out_specs=pl.BlockSpec((bm, bn), lambda i, j: (i, j)),
        compiler_params=_params("parallel", "parallel"), name="branch_merge",
    )(o_sb, o_da, w_sb, w_da, gate, gate)


def _ln_kernel(res_ref, y_ref, g_ref, b_ref, o_ref, *low_refs, alpha):
    z = alpha * res_ref[...] + y_ref[...]
    mu = jnp.mean(z, axis=-1, keepdims=True)
    zc = z - mu
    var = jnp.mean(zc * zc, axis=-1, keepdims=True)
    out = zc * lax.rsqrt(var + LN_EPS) * g_ref[...] + b_ref[...]
    o_ref[...] = out
    if low_refs:
        low = out.astype(low_refs[0].dtype)
        low_refs[0][...] = low
        low_refs[1][...] = low.T


def _residual_ln(res, y, g, b, alpha, with_bf16):
    m, d = res.shape
    bm = _blk(m, 256)
    row = pl.BlockSpec((bm, d), lambda i: (i, 0))
    vec = pl.BlockSpec((1, d), lambda i: (0, 0))
    out_shape = [jax.ShapeDtypeStruct((m, d), jnp.float32)]
    out_specs = [row]
    if with_bf16:
        out_shape += [jax.ShapeDtypeStruct((m, d), jnp.bfloat16), jax.ShapeDtypeStruct((d, m), jnp.bfloat16)]
        out_specs += [row, pl.BlockSpec((d, bm), lambda i: (0, i))]
    return pl.pallas_call(
        functools.partial(_ln_kernel, alpha=alpha), out_shape=tuple(out_shape),
        grid=(m // bm,), in_specs=[row, row, vec, vec], out_specs=tuple(out_specs),
        compiler_params=_params("parallel"), name="residual_layernorm",
    )(res, y, g.reshape(1, d), b.reshape(1, d))


SB_SKIP_LOG2 = -100.0 * math.log2(math.e)


def _sb_kernel(q_ref, k_ref, v_ref, tri_ref, o_ref, c_ref, acc_ref, *, blk, hd, hps):
    qi = pl.program_id(2)
    row = lax.broadcasted_iota(jnp.int32, (blk, blk), 0)
    col = lax.broadcasted_iota(jnp.int32, (blk, blk), 1)
    valid_diag = col < row
    tri = tri_ref[...]

    def visit(j, valid):
        start = pl.multiple_of(j * blk, blk)
        zs, log_fails = [], []
        for g in range(hps):
            q = q_ref[:, g * hd:(g + 1) * hd]
            k = k_ref[pl.ds(start, blk), g * hd:(g + 1) * hd]
            z = lax.dot_general(q, k, (((1,), (1,)), ((), ())), preferred_element_type=jnp.float32)
            log_fail = -(jnp.maximum(z, 0.0) + jnp.log2(1.0 + jnp.exp2(-jnp.abs(z))))
            if valid is not None:
                log_fail = jnp.where(valid, log_fail, 0.0)
            zs.append(z)
            log_fails.append(log_fail)
        sums_all = jnp.dot(jnp.concatenate([lf.astype(jnp.bfloat16) for lf in log_fails], axis=0), tri,
                           preferred_element_type=jnp.float32)
        for g in range(hps):
            v = v_ref[pl.ds(start, blk), g * hd:(g + 1) * hd]
            sums = sums_all[g * blk:(g + 1) * blk]
            c = c_ref[g]
            w = jnp.exp2(zs[g] + log_fails[g] + sums[:, :blk] + c)
            if valid is not None:
                w = jnp.where(valid, w, 0.0)
            acc_ref[g] += jnp.dot(w.astype(v.dtype), v, preferred_element_type=jnp.float32)
            c_ref[g] = c + sums[:, blk:]

    def bound():
        return jnp.max(c_ref[:, :, :hd])

    c_ref[...] = jnp.zeros_like(c_ref)
    acc_ref[...] = jnp.zeros_like(acc_ref)
    visit(qi, valid_diag)

    def cond(state):
        j, worst = state
        return jnp.logical_and(j >= 0, worst > SB_SKIP_LOG2)

    def body(state):
        j, _ = state
        visit(j, None)
        return j - 1, bound()

    lax.while_loop(cond, body, (qi - 1, bound()))
    for g in range(hps):
        o_ref[:, g * hd:(g + 1) * hd] = acc_ref[g].astype(o_ref.dtype)


def _sb_attention(qkv, batch, seq, heads, hd):
    blk = _blk(seq, 256)
    nq = seq // blk
    hps = max(h for h in (4, 2, 1) if heads % h == 0)
    hg, w = heads // hps, hps * hd
    tri = np.concatenate([np.tril(np.ones((blk, blk), np.float32), -1), np.ones((blk, blk), np.float32)], axis=1)
    kern = functools.partial(_sb_kernel, blk=blk, hd=hd, hps=hps)
    return pl.pallas_call(
        kern, out_shape=jax.ShapeDtypeStruct((batch * seq, heads * hd), jnp.bfloat16), grid=(batch, hg, nq),
        in_specs=[pl.BlockSpec((blk, w), lambda b, h, i: (b * nq + i, h)),
                  pl.BlockSpec((seq, w), lambda b, h, i: (b, hg + h)),
                  pl.BlockSpec((seq, w), lambda b, h, i: (b, 2 * hg + h)),
                  pl.BlockSpec((blk, 2 * blk), lambda b, h, i: (0, 0))],
        out_specs=pl.BlockSpec((blk, w), lambda b, h, i: (b * nq + i, h)),
        scratch_shapes=[pltpu.VMEM((hps, blk, blk), jnp.float32), pltpu.VMEM((hps, blk, hd), jnp.float32)],
        compiler_params=_params("parallel", "parallel", "parallel"), name="stick_breaking_attention",
    )(qkv, qkv, qkv, jnp.asarray(tri, jnp.bfloat16))


def _da_kernel(lq1_ref, lk1_ref, lq2_ref, lk2_ref, g_ref, q1_ref, q2_ref, k1_ref, k2_ref, v_ref, o_ref,
               m_ref, l_ref, acc_ref, *, blk, hd, lam_init):
    qi = pl.program_id(2)
    row = lax.broadcasted_iota(jnp.int32, (blk, blk), 0)
    col = lax.broadcasted_iota(jnp.int32, (blk, blk), 1)
    causal_diag = col <= row
    qs = (q1_ref[...], q2_ref[...])
    ks = (k1_ref, k2_ref)

    m_ref[...] = jnp.full_like(m_ref, NEG_INF)
    l_ref[...] = jnp.zeros_like(l_ref)
    acc_ref[...] = jnp.zeros_like(acc_ref)

    def visit(j, mask):
        start = pl.multiple_of(j * blk, blk)
        v = v_ref[pl.ds(start, blk), :]
        probs, corrs = [], []
        for s in range(2):
            k = ks[s][pl.ds(start, blk), :]
            sc = lax.dot_general(qs[s], k, (((1,), (1,)), ((), ())), preferred_element_type=jnp.float32)
            if mask is not None:
                sc = jnp.where(mask, sc, NEG_INF)
            m_old = m_ref[s]
            m_new = jnp.maximum(m_old, jnp.max(sc, axis=-1, keepdims=True))
            p = jnp.exp2(sc - jnp.concatenate([m_new] * (blk // hd), axis=1))
            corr = jnp.exp2(m_old - m_new)
            l_ref[s] = corr * l_ref[s] + jnp.sum(p, axis=-1, keepdims=True)
            m_ref[s] = m_new
            probs.append(p.astype(v.dtype))
            corrs.append(jnp.concatenate([corr, corr], axis=1))
        pv = jnp.dot(jnp.concatenate(probs, axis=0), v, preferred_element_type=jnp.float32)
        for s in range(2):
            acc_ref[s] = corrs[s] * acc_ref[s] + pv[s * blk:(s + 1) * blk]

    def body(j, carry):
        visit(j, None)
        return carry

    lax.fori_loop(0, qi, body, 0)
    visit(qi, causal_diag)

    lam = (jnp.exp(jnp.sum(lq1_ref[...] * lk1_ref[...], axis=-1, keepdims=True))
           - jnp.exp(jnp.sum(lq2_ref[...] * lk2_ref[...], axis=-1, keepdims=True)) + lam_init)
    l1 = jnp.concatenate([l_ref[0], l_ref[0]], axis=1)
    l2 = jnp.concatenate([l_ref[1], l_ref[1]], axis=1)
    o = acc_ref[0] / l1 - lam * (acc_ref[1] / l2)
    o = o * lax.rsqrt(jnp.mean(o * o, axis=-1, keepdims=True) + LN_EPS) * g_ref[...] * (1.0 - lam_init)
    o_ref[...] = o.astype(o_ref.dtype)


def _da_attention(qkv, lambdas, subln_g, batch, seq, heads, hd, col0, lam_init):
    blk = _blk(seq, 512)
    nq = seq // blk
    qb, kb, vb = col0 // hd, col0 // hd + 2 * heads, col0 // (2 * hd) + 2 * heads
    vec = pl.BlockSpec((1, hd), lambda b, h, i: (0, 0))
    kern = functools.partial(_da_kernel, blk=blk, hd=hd, lam_init=lam_init)
    return pl.pallas_call(
        kern, out_shape=jax.ShapeDtypeStruct((batch * seq, heads * 2 * hd), jnp.bfloat16), grid=(batch, heads, nq),
        in_specs=[vec, vec, vec, vec, pl.BlockSpec((1, 2 * hd), lambda b, h, i: (0, 0)),
                  pl.BlockSpec((blk, hd), lambda b, h, i: (b * nq + i, qb + 2 * h)),
                  pl.BlockSpec((blk, hd), lambda b, h, i: (b * nq + i, qb + 2 * h + 1)),
                  pl.BlockSpec((seq, hd), lambda b, h, i: (b, kb + 2 * h)),
                  pl.BlockSpec((seq, hd), lambda b, h, i: (b, kb + 2 * h + 1)),
                  pl.BlockSpec((seq, 2 * hd), lambda b, h, i: (b, vb + h))],
        out_specs=pl.BlockSpec((blk, 2 * hd), lambda b, h, i: (b * nq + i, h)),
        scratch_shapes=[pltpu.VMEM((2, blk, hd), jnp.float32), pltpu.VMEM((2, blk, hd), jnp.float32),
                        pltpu.VMEM((2, blk, 2 * hd), jnp.float32)],
        compiler_params=_params("parallel", "parallel", "parallel"), name="differential_attention",
    )(*[l.reshape(1, hd) for l in lambdas], subln_g.reshape(1, 2 * hd), qkv, qkv, qkv, qkv, qkv)


def _extract_top(x, n):
    rows = lax.broadcasted_iota(jnp.int32, x.shape, 0).astype(jnp.float32)
    tops = []
    for _ in range(n):
        m = jnp.max(x, axis=0, keepdims=True)
        tops.append(m)
        first = jnp.min(jnp.where(x == m, rows, float(x.shape[0])), axis=0, keepdims=True)
        x = jnp.where(rows == first, NEG_INF, x)
    return tops


def _sorting_network(n):
    pairs, p = [], 1
    while p < n:
        k = p
        while k >= 1:
            for j in range(k % p, n - k, 2 * k):
                for i in range(min(k, n - j - k)):
                    if (i + j) // (2 * p) == (i + j + k) // (2 * p):
                        pairs.append((i + j, i + j + k))
            k //= 2
        p *= 2
    return pairs


def _extract_top_sorted(x, n):
    groups = [x[g * SUBLANES:(g + 1) * SUBLANES] for g in range(x.shape[0] // SUBLANES)]
    for a, b in _sorting_network(len(groups)):
        groups[a], groups[b] = jnp.maximum(groups[a], groups[b]), jnp.minimum(groups[a], groups[b])
    lane_list = lax.broadcasted_iota(jnp.int32, groups[0].shape, 0).astype(jnp.float32)
    tops = []
    for t in range(n):
        head = groups[0]
        m = jnp.max(head, axis=0, keepdims=True)
        tops.append(m)
        if t == n - 1:
            break
        first = jnp.min(jnp.where(head == m, lane_list, float(SUBLANES)), axis=0, keepdims=True)
        popped = lane_list == first
        live = min(len(groups), n - t)
        for g in range(live):
            below = groups[g + 1] if g + 1 < len(groups) else NEG_INF
            groups[g] = jnp.where(popped, below, groups[g])
    return tops


def _pair_candidates(top_a, top_b, combine):
    k = len(top_a)
    sub = SUBLANES
    b_all = jnp.concatenate(top_b, axis=0)
    pieces, masks, singles = [], [], []
    for p in range(k):
        n_q = k // (p + 1)
        if n_q == 1:
            singles.append(top_a[p])
            continue
        rows = -(-n_q // sub) * sub
        pieces.append(combine(top_a[p], b_all[0:rows]))
        masks.append(lax.broadcasted_iota(jnp.int32, (rows, 1), 0) < n_q)
    if singles:
        pieces.append(combine(jnp.concatenate(singles, axis=0), top_b[0]))
        masks.append(None)
    return pieces, masks


def _peer_topk_kernel(q_ref, keys_ref, s_ref, e_ref, tau_ref, *, n_keys, topk):
    half = q_ref.shape[1] // 2
    scores, tops = [], []
    for c in range(2):
        s = lax.dot_general(keys_ref[0, c], q_ref[:, c * half:(c + 1) * half], (((1,), (1,)), ((), ())),
                            preferred_element_type=jnp.float32)
        scores.append(s)
        n_groups = n_keys // SUBLANES
        pow2 = n_keys % SUBLANES == 0 and n_groups & (n_groups - 1) == 0
        tops.append(_extract_top_sorted(s, topk) if pow2 else _extract_top(s, topk))
    exps = [[jnp.exp(t - top[0]) for t in top] for top in tops]
    sums, masks = _pair_candidates(tops[0], tops[1], jnp.add)
    prods, _ = _pair_candidates(exps[0], exps[1], jnp.multiply)
    cand = jnp.concatenate([s if m is None else jnp.where(m, s, NEG_INF) for s, m in zip(sums, masks)], axis=0)
    cand_e = jnp.concatenate(prods, axis=0)
    groups = -(-cand.shape[0] // SUBLANES)
    pad_rows = SUBLANES * (1 << (groups - 1).bit_length()) - cand.shape[0]
    padded = jnp.concatenate([cand, jnp.full((pad_rows, cand.shape[1]), NEG_INF, cand.dtype)], axis=0)
    tau = _extract_top_sorted(padded, topk)[-1]
    z = jnp.sum(jnp.where(cand >= tau, cand_e, 0.0), axis=0, keepdims=True)
    s_ref[...] = jnp.concatenate(scores, axis=0)
    e_ref[...] = jnp.concatenate([jnp.exp(scores[0] - tops[0][0]) / z, jnp.exp(scores[1] - tops[1][0])], axis=0)
    tau_ref[0] = tau


def _peer_topk(hq, keys):
    t = hq.shape[0]
    heads, _, n_keys, half = keys.shape
    tn = _blk(t, 512)
    kern = functools.partial(_peer_topk_kernel, n_keys=n_keys, topk=PEER_TOPK)
    st = jax.ShapeDtypeStruct((heads * 2 * n_keys, t), jnp.float32)
    blk = pl.BlockSpec((2 * n_keys, tn), lambda i, h: (h, i))
    return pl.pallas_call(
        kern, out_shape=(st, st, jax.ShapeDtypeStruct((heads, 1, t), jnp.float32)), grid=(t // tn, heads),
        in_specs=[pl.BlockSpec((tn, 2 * half), lambda i, h: (i, h)),
                  pl.BlockSpec((1, 2, n_keys, half), lambda i, h: (h, 0, 0, 0))],
        out_specs=(blk, blk, pl.BlockSpec((1, 1, tn), lambda i, h: (h, 0, i))),
        compiler_params=_params("parallel", "parallel"), name="peer_topk",
    )(hq, keys)


def _peer_dense_kernel(h_ref, u_ref, v_ref, s_ref, e_ref, tau_ref, o_ref, ga_ref, *, heads, n_keys, n_blocks):
    j = pl.program_id(1)

    @pl.when(j == 0)
    def _():
        ga_ref[1] = jnp.zeros(ga_ref.shape[1:], ga_ref.dtype)
        o_ref[...] = jnp.zeros_like(o_ref)

    for parity in range(2):
        pl.when(j % 2 == parity)(functools.partial(
            _peer_dense_step, h_ref, u_ref, v_ref, s_ref, e_ref, tau_ref, o_ref, ga_ref,
            jnp.minimum(j, n_blocks - 1), parity, heads, n_keys))


def _peer_dense_step(h_ref, u_ref, v_ref, s_ref, e_ref, tau_ref, o_ref, ga_ref, jb, slot_now, heads, n_keys):
    tm, te = ga_ref.shape[1:]
    d = o_ref.shape[1]
    n_i = te // n_keys
    mxu = MXU_TILE
    n_tok_tiles = tm // mxu
    rt = min(te, PEER_KEY_ROWS)

    def link_from(x):
        bits = pltpu.bitcast(x[0:8, 0:LANES], jnp.uint32)
        zero = lax.shift_right_logical(lax.shift_right_logical(bits, jnp.uint32(16)), jnp.uint32(16))
        return pltpu.bitcast(zero, jnp.float32)

    pre_tiles = {}

    k_parts = PEER_KEY_SPLIT
    kw = d // k_parts
    done_tiles = set()

    def pre_tile(p, kk):
        mi, ni = divmod(p, n_tok_tiles)
        part = jnp.dot(u_ref[mi * rt:(mi + 1) * rt, kk * kw:(kk + 1) * kw],
                       h_ref[kk * kw:(kk + 1) * kw, ni * mxu:(ni + 1) * mxu], preferred_element_type=jnp.float32)
        pre_tiles[mi, ni] = part if kk == 0 else pre_tiles[mi, ni] + part
        if kk == k_parts - 1:
            done_tiles.add((mi, ni))
        return link_from(part)

    def value_chunk(c):
        cols = slice(c * mxu, (c + 1) * mxu)
        acc = o_ref[:, cols] + jnp.dot(ga_ref[1 - slot_now], v_ref[:, cols], preferred_element_type=jnp.float32)
        o_ref[:, cols] = acc
        return link_from(acc)

    n_pre = (te // rt) * n_tok_tiles
    chunks = [functools.partial(pre_tile, p, kk) for p in range(n_pre) for kk in range(k_parts)]
    chunks += [functools.partial(value_chunk, c) for c in range(d // mxu)]
    results = []

    def finalize(il):
        mi, off = divmod(il * n_keys, rt)
        while any((mi, ni) not in done_tiles for ni in range(n_tok_tiles)):
            results.append(chunks.pop(0)())
        p = jnp.concatenate([pre_tiles[mi, ni][off:off + n_keys, :] for ni in range(n_tok_tiles)], axis=1)
        act = 0.5 * p * (1.0 + lax.erf(p * (2.0 ** -0.5)))
        ga_ref[slot_now, :, il * n_keys:(il + 1) * n_keys] = (gates.pop(il) * act).astype(ga_ref.dtype).T

    pieces = [(il, h) for il in range(n_i) for h in range(heads)]
    costs = [rt * kw * mxu] * (n_pre * k_parts) + [tm * te * mxu] * (d // mxu)
    ends = [sum(costs[:c + 1]) for c in range(len(costs))]
    slot = ends[-1] / len(pieces)
    linked = 0
    gates = {}
    link = link_from(s_ref[0:8, 0:LANES])
    for k, (il, h) in enumerate(pieces):
        while len(results) < len(costs) and ends[len(results)] - costs[len(results)] < (k + 1) * slot:
            results.append(chunks.pop(0)())
        while linked < len(results) and ends[linked] <= k * slot:
            link, linked = link + results[linked], linked + 1
        for _ in range(PEER_LINK_ROLLS):
            link = pltpu.roll(link, 1, 1)
        base = h * 2 * n_keys
        a_row = s_ref[pl.ds(base + jb * n_i + il, 1), :] + jnp.concatenate([link[0:1, :]] * (tm // LANES), axis=1)
        ea_row = e_ref[pl.ds(base + jb * n_i + il, 1), :]
        b_tile = s_ref[base + n_keys:base + 2 * n_keys, :]
        eb_tile = e_ref[base + n_keys:base + 2 * n_keys, :]
        term = jnp.where(a_row + b_tile >= tau_ref[h], ea_row * eb_tile, 0.0)
        gates[il] = term if h == 0 else gates[il] + term
        if h == heads - 1:
            finalize(il)
    while chunks:
        chunks.pop(0)()


def _peer_dense(h_bf_t, u_bf, v_bf, scores_t, exps_t, tau, heads, n_keys):
    d, t = h_bf_t.shape
    e = u_bf.shape[0]
    tm, te = _blk(t, 512), _blk(e, 512)
    nb = e // te
    kern = functools.partial(_peer_dense_kernel, heads=heads, n_keys=n_keys, n_blocks=nb)
    rows = scores_t.shape[0]
    once = pl.Buffered(1)
    return pl.pallas_call(
        kern, out_shape=jax.ShapeDtypeStruct((t, d), jnp.float32), grid=(t // tm, nb + 1),
        in_specs=[pl.BlockSpec((d, tm), lambda i, j: (0, i), pipeline_mode=once),
                  pl.BlockSpec((te, d), lambda i, j: (jnp.minimum(j, nb - 1), 0)),
                  pl.BlockSpec((te, d), lambda i, j: (jnp.maximum(j - 1, 0), 0)),
                  pl.BlockSpec((rows, tm), lambda i, j: (0, i), pipeline_mode=once),
                  pl.BlockSpec((rows, tm), lambda i, j: (0, i), pipeline_mode=once),
                  pl.BlockSpec((heads, 1, tm), lambda i, j: (0, 0, i))],
        out_specs=pl.BlockSpec((tm, d), lambda i, j: (i, 0)),
        scratch_shapes=[pltpu.VMEM((2, tm, te), jnp.bfloat16)],
        compiler_params=_params("parallel", "arbitrary"), name="peer_dense",
    )(h_bf_t, u_bf, v_bf, scores_t, exps_t, tau)


def kernel(x, positions, w_in, b_gate, lambda_q1, lambda_k1, lambda_q2, lambda_k2, subln_g, w_sb_branch,
           w_da_branch, w_out, ln1_g, ln1_b, peer_w_q, peer_sub_keys, peer_u, peer_v, ln2_g, ln2_b):
    batch, seq, d = x.shape
    depth = w_in.shape[0]
    hd = lambda_q1.shape[-1]
    sb_w, da_w = w_sb_branch.shape[1], w_da_branch.shape[1]
    sb_heads, da_heads = sb_w // hd, da_w // (2 * hd)
    n_qkv = 3 * sb_w + 3 * da_w
    alpha = (2 * depth) ** 0.25
    bf = jnp.bfloat16
    t = batch * seq

    cos, sin = _rope_tables(positions, hd)
    col_scale = np.ones((1, n_qkv), np.float32)
    col_scale[:, :sb_w] = hd ** -0.5 * math.log2(math.e)
    col_scale[:, 3 * sb_w:3 * sb_w + da_w] = hd ** -0.5 * math.log2(math.e)
    col_scale = jnp.asarray(col_scale)
    h = x.reshape(t, d)
    h_bf = h.astype(bf)
    for l in range(depth):
        lam_init = 0.8 - 0.6 * math.exp(-0.3 * l)
        w_in_bf = w_in[l].astype(bf)
        qkv = _qkv_proj(h_bf, w_in_bf, col_scale, cos, sin, n_qkv, (3 * sb_w, 3 * sb_w + 2 * da_w), hd)
        gate = _gate_proj(h_bf, w_in_bf, b_gate[l].reshape(1, -1), n_qkv, 2 * d)
        o_sb = _sb_attention(qkv, batch, seq, sb_heads, hd)
        o_da = _da_attention(qkv, (lambda_q1[l], lambda_k1[l], lambda_q2[l], lambda_k2[l]), subln_g[l],
                             batch, seq, da_heads, hd, 3 * sb_w, lam_init)
        merged = _merge(o_sb, o_da, w_sb_branch[l].astype(bf), w_da_branch[l].astype(bf), gate)
        mix = _matmul(merged, w_out[l].astype(bf), jnp.float32, "out_proj")
        h, h_bf, h_bf_t = _residual_ln(h, mix, ln1_g[l], ln1_b[l], alpha, True)

        heads, _, n_keys, half = peer_sub_keys[l].shape
        hq = _matmul(h_bf, peer_w_q[l].astype(bf), bf, "peer_query")
        scores_t, exps_t, tau = _peer_topk(hq, peer_sub_keys[l].astype(bf))
        ffn = _peer_dense(h_bf_t, peer_u[l].astype(bf), peer_v[l].astype(bf), scores_t, exps_t, tau, heads, n_keys)
        outs = _residual_ln(h, ffn, ln2_g[l], ln2_b[l], alpha, l + 1 < depth)
        h = outs[0]
        if l + 1 < depth:
            h_bf = outs[1]
    return h.reshape(batch, seq, d)
```

```python
import functools
import math

import numpy as np
import jax
import jax.numpy as jnp
from jax import lax
from jax.experimental import pallas as pl
from jax.experimental.pallas import tpu as pltpu

LN_EPS = 1e-5
ROPE_THETA = 10000.0
PEER_TOPK = 16
V7X_VMEM_LIMIT_BYTES = 56 * 1024 * 1024
LANES = 128
SUBLANES = 8
MXU_TILE = 256
PEER_TOKEN_BLOCK = 512
PEER_EXPERT_BLOCK = 1024
PEER_VMEM_LIMIT_BYTES = 60 * 1024 * 1024
PEER_GATE_SPAN = 1.0
PEER_KEY_ROWS = 256
PEER_KEY_SPLIT = 4
PEER_LINK_ROLLS = 1
NEG_INF = float("-inf")


def _params(*sem):
    return pltpu.CompilerParams(dimension_semantics=sem, vmem_limit_bytes=V7X_VMEM_LIMIT_BYTES)


def _blk(n, pref):
    b = min(n, pref)
    while n % b:
        b //= 2
    return b


def _rope_kernel(pos_ref, freq_ref, cos_ref, sin_ref):
    ang = pos_ref[...].astype(jnp.float32) * freq_ref[...]
    hd = freq_ref.shape[1]
    lane = lax.broadcasted_iota(jnp.int32, ang.shape, 1)
    cos_ref[...] = jnp.cos(ang)
    s = jnp.sin(ang)
    sin_ref[...] = jnp.where(lane < hd // 2, -s, s)


def _rope_tables(positions, hd):
    t = positions.size
    bm = _blk(t, 2048)
    inv = ROPE_THETA ** (-np.arange(0, hd, 2, dtype=np.float32) / np.float32(hd))
    freq = jnp.asarray(np.concatenate([inv, inv]).astype(np.float32).reshape(1, hd))
    out = jax.ShapeDtypeStruct((t, hd), jnp.float32)
    return pl.pallas_call(
        _rope_kernel, out_shape=(out, out), grid=(t // bm,),
        in_specs=[pl.BlockSpec((bm, 1), lambda i: (i, 0)), pl.BlockSpec((1, hd), lambda i: (0, 0))],
        out_specs=(pl.BlockSpec((bm, hd), lambda i: (i, 0)), pl.BlockSpec((bm, hd), lambda i: (i, 0))),
        compiler_params=_params("parallel"), name="rope_tables",
    )(positions.reshape(t, 1), freq)


def _qkv_kernel(x_ref, w_ref, cs_ref, cos_ref, sin_ref, o_ref, *, rope_lo, rope_hi, hd):
    acc = jnp.dot(x_ref[...], w_ref[...], preferred_element_type=jnp.float32) * cs_ref[...]
    j = pl.program_id(1)
    is_rope = jnp.logical_and(j >= rope_lo, j < rope_hi)

    @pl.when(is_rope)
    def _():
        c = cos_ref[...]
        s = sin_ref[...]
        for t in range(acc.shape[1] // hd):
            a = acc[:, t * hd:(t + 1) * hd]
            o_ref[:, t * hd:(t + 1) * hd] = (a * c + pltpu.roll(a, hd // 2, 1) * s).astype(o_ref.dtype)

    @pl.when(jnp.logical_not(is_rope))
    def _():
        o_ref[...] = acc.astype(o_ref.dtype)


def _qkv_proj(x, w, col_scale, cos, sin, n_out, rope_cols, hd):
    m, k = x.shape
    bm, bn = _blk(m, 1024), _blk(math.gcd(n_out, rope_cols[0], rope_cols[1]), 1024)
    kern = functools.partial(_qkv_kernel, rope_lo=rope_cols[0] // bn, rope_hi=rope_cols[1] // bn, hd=hd)
    return pl.pallas_call(
        kern, out_shape=jax.ShapeDtypeStruct((m, n_out), jnp.bfloat16), grid=(m // bm, n_out // bn),
        in_specs=[pl.BlockSpec((bm, k), lambda i, j: (i, 0)), pl.BlockSpec((k, bn), lambda i, j: (0, j)),
                  pl.BlockSpec((1, bn), lambda i, j: (0, j)),
                  pl.BlockSpec((bm, hd), lambda i, j: (i, 0)), pl.BlockSpec((bm, hd), lambda i, j: (i, 0))],
        out_specs=pl.BlockSpec((bm, bn), lambda i, j: (i, j)),
        compiler_params=_params("parallel", "parallel"), name="qkv_proj",
    )(x, w, col_scale, cos, sin)


def _gate_kernel(x_ref, w_ref, b_ref, o_ref):
    acc = jnp.dot(x_ref[...], w_ref[...], preferred_element_type=jnp.float32)
    o_ref[...] = jax.nn.sigmoid(acc + b_ref[...]).astype(o_ref.dtype)


def _gate_proj(x, w, bias, col0, n_out):
    m, k = x.shape
    bm, bn = _blk(m, 1024), _blk(math.gcd(n_out, col0), 1024)
    off = col0 // bn
    return pl.pallas_call(
        _gate_kernel, out_shape=jax.ShapeDtypeStruct((m, n_out), jnp.float32), grid=(m // bm, n_out // bn),
        in_specs=[pl.BlockSpec((bm, k), lambda i, j: (i, 0)), pl.BlockSpec((k, bn), lambda i, j: (0, j + off)),
                  pl.BlockSpec((1, bn), lambda i, j: (0, j))],
        out_specs=pl.BlockSpec((bm, bn), lambda i, j: (i, j)),
        compiler_params=_params("parallel", "parallel"), name="gate_proj",
    )(x, w, bias)


def _mm_kernel(a_ref, b_ref, o_ref):
    o_ref[...] = jnp.dot(a_ref[...], b_ref[...], preferred_element_type=jnp.float32).astype(o_ref.dtype)


def _matmul(a, b, out_dtype, name):
    m, k = a.shape
    n = b.shape[1]
    bm, bn = _blk(m, 1024), _blk(n, 1024)
    return pl.pallas_call(
        _mm_kernel, out_shape=jax.ShapeDtypeStruct((m, n), out_dtype), grid=(m // bm, n // bn),
        in_specs=[pl.BlockSpec((bm, k), lambda i, j: (i, 0)), pl.BlockSpec((k, bn), lambda i, j: (0, j))],
        out_specs=pl.BlockSpec((bm, bn), lambda i, j: (i, j)),
        compiler_params=_params("parallel", "parallel"), name=name,
    )(a, b)


def _merge_kernel(osb_ref, oda_ref, wsb_ref, wda_ref, gsb_ref, gda_ref, o_ref):
    ysb = jnp.dot(osb_ref[...], wsb_ref[...], preferred_element_type=jnp.float32)
    yda = jnp.dot(oda_ref[...], wda_ref[...], preferred_element_type=jnp.float32)
    o_ref[...] = (gsb_ref[...] * ysb + gda_ref[...] * yda).astype(o_ref.dtype)


def _merge(o_sb, o_da, w_sb, w_da, gate):
    m = o_sb.shape[0]
    d = w_sb.shape[1]
    bm, bn = _blk(m, 1024), _blk(d, 512)
    goff = d // bn
    return pl.pallas_call(
        _merge_kernel, out_shape=jax.ShapeDtypeStruct((m, d), jnp.bfloat16), grid=(m // bm, d // bn),
        in_specs=[pl.BlockSpec((bm, o_sb.shape[1]), lambda i, j: (i, 0)),
                  pl.BlockSpec((bm, o_da.shape[1]), lambda i, j: (i, 0)),
                  pl.BlockSpec((w_sb.shape[0], bn), lambda i, j: (0, j)),
                  pl.BlockSpec((w_da.shape[0], bn), lambda i, j: (0, j)),
                  pl.BlockSpec((bm, bn), lambda i, j: (i, j)),
                  pl.BlockSpec((bm, bn), lambda i, j: (i, j + goff))],
        out_specs=pl.BlockSpec((bm, bn), lambda i, j: (i, j)),
        compiler_params=_params("parallel", "parallel"), name="branch_merge",
    )(o_sb, o_da, w_sb, w_da, gate, gate)


def _ln_kernel(res_ref, y_ref, g_ref, b_ref, o_ref, *low_refs, alpha):
    z = alpha * res_ref[...] + y_ref[...]
    mu = jnp.mean(z, axis=-1, keepdims=True)
    zc = z - mu
    var = jnp.mean(zc * zc, axis=-1, keepdims=True)
    out = zc * lax.rsqrt(var + LN_EPS) * g_ref[...] + b_ref[...]
    o_ref[...] = out
    if low_refs:
        low = out.astype(low_refs[0].dtype)
        low_refs[0][...] = low
        low_refs[1][...] = low.T


def _residual_ln(res, y, g, b, alpha, with_bf16):
    m, d = res.shape
    bm = _blk(m, 256)
    row = pl.BlockSpec((bm, d), lambda i: (i, 0))
    vec = pl.BlockSpec((1, d), lambda i: (0, 0))
    out_shape = [jax.ShapeDtypeStruct((m, d), jnp.float32)]
    out_specs = [row]
    if with_bf16:
        out_shape += [jax.ShapeDtypeStruct((m, d), jnp.bfloat16), jax.ShapeDtypeStruct((d, m), jnp.bfloat16)]
        out_specs += [row, pl.BlockSpec((d, bm), lambda i: (0, i))]
    return pl.pallas_call(
        functools.partial(_ln_kernel, alpha=alpha), out_shape=tuple(out_shape),
        grid=(m // bm,), in_specs=[row, row, vec, vec], out_specs=tuple(out_specs),
        compiler_params=_params("parallel"), name="residual_layernorm",
    )(res, y, g.reshape(1, d), b.reshape(1, d))


SB_SKIP_LOG2 = -100.0 * math.log2(math.e)


def _sb_kernel(q_ref, k_ref, v_ref, tri_ref, o_ref, c_ref, acc_ref, *, blk, hd, hps):
    qi = pl.program_id(2)
    row = lax.broadcasted_iota(jnp.int32, (blk, blk), 0)
    col = lax.broadcasted_iota(jnp.int32, (blk, blk), 1)
    valid_diag = col < row
    tri = tri_ref[...]

    def visit(j, valid):
        start = pl.multiple_of(j * blk, blk)
        zs, log_fails = [], []
        for g in range(hps):
            q = q_ref[:, g * hd:(g + 1) * hd]
            k = k_ref[pl.ds(start, blk), g * hd:(g + 1) * hd]
            z = lax.dot_general(q, k, (((1,), (1,)), ((), ())), preferred_element_type=jnp.float32)
            log_fail = -(jnp.maximum(z, 0.0) + jnp.log2(1.0 + jnp.exp2(-jnp.abs(z))))
            if valid is not None:
                log_fail = jnp.where(valid, log_fail, 0.0)
            zs.append(z)
            log_fails.append(log_fail)
        sums_all = jnp.dot(jnp.concatenate([lf.astype(jnp.bfloat16) for lf in log_fails], axis=0), tri,
                           preferred_element_type=jnp.float32)
        for g in range(hps):
            v = v_ref[pl.ds(start, blk), g * hd:(g + 1) * hd]
            sums = sums_all[g * blk:(g + 1) * blk]
            c = c_ref[g]
            w = jnp.exp2(zs[g] + log_fails[g] + sums[:, :blk] + c)
            if valid is not None:
                w = jnp.where(valid, w, 0.0)
            acc_ref[g] += jnp.dot(w.astype(v.dtype), v, preferred_element_type=jnp.float32)
            c_ref[g] = c + sums[:, blk:]

    def bound():
        return jnp.max(c_ref[:, :, :hd])

    c_ref[...] = jnp.zeros_like(c_ref)
    acc_ref[...] = jnp.zeros_like(acc_ref)
    visit(qi, valid_diag)

    def cond(state):
        j, worst = state
        return jnp.logical_and(j >= 0, worst > SB_SKIP_LOG2)

    def body(state):
        j, _ = state
        visit(j, None)
        return j - 1, bound()

    lax.while_loop(cond, body, (qi - 1, bound()))
    for g in range(hps):
        o_ref[:, g * hd:(g + 1) * hd] = acc_ref[g].astype(o_ref.dtype)


def _sb_attention(qkv, batch, seq, heads, hd):
    blk = _blk(seq, 256)
    nq = seq // blk
    hps = max(h for h in (4, 2, 1) if heads % h == 0)
    hg, w = heads // hps, hps * hd
    tri = np.concatenate([np.tril(np.ones((blk, blk), np.float32), -1), np.ones((blk, blk), np.float32)], axis=1)
    kern = functools.partial(_sb_kernel, blk=blk, hd=hd, hps=hps)
    return pl.pallas_call(
        kern, out_shape=jax.ShapeDtypeStruct((batch * seq, heads * hd), jnp.bfloat16), grid=(batch, hg, nq),
        in_specs=[pl.BlockSpec((blk, w), lambda b, h, i: (b * nq + i, h)),
                  pl.BlockSpec((seq, w), lambda b, h, i: (b, hg + h)),
                  pl.BlockSpec((seq, w), lambda b, h, i: (b, 2 * hg + h)),
                  pl.BlockSpec((blk, 2 * blk), lambda b, h, i: (0, 0))],
        out_specs=pl.BlockSpec((blk, w), lambda b, h, i: (b * nq + i, h)),
        scratch_shapes=[pltpu.VMEM((hps, blk, blk), jnp.float32), pltpu.VMEM((hps, blk, hd), jnp.float32)],
        compiler_params=_params("parallel", "parallel", "parallel"), name="stick_breaking_attention",
    )(qkv, qkv, qkv, jnp.asarray(tri, jnp.bfloat16))


def _da_kernel(lq1_ref, lk1_ref, lq2_ref, lk2_ref, g_ref, q1_ref, q2_ref, k1_ref, k2_ref, v_ref, o_ref,
               m_ref, l_ref, acc_ref, *, blk, hd, lam_init):
    qi = pl.program_id(2)
    row = lax.broadcasted_iota(jnp.int32, (blk, blk), 0)
    col = lax.broadcasted_iota(jnp.int32, (blk, blk), 1)
    causal_diag = col <= row
    qs = (q1_ref[...], q2_ref[...])
    ks = (k1_ref, k2_ref)

    m_ref[...] = jnp.full_like(m_ref, NEG_INF)
    l_ref[...] = jnp.zeros_like(l_ref)
    acc_ref[...] = jnp.zeros_like(acc_ref)

    def visit(j, mask):
        start = pl.multiple_of(j * blk, blk)
        v = v_ref[pl.ds(start, blk), :]
        probs, corrs = [], []
        for s in range(2):
            k = ks[s][pl.ds(start, blk), :]
            sc = lax.dot_general(qs[s], k, (((1,), (1,)), ((), ())), preferred_element_type=jnp.float32)
            if mask is not None:
                sc = jnp.where(mask, sc, NEG_INF)
            m_old = m_ref[s]
            m_new = jnp.maximum(m_old, jnp.max(sc, axis=-1, keepdims=True))
            p = jnp.exp2(sc - jnp.concatenate([m_new] * (blk // hd), axis=1))
            corr = jnp.exp2(m_old - m_new)
            l_ref[s] = corr * l_ref[s] + jnp.sum(p, axis=-1, keepdims=True)
            m_ref[s] = m_new
            probs.append(p.astype(v.dtype))
            corrs.append(jnp.concatenate([corr, corr], axis=1))
        pv = jnp.dot(jnp.concatenate(probs, axis=0), v, preferred_element_type=jnp.float32)
        for s in range(2):
            acc_ref[s] = corrs[s] * acc_ref[s] + pv[s * blk:(s + 1) * blk]

    def body(j, carry):
        visit(j, None)
        return carry

    lax.fori_loop(0, qi, body, 0)
    visit(qi, causal_diag)

    lam = (jnp.exp(jnp.sum(lq1_ref[...] * lk1_ref[...], axis=-1, keepdims=True))
           - jnp.exp(jnp.sum(lq2_ref[...] * lk2_ref[...], axis=-1, keepdims=True)) + lam_init)
    l1 = jnp.concatenate([l_ref[0], l_ref[0]], axis=1)
    l2 = jnp.concatenate([l_ref[1], l_ref[1]], axis=1)
    o = acc_ref[0] / l1 - lam * (acc_ref[1] / l2)
    o = o * lax.rsqrt(jnp.mean(o * o, axis=-1, keepdims=True) + LN_EPS) * g_ref[...] * (1.0 - lam_init)
    o_ref[...] = o.astype(o_ref.dtype)


def _da_attention(qkv, lambdas, subln_g, batch, seq, heads, hd, col0, lam_init):
    blk = _blk(seq, 512)
    nq = seq // blk
    qb, kb, vb = col0 // hd, col0 // hd + 2 * heads, col0 // (2 * hd) + 2 * heads
    vec = pl.BlockSpec((1, hd), lambda b, h, i: (0, 0))
    kern = functools.partial(_da_kernel, blk=blk, hd=hd, lam_init=lam_init)
    return pl.pallas_call(
        kern, out_shape=jax.ShapeDtypeStruct((batch * seq, heads * 2 * hd), jnp.bfloat16), grid=(batch, heads, nq),
        in_specs=[vec, vec, vec, vec, pl.BlockSpec((1, 2 * hd), lambda b, h, i: (0, 0)),
                  pl.BlockSpec((blk, hd), lambda b, h, i: (b * nq + i, qb + 2 * h)),
                  pl.BlockSpec((blk, hd), lambda b, h, i: (b * nq + i, qb + 2 * h + 1)),
                  pl.BlockSpec((seq, hd), lambda b, h, i: (b, kb + 2 * h)),
                  pl.BlockSpec((seq, hd), lambda b, h, i: (b, kb + 2 * h + 1)),
                  pl.BlockSpec((seq, 2 * hd), lambda b, h, i: (b, vb + h))],
        out_specs=pl.BlockSpec((blk, 2 * hd), lambda b, h, i: (b * nq + i, h)),
        scratch_shapes=[pltpu.VMEM((2, blk, hd), jnp.float32), pltpu.VMEM((2, blk, hd), jnp.float32),
                        pltpu.VMEM((2, blk, 2 * hd), jnp.float32)],
        compiler_params=_params("parallel", "parallel", "parallel"), name="differential_attention",
    )(*[l.reshape(1, hd) for l in lambdas], subln_g.reshape(1, 2 * hd), qkv, qkv, qkv, qkv, qkv)


def _extract_top(x, n):
    rows = lax.broadcasted_iota(jnp.int32, x.shape, 0).astype(jnp.float32)
    tops = []
    for _ in range(n):
        m = jnp.max(x, axis=0, keepdims=True)
        tops.append(m)
        first = jnp.min(jnp.where(x == m, rows, float(x.shape[0])), axis=0, keepdims=True)
        x = jnp.where(rows == first, NEG_INF, x)
    return tops


def _sorting_network(n):
    pairs, p = [], 1
    while p < n:
        k = p
        while k >= 1:
            for j in range(k % p, n - k, 2 * k):
                for i in range(min(k, n - j - k)):
                    if (i + j) // (2 * p) == (i + j + k) // (2 * p):
                        pairs.append((i + j, i + j + k))
            k //= 2
        p *= 2
    return pairs


def _extract_top_sorted(x, n):
    groups = [x[g * SUBLANES:(g + 1) * SUBLANES] for g in range(x.shape[0] // SUBLANES)]
    for a, b in _sorting_network(len(groups)):
        groups[a], groups[b] = jnp.maximum(groups[a], groups[b]), jnp.minimum(groups[a], groups[b])
    lane_list = lax.broadcasted_iota(jnp.int32, groups[0].shape, 0).astype(jnp.float32)
    tops = []
    for t in range(n):
        head = groups[0]
        m = jnp.max(head, axis=0, keepdims=True)
        tops.append(m)
        if t == n - 1:
            break
        first = jnp.min(jnp.where(head == m, lane_list, float(SUBLANES)), axis=0, keepdims=True)
        popped = lane_list == first
        live = min(len(groups), n - t)
        for g in range(live):
            below = groups[g + 1] if g + 1 < len(groups) else NEG_INF
            groups[g] = jnp.where(popped, below, groups[g])
    return tops


def _pair_candidates(top_a, top_b, combine):
    k = len(top_a)
    sub = SUBLANES
    b_all = jnp.concatenate(top_b, axis=0)
    pieces, masks, singles = [], [], []
    for p in range(k):
        n_q = k // (p + 1)
        if n_q == 1:
            singles.append(top_a[p])
            continue
        rows = -(-n_q // sub) * sub
        pieces.append(combine(top_a[p], b_all[0:rows]))
        masks.append(lax.broadcasted_iota(jnp.int32, (rows, 1), 0) < n_q)
    if singles:
        pieces.append(combine(jnp.concatenate(singles, axis=0), top_b[0]))
        masks.append(None)
    return pieces, masks


def _peer_topk_kernel(q_ref, keys_ref, s_ref, e_ref, tau_ref, *, n_keys, topk):
    half = q_ref.shape[1] // 2
    scores, tops = [], []
    for c in range(2):
        s = lax.dot_general(keys_ref[0, c], q_ref[:, c * half:(c + 1) * half], (((1,), (1,)), ((), ())),
                            preferred_element_type=jnp.float32)
        scores.append(s)
        n_groups = n_keys // SUBLANES
        pow2 = n_keys % SUBLANES == 0 and n_groups & (n_groups - 1) == 0
        tops.append(_extract_top_sorted(s, topk) if pow2 else _extract_top(s, topk))
    exps = [[jnp.exp(t - top[0]) for t in top] for top in tops]
    sums, masks = _pair_candidates(tops[0], tops[1], jnp.add)
    prods, _ = _pair_candidates(exps[0], exps[1], jnp.multiply)
    cand = jnp.concatenate([s if m is None else jnp.where(m, s, NEG_INF) for s, m in zip(sums, masks)], axis=0)
    cand_e = jnp.concatenate(prods, axis=0)
    groups = -(-cand.shape[0] // SUBLANES)
    pad_rows = SUBLANES * (1 << (groups - 1).bit_length()) - cand.shape[0]
    padded = jnp.concatenate([cand, jnp.full((pad_rows, cand.shape[1]), NEG_INF, cand.dtype)], axis=0)
    tau = _extract_top_sorted(padded, topk)[-1]
    z = jnp.sum(jnp.where(cand >= tau, cand_e, 0.0), axis=0, keepdims=True)
    s_ref[...] = jnp.concatenate(scores, axis=0)
    e_ref[...] = jnp.concatenate([jnp.exp(scores[0] - tops[0][0]) / z, jnp.exp(scores[1] - tops[1][0])], axis=0)
    tau_ref[0] = tau


def _peer_topk(hq, keys):
    t = hq.shape[0]
    heads, _, n_keys, half = keys.shape
    tn = _blk(t, 512)
    kern = functools.partial(_peer_topk_kernel, n_keys=n_keys, topk=PEER_TOPK)
    st = jax.ShapeDtypeStruct((heads * 2 * n_keys, t), jnp.float32)
    blk = pl.BlockSpec((2 * n_keys, tn), lambda i, h: (h, i))
    return pl.pallas_call(
        kern, out_shape=(st, st, jax.ShapeDtypeStruct((heads, 1, t), jnp.float32)), grid=(t // tn, heads),
        in_specs=[pl.BlockSpec((tn, 2 * half), lambda i, h: (i, h)),
                  pl.BlockSpec((1, 2, n_keys, half), lambda i, h: (h, 0, 0, 0))],
        out_specs=(blk, blk, pl.BlockSpec((1, 1, tn), lambda i, h: (h, 0, i))),
        compiler_params=_params("parallel", "parallel"), name="peer_topk",
    )(hq, keys)


def _peer_dense_kernel(h_ref, u_ref, v_ref, s_ref, e_ref, tau_ref, o_ref, ga_ref, *, heads, n_keys, n_blocks):
    j = pl.program_id(1)

    @pl.when(j == 0)
    def _():
        ga_ref[1] = jnp.zeros(ga_ref.shape[1:], ga_ref.dtype)
        o_ref[...] = jnp.zeros_like(o_ref)

    for parity in range(2):
        pl.when(j % 2 == parity)(functools.partial(
            _peer_dense_step, h_ref, u_ref, v_ref, s_ref, e_ref, tau_ref, o_ref, ga_ref,
            jnp.minimum(j, n_blocks - 1), parity, heads, n_keys))


def _peer_dense_step(h_ref, u_ref, v_ref, s_ref, e_ref, tau_ref, o_ref, ga_ref, jb, slot_now, heads, n_keys):
    tm, te = ga_ref.shape[1:]
    d = o_ref.shape[1]
    n_i = te // n_keys
    mxu = MXU_TILE
    n_tok_tiles = tm // mxu
    rt = min(te, PEER_KEY_ROWS)

    def link_from(x):
        bits = pltpu.bitcast(x[0:8, 0:LANES], jnp.uint32)
        zero = lax.shift_right_logical(lax.shift_right_logical(bits, jnp.uint32(16)), jnp.uint32(16))
        return pltpu.bitcast(zero, jnp.float32)

    pre_tiles = {}

    k_parts = PEER_KEY_SPLIT
    kw = d // k_parts
    done_tiles = set()

    def pre_tile(p, kk):
        mi, ni = divmod(p, n_tok_tiles)
        part = jnp.dot(u_ref[mi * rt:(mi + 1) * rt, kk * kw:(kk + 1) * kw],
                       h_ref[kk * kw:(kk + 1) * kw, ni * mxu:(ni + 1) * mxu], preferred_element_type=jnp.float32)
        pre_tiles[mi, ni] = part if kk == 0 else pre_tiles[mi, ni] + part
        if kk == k_parts - 1:
            done_tiles.add((mi, ni))
        return link_from(part)

    def value_chunk(c):
        cols = slice(c * mxu, (c + 1) * mxu)
        acc = o_ref[:, cols] + jnp.dot(ga_ref[1 - slot_now], v_ref[:, cols], preferred_element_type=jnp.float32)
        o_ref[:, cols] = acc
        return link_from(acc)

    n_pre = (te // rt) * n_tok_tiles
    chunks = [functools.partial(pre_tile, p, kk) for p in range(n_pre) for kk in range(k_parts)]
    chunks += [functools.partial(value_chunk, c) for c in range(d // mxu)]
    results = []

    def finalize(il):
        mi, off = divmod(il * n_keys, rt)
        while any((mi, ni) not in done_tiles for ni in range(n_tok_tiles)):
            results.append(chunks.pop(0)())
        p = jnp.concatenate([pre_tiles[mi, ni][off:off + n_keys, :] for ni in range(n_tok_tiles)], axis=1)
        act = 0.5 * p * (1.0 + lax.erf(p * (2.0 ** -0.5)))
        ga_ref[slot_now, :, il * n_keys:(il + 1) * n_keys] = (gates.pop(il) * act).astype(ga_ref.dtype).T

    pieces = [(il, h) for il in range(n_i) for h in range(heads)]
    costs = [rt * kw * mxu] * (n_pre * k_parts) + [tm * te * mxu] * (d // mxu)
    ends = [sum(costs[:c + 1]) for c in range(len(costs))]
    slot = PEER_GATE_SPAN * ends[-1] / len(pieces)
    linked = 0
    gates = {}
    link = link_from(s_ref[0:8, 0:LANES])
    for k, (il, h) in enumerate(pieces):
        while len(results) < len(costs) and ends[len(results)] - costs[len(results)] < (k + 1) * slot:
            results.append(chunks.pop(0)())
        while linked < len(results) and ends[linked] <= k * slot:
            link, linked = link + results[linked], linked + 1
        for _ in range(PEER_LINK_ROLLS):
            link = pltpu.roll(link, 1, 1)
        base = h * 2 * n_keys
        a_row = s_ref[pl.ds(base + jb * n_i + il, 1), :] + jnp.concatenate([link[0:1, :]] * (tm // LANES), axis=1)
        ea_row = e_ref[pl.ds(base + jb * n_i + il, 1), :]
        b_tile = s_ref[base + n_keys:base + 2 * n_keys, :]
        eb_tile = e_ref[base + n_keys:base + 2 * n_keys, :]
        term = jnp.where(a_row + b_tile >= tau_ref[h], ea_row * eb_tile, 0.0)
        gates[il] = term if h == 0 else gates[il] + term
        if h == heads - 1:
            finalize(il)
    while chunks:
        chunks.pop(0)()


def _peer_dense(h_bf_t, u_bf, v_bf, scores_t, exps_t, tau, heads, n_keys):
    d, t = h_bf_t.shape
    e = u_bf.shape[0]
    tm, te = _blk(t, PEER_TOKEN_BLOCK), _blk(e, PEER_EXPERT_BLOCK)
    nb = e // te
    kern = functools.partial(_peer_dense_kernel, heads=heads, n_keys=n_keys, n_blocks=nb)
    rows = scores_t.shape[0]
    once = pl.Buffered(1)
    return pl.pallas_call(
        kern, out_shape=jax.ShapeDtypeStruct((t, d), jnp.float32), grid=(t // tm, nb + 1),
        in_specs=[pl.BlockSpec((d, tm), lambda i, j: (0, i), pipeline_mode=once),
                  pl.BlockSpec((te, d), lambda i, j: (jnp.minimum(j, nb - 1), 0)),
                  pl.BlockSpec((te, d), lambda i, j: (jnp.maximum(j - 1, 0), 0)),
                  pl.BlockSpec((rows, tm), lambda i, j: (0, i), pipeline_mode=once),
                  pl.BlockSpec((rows, tm), lambda i, j: (0, i), pipeline_mode=once),
                  pl.BlockSpec((heads, 1, tm), lambda i, j: (0, 0, i))],
        out_specs=pl.BlockSpec((tm, d), lambda i, j: (i, 0), pipeline_mode=once),
        scratch_shapes=[pltpu.VMEM((2, tm, te), jnp.bfloat16)],
        compiler_params=pltpu.CompilerParams(dimension_semantics=("parallel", "arbitrary"),
                                             vmem_limit_bytes=PEER_VMEM_LIMIT_BYTES), name="peer_dense",
    )(h_bf_t, u_bf, v_bf, scores_t, exps_t, tau)


def kernel(x, positions, w_in, b_gate, lambda_q1, lambda_k1, lambda_q2, lambda_k2, subln_g, w_sb_branch,
           w_da_branch, w_out, ln1_g, ln1_b, peer_w_q, peer_sub_keys, peer_u, peer_v, ln2_g, ln2_b):
    batch, seq, d = x.shape
    depth = w_in.shape[0]
    hd = lambda_q1.shape[-1]
    sb_w, da_w = w_sb_branch.shape[1], w_da_branch.shape[1]
    sb_heads, da_heads = sb_w // hd, da_w // (2 * hd)
    n_qkv = 3 * sb_w + 3 * da_w
    alpha = (2 * depth) ** 0.25
    bf = jnp.bfloat16
    t = batch * seq

    cos, sin = _rope_tables(positions, hd)
    col_scale = np.ones((1, n_qkv), np.float32)
    col_scale[:, :sb_w] = hd ** -0.5 * math.log2(math.e)
    col_scale[:, 3 * sb_w:3 * sb_w + da_w] = hd ** -0.5 * math.log2(math.e)
    col_scale = jnp.asarray(col_scale)
    h = x.reshape(t, d)
    h_bf = h.astype(bf)
    for l in range(depth):
        lam_init = 0.8 - 0.6 * math.exp(-0.3 * l)
        w_in_bf = w_in[l].astype(bf)
        qkv = _qkv_proj(h_bf, w_in_bf, col_scale, cos, sin, n_qkv, (3 * sb_w, 3 * sb_w + 2 * da_w), hd)
        gate = _gate_proj(h_bf, w_in_bf, b_gate[l].reshape(1, -1), n_qkv, 2 * d)
        o_sb = _sb_attention(qkv, batch, seq, sb_heads, hd)
        o_da = _da_attention(qkv, (lambda_q1[l], lambda_k1[l], lambda_q2[l], lambda_k2[l]), subln_g[l],
                             batch, seq, da_heads, hd, 3 * sb_w, lam_init)
        merged = _merge(o_sb, o_da, w_sb_branch[l].astype(bf), w_da_branch[l].astype(bf), gate)
        mix = _matmul(merged, w_out[l].astype(bf), jnp.float32, "out_proj")
        h, h_bf, h_bf_t = _residual_ln(h, mix, ln1_g[l], ln1_b[l], alpha, True)

        heads, _, n_keys, half = peer_sub_keys[l].shape
        hq = _matmul(h_bf, peer_w_q[l].astype(bf), bf, "peer_query")
        scores_t, exps_t, tau = _peer_topk(hq, peer_sub_keys[l].astype(bf))
        ffn = _peer_dense(h_bf_t, peer_u[l].astype(bf), peer_v[l].astype(bf), scores_t, exps_t, tau, heads, n_keys)
        outs = _residual_ln(h, ffn, ln2_g[l], ln2_b[l], alpha, l + 1 < depth)
        h = outs[0]
        if l + 1 < depth:
            h_bf = outs[1]
    return h.reshape(batch, seq, d)
```

```python
import functools
import math

import numpy as np
import jax
import jax.numpy as jnp
from jax import lax
from jax.experimental import pallas as pl
from jax.experimental.pallas import tpu as pltpu

LN_EPS = 1e-5
ROPE_THETA = 10000.0
PEER_TOPK = 16
V7X_VMEM_LIMIT_BYTES = 56 * 1024 * 1024
LANES = 128
SUBLANES = 8
MXU_TILE = 256
PEER_TOKEN_BLOCK = 512
PEER_EXPERT_BLOCK = 512
PEER_KEY_ROWS = 256
PEER_KEY_SPLIT = 4
PEER_LINK_ROLLS = 1
NEG_INF = float("-inf")


def _params(*sem):
    return pltpu.CompilerParams(dimension_semantics=sem, vmem_limit_bytes=V7X_VMEM_LIMIT_BYTES)


def _blk(n, pref):
    b = min(n, pref)
    while n % b:
        b //= 2
    return b


def _cast_kernel(x_ref, o_ref):
    o_ref[...] = x_ref[...].astype(o_ref.dtype)


def _to_bf16(a):
    m, n = a.shape
    rows = max(2 * SUBLANES, (8 * 1024 * 1024) // (4 * n))
    bm = _blk(m, 1 << (rows.bit_length() - 1))
    spec = pl.BlockSpec((bm, n), lambda i: (i, 0))
    return pl.pallas_call(
        _cast_kernel, out_shape=jax.ShapeDtypeStruct((m, n), jnp.bfloat16), grid=(m // bm,),
        in_specs=[spec], out_specs=spec, compiler_params=_params("parallel"), name="to_bf16",
    )(a)


def _rope_kernel(pos_ref, freq_ref, cos_ref, sin_ref):
    ang = pos_ref[...].astype(jnp.float32) * freq_ref[...]
    hd = freq_ref.shape[1]
    lane = lax.broadcasted_iota(jnp.int32, ang.shape, 1)
    cos_ref[...] = jnp.cos(ang)
    s = jnp.sin(ang)
    sin_ref[...] = jnp.where(lane < hd // 2, -s, s)


def _rope_tables(positions, hd):
    t = positions.size
    bm = _blk(t, 2048)
    inv = ROPE_THETA ** (-np.arange(0, hd, 2, dtype=np.float32) / np.float32(hd))
    freq = jnp.asarray(np.concatenate([inv, inv]).astype(np.float32).reshape(1, hd))
    out = jax.ShapeDtypeStruct((t, hd), jnp.float32)
    return pl.pallas_call(
        _rope_kernel, out_shape=(out, out), grid=(t // bm,),
        in_specs=[pl.BlockSpec((bm, 1), lambda i: (i, 0)), pl.BlockSpec((1, hd), lambda i: (0, 0))],
        out_specs=(pl.BlockSpec((bm, hd), lambda i: (i, 0)), pl.BlockSpec((bm, hd), lambda i: (i, 0))),
        compiler_params=_params("parallel"), name="rope_tables",
    )(positions.reshape(t, 1), freq)


def _qkv_kernel(x_ref, w_ref, cs_ref, cos_ref, sin_ref, o_ref, *, rope_lo, rope_hi, hd):
    acc = jnp.dot(x_ref[...], w_ref[...], preferred_element_type=jnp.float32) * cs_ref[...]
    j = pl.program_id(1)
    is_rope = jnp.logical_and(j >= rope_lo, j < rope_hi)

    @pl.when(is_rope)
    def _():
        c = cos_ref[...]
        s = sin_ref[...]
        for t in range(acc.shape[1] // hd):
            a = acc[:, t * hd:(t + 1) * hd]
            o_ref[:, t * hd:(t + 1) * hd] = (a * c + pltpu.roll(a, hd // 2, 1) * s).astype(o_ref.dtype)

    @pl.when(jnp.logical_not(is_rope))
    def _():
        o_ref[...] = acc.astype(o_ref.dtype)


def _qkv_proj(x, w, col_scale, cos, sin, n_out, rope_cols, hd):
    m, k = x.shape
    bm, bn = _blk(m, 1024), _blk(math.gcd(n_out, rope_cols[0], rope_cols[1]), 1024)
    kern = functools.partial(_qkv_kernel, rope_lo=rope_cols[0] // bn, rope_hi=rope_cols[1] // bn, hd=hd)
    return pl.pallas_call(
        kern, out_shape=jax.ShapeDtypeStruct((m, n_out), jnp.bfloat16), grid=(m // bm, n_out // bn),
        in_specs=[pl.BlockSpec((bm, k), lambda i, j: (i, 0)), pl.BlockSpec((k, bn), lambda i, j: (0, j)),
                  pl.BlockSpec((1, bn), lambda i, j: (0, j)),
                  pl.BlockSpec((bm, hd), lambda i, j: (i, 0)), pl.BlockSpec((bm, hd), lambda i, j: (i, 0))],
        out_specs=pl.BlockSpec((bm, bn), lambda i, j: (i, j)),
        compiler_params=_params("parallel", "parallel"), name="qkv_proj",
    )(x, w, col_scale, cos, sin)


def _gate_kernel(x_ref, w_ref, b_ref, o_ref):
    acc = jnp.dot(x_ref[...], w_ref[...], preferred_element_type=jnp.float32)
    o_ref[...] = jax.nn.sigmoid(acc + b_ref[...]).astype(o_ref.dtype)


def _gate_proj(x, w, bias, col0, n_out):
    m, k = x.shape
    bm, bn = _blk(m, 1024), _blk(math.gcd(n_out, col0), 1024)
    off = col0 // bn
    return pl.pallas_call(
        _gate_kernel, out_shape=jax.ShapeDtypeStruct((m, n_out), jnp.float32), grid=(m // bm, n_out // bn),
        in_specs=[pl.BlockSpec((bm, k), lambda i, j: (i, 0)), pl.BlockSpec((k, bn), lambda i, j: (0, j + off)),
                  pl.BlockSpec((1, bn), lambda i, j: (0, j))],
        out_specs=pl.BlockSpec((bm, bn), lambda i, j: (i, j)),
        compiler_params=_params("parallel", "parallel"), name="gate_proj",
    )(x, w, bias)


def _mm_kernel(a_ref, b_ref, o_ref):
    o_ref[...] = jnp.dot(a_ref[...], b_ref[...], preferred_element_type=jnp.float32).astype(o_ref.dtype)


def _matmul(a, b, out_dtype, name):
    m, k = a.shape
    n = b.shape[1]
    bm, bn = _blk(m, 1024), _blk(n, 1024)
    return pl.pallas_call(
        _mm_kernel, out_shape=jax.ShapeDtypeStruct((m, n), out_dtype), grid=(m // bm, n // bn),
        in_specs=[pl.BlockSpec((bm, k), lambda i, j: (i, 0)), pl.BlockSpec((k, bn), lambda i, j: (0, j))],
        out_specs=pl.BlockSpec((bm, bn), lambda i, j: (i, j)),
        compiler_params=_params("parallel", "parallel"), name=name,
    )(a, b)


def _merge_kernel(osb_ref, oda_ref, wsb_ref, wda_ref, gsb_ref, gda_ref, o_ref):
    ysb = jnp.dot(osb_ref[...], wsb_ref[...], preferred_element_type=jnp.float32)
    yda = jnp.dot(oda_ref[...], wda_ref[...], preferred_element_type=jnp.float32)
    o_ref[...] = (gsb_ref[...] * ysb + gda_ref[...] * yda).astype(o_ref.dtype)


def _merge(o_sb, o_da, w_sb, w_da, gate):
    m = o_sb.shape[0]
    d = w_sb.shape[1]
    bm, bn = _blk(m, 1024), _blk(d, 512)
    goff = d // bn
    return pl.pallas_call(
        _merge_kernel, out_shape=jax.ShapeDtypeStruct((m, d), jnp.bfloat16), grid=(m // bm, d // bn),
        in_specs=[pl.BlockSpec((bm, o_sb.shape[1]), lambda i, j: (i, 0)),
                  pl.BlockSpec((bm, o_da.shape[1]), lambda i, j: (i, 0)),
                  pl.BlockSpec((w_sb.shape[0], bn), lambda i, j: (0, j)),
                  pl.BlockSpec((w_da.shape[0], bn), lambda i, j: (0, j)),
                  pl.BlockSpec((bm, bn), lambda i, j: (i, j)),
                  pl.BlockSpec((bm, bn), lambda i, j: (i, j + goff))],
        out_specs=pl.BlockSpec((bm, bn), lambda i, j: (i, j)),
        compiler_params=_params("parallel", "parallel"), name="branch_merge",
    )(o_sb, o_da, w_sb, w_da, gate, gate)


def _ln_kernel(res_ref, y_ref, g_ref, b_ref, o_ref, *low_refs, alpha):
    z = alpha * res_ref[...] + y_ref[...]
    mu = jnp.mean(z, axis=-1, keepdims=True)
    zc = z - mu
    var = jnp.mean(zc * zc, axis=-1, keepdims=True)
    out = zc * lax.rsqrt(var + LN_EPS) * g_ref[...] + b_ref[...]
    o_ref[...] = out
    if low_refs:
        low = out.astype(low_refs[0].dtype)
        low_refs[0][...] = low
        low_refs[1][...] = low.T


def _residual_ln(res, y, g, b, alpha, with_bf16):
    m, d = res.shape
    bm = _blk(m, 256)
    row = pl.BlockSpec((bm, d), lambda i: (i, 0))
    vec = pl.BlockSpec((1, d), lambda i: (0, 0))
    out_shape = [jax.ShapeDtypeStruct((m, d), jnp.float32)]
    out_specs = [row]
    if with_bf16:
        out_shape += [jax.ShapeDtypeStruct((m, d), jnp.bfloat16), jax.ShapeDtypeStruct((d, m), jnp.bfloat16)]
        out_specs += [row, pl.BlockSpec((d, bm), lambda i: (0, i))]
    return pl.pallas_call(
        functools.partial(_ln_kernel, alpha=alpha), out_shape=tuple(out_shape),
        grid=(m // bm,), in_specs=[row, row, vec, vec], out_specs=tuple(out_specs),
        compiler_params=_params("parallel"), name="residual_layernorm",
    )(res, y, g.reshape(1, d), b.reshape(1, d))


SB_SKIP_LOG2 = -100.0 * math.log2(math.e)


def _sb_kernel(q_ref, k_ref, v_ref, tri_ref, o_ref, c_ref, acc_ref, *, blk, hd, hps):
    qi = pl.program_id(2)
    row = lax.broadcasted_iota(jnp.int32, (blk, blk), 0)
    col = lax.broadcasted_iota(jnp.int32, (blk, blk), 1)
    valid_diag = col < row
    tri = tri_ref[...]

    def visit(j, valid):
        start = pl.multiple_of(j * blk, blk)
        zs, log_fails = [], []
        for g in range(hps):
            q = q_ref[:, g * hd:(g + 1) * hd]
            k = k_ref[pl.ds(start, blk), g * hd:(g + 1) * hd]
            z = lax.dot_general(q, k, (((1,), (1,)), ((), ())), preferred_element_type=jnp.float32)
            log_fail = -(jnp.maximum(z, 0.0) + jnp.log2(1.0 + jnp.exp2(-jnp.abs(z))))
            if valid is not None:
                log_fail = jnp.where(valid, log_fail, 0.0)
            zs.append(z)
            log_fails.append(log_fail)
        sums_all = jnp.dot(jnp.concatenate([lf.astype(jnp.bfloat16) for lf in log_fails], axis=0), tri,
                           preferred_element_type=jnp.float32)
        for g in range(hps):
            v = v_ref[pl.ds(start, blk), g * hd:(g + 1) * hd]
            sums = sums_all[g * blk:(g + 1) * blk]
            c = c_ref[g]
            w = jnp.exp2(zs[g] + log_fails[g] + sums[:, :blk] + c)
            if valid is not None:
                w = jnp.where(valid, w, 0.0)
            acc_ref[g] += jnp.dot(w.astype(v.dtype), v, preferred_element_type=jnp.float32)
            c_ref[g] = c + sums[:, blk:]

    def bound():
        return jnp.max(c_ref[:, :, :hd])

    c_ref[...] = jnp.zeros_like(c_ref)
    acc_ref[...] = jnp.zeros_like(acc_ref)
    visit(qi, valid_diag)

    def cond(state):
        j, worst = state
        return jnp.logical_and(j >= 0, worst > SB_SKIP_LOG2)

    def body(state):
        j, _ = state
        visit(j, None)
        return j - 1, bound()

    lax.while_loop(cond, body, (qi - 1, bound()))
    for g in range(hps):
        o_ref[:, g * hd:(g + 1) * hd] = acc_ref[g].astype(o_ref.dtype)


def _sb_attention(qkv, batch, seq, heads, hd):
    blk = _blk(seq, 256)
    nq = seq // blk
    hps = max(h for h in (4, 2, 1) if heads % h == 0)
    hg, w = heads // hps, hps * hd
    tri = np.concatenate([np.tril(np.ones((blk, blk), np.float32), -1), np.ones((blk, blk), np.float32)], axis=1)
    kern = functools.partial(_sb_kernel, blk=blk, hd=hd, hps=hps)
    return pl.pallas_call(
        kern, out_shape=jax.ShapeDtypeStruct((batch * seq, heads * hd), jnp.bfloat16), grid=(batch, hg, nq),
        in_specs=[pl.BlockSpec((blk, w), lambda b, h, i: (b * nq + i, h)),
                  pl.BlockSpec((seq, w), lambda b, h, i: (b, hg + h)),
                  pl.BlockSpec((seq, w), lambda b, h, i: (b, 2 * hg + h)),
                  pl.BlockSpec((blk, 2 * blk), lambda b, h, i: (0, 0))],
        out_specs=pl.BlockSpec((blk, w), lambda b, h, i: (b * nq + i, h)),
        scratch_shapes=[pltpu.VMEM((hps, blk, blk), jnp.float32), pltpu.VMEM((hps, blk, hd), jnp.float32)],
        compiler_params=_params("parallel", "parallel", "parallel"), name="stick_breaking_attention",
    )(qkv, qkv, qkv, jnp.asarray(tri, jnp.bfloat16))


def _da_kernel(lq1_ref, lk1_ref, lq2_ref, lk2_ref, g_ref, q1_ref, q2_ref, k1_ref, k2_ref, v_ref, o_ref,
               m_ref, l_ref, acc_ref, *, blk, hd, lam_init):
    qi = pl.program_id(2)
    row = lax.broadcasted_iota(jnp.int32, (blk, blk), 0)
    col = lax.broadcasted_iota(jnp.int32, (blk, blk), 1)
    causal_diag = col <= row
    qs = (q1_ref[...], q2_ref[...])
    ks = (k1_ref, k2_ref)

    m_ref[...] = jnp.full_like(m_ref, NEG_INF)
    l_ref[...] = jnp.zeros_like(l_ref)
    acc_ref[...] = jnp.zeros_like(acc_ref)

    def visit(j, mask):
        start = pl.multiple_of(j * blk, blk)
        v = v_ref[pl.ds(start, blk), :]
        probs, corrs = [], []
        for s in range(2):
            k = ks[s][pl.ds(start, blk), :]
            sc = lax.dot_general(qs[s], k, (((1,), (1,)), ((), ())), preferred_element_type=jnp.float32)
            if mask is not None:
                sc = jnp.where(mask, sc, NEG_INF)
            m_old = m_ref[s]
            m_new = jnp.maximum(m_old, jnp.max(sc, axis=-1, keepdims=True))
            p = jnp.exp2(sc - jnp.concatenate([m_new] * (blk // hd), axis=1))
            corr = jnp.exp2(m_old - m_new)
            l_ref[s] = corr * l_ref[s] + jnp.sum(p, axis=-1, keepdims=True)
            m_ref[s] = m_new
            probs.append(p.astype(v.dtype))
            corrs.append(jnp.concatenate([corr, corr], axis=1))
        pv = jnp.dot(jnp.concatenate(probs, axis=0), v, preferred_element_type=jnp.float32)
        for s in range(2):
            acc_ref[s] = corrs[s] * acc_ref[s] + pv[s * blk:(s + 1) * blk]

    def body(j, carry):
        visit(j, None)
        return carry

    lax.fori_loop(0, qi, body, 0)
    visit(qi, causal_diag)

    lam = (jnp.exp(jnp.sum(lq1_ref[...] * lk1_ref[...], axis=-1, keepdims=True))
           - jnp.exp(jnp.sum(lq2_ref[...] * lk2_ref[...], axis=-1, keepdims=True)) + lam_init)
    l1 = jnp.concatenate([l_ref[0], l_ref[0]], axis=1)
    l2 = jnp.concatenate([l_ref[1], l_ref[1]], axis=1)
    o = acc_ref[0] / l1 - lam * (acc_ref[1] / l2)
    o = o * lax.rsqrt(jnp.mean(o * o, axis=-1, keepdims=True) + LN_EPS) * g_ref[...] * (1.0 - lam_init)
    o_ref[...] = o.astype(o_ref.dtype)


def _da_attention(qkv, lambdas, subln_g, batch, seq, heads, hd, col0, lam_init):
    blk = _blk(seq, 512)
    nq = seq // blk
    qb, kb, vb = col0 // hd, col0 // hd + 2 * heads, col0 // (2 * hd) + 2 * heads
    vec = pl.BlockSpec((1, hd), lambda b, h, i: (0, 0))
    kern = functools.partial(_da_kernel, blk=blk, hd=hd, lam_init=lam_init)
    return pl.pallas_call(
        kern, out_shape=jax.ShapeDtypeStruct((batch * seq, heads * 2 * hd), jnp.bfloat16), grid=(batch, heads, nq),
        in_specs=[vec, vec, vec, vec, pl.BlockSpec((1, 2 * hd), lambda b, h, i: (0, 0)),
                  pl.BlockSpec((blk, hd), lambda b, h, i: (b * nq + i, qb + 2 * h)),
                  pl.BlockSpec((blk, hd), lambda b, h, i: (b * nq + i, qb + 2 * h + 1)),
                  pl.BlockSpec((seq, hd), lambda b, h, i: (b, kb + 2 * h)),
                  pl.BlockSpec((seq, hd), lambda b, h, i: (b, kb + 2 * h + 1)),
                  pl.BlockSpec((seq, 2 * hd), lambda b, h, i: (b, vb + h))],
        out_specs=pl.BlockSpec((blk, 2 * hd), lambda b, h, i: (b * nq + i, h)),
        scratch_shapes=[pltpu.VMEM((2, blk, hd), jnp.float32), pltpu.VMEM((2, blk, hd), jnp.float32),
                        pltpu.VMEM((2, blk, 2 * hd), jnp.float32)],
        compiler_params=_params("parallel", "parallel", "parallel"), name="differential_attention",
    )(*[l.reshape(1, hd) for l in lambdas], subln_g.reshape(1, 2 * hd), qkv, qkv, qkv, qkv, qkv)


def _extract_top(x, n):
    rows = lax.broadcasted_iota(jnp.int32, x.shape, 0).astype(jnp.float32)
    tops = []
    for _ in range(n):
        m = jnp.max(x, axis=0, keepdims=True)
        tops.append(m)
        first = jnp.min(jnp.where(x == m, rows, float(x.shape[0])), axis=0, keepdims=True)
        x = jnp.where(rows == first, NEG_INF, x)
    return tops


def _sorting_network(n):
    pairs, p = [], 1
    while p < n:
        k = p
        while k >= 1:
            for j in range(k % p, n - k, 2 * k):
                for i in range(min(k, n - j - k)):
                    if (i + j) // (2 * p) == (i + j + k) // (2 * p):
                        pairs.append((i + j, i + j + k))
            k //= 2
        p *= 2
    return pairs


def _extract_top_sorted(x, n):
    groups = [x[g * SUBLANES:(g + 1) * SUBLANES] for g in range(x.shape[0] // SUBLANES)]
    for a, b in _sorting_network(len(groups)):
        groups[a], groups[b] = jnp.maximum(groups[a], groups[b]), jnp.minimum(groups[a], groups[b])
    lane_list = lax.broadcasted_iota(jnp.int32, groups[0].shape, 0).astype(jnp.float32)
    tops = []
    for t in range(n):
        head = groups[0]
        m = jnp.max(head, axis=0, keepdims=True)
        tops.append(m)
        if t == n - 1:
            break
        first = jnp.min(jnp.where(head == m, lane_list, float(SUBLANES)), axis=0, keepdims=True)
        popped = lane_list == first
        live = min(len(groups), n - t)
        for g in range(live):
            below = groups[g + 1] if g + 1 < len(groups) else NEG_INF
            groups[g] = jnp.where(popped, below, groups[g])
    return tops


def _pair_candidates(top_a, top_b, combine):
    k = len(top_a)
    sub = SUBLANES
    b_all = jnp.concatenate(top_b, axis=0)
    pieces, masks, singles = [], [], []
    for p in range(k):
        n_q = k // (p + 1)
        if n_q == 1:
            singles.append(top_a[p])
            continue
        rows = -(-n_q // sub) * sub
        pieces.append(combine(top_a[p], b_all[0:rows]))
        masks.append(lax.broadcasted_iota(jnp.int32, (rows, 1), 0) < n_q)
    if singles:
        pieces.append(combine(jnp.concatenate(singles, axis=0), top_b[0]))
        masks.append(None)
    return pieces, masks


def _peer_topk_kernel(q_ref, keys_ref, s_ref, e_ref, tau_ref, *, n_keys, topk):
    half = q_ref.shape[1] // 2
    scores, tops = [], []
    for c in range(2):
        s = lax.dot_general(keys_ref[0, c], q_ref[:, c * half:(c + 1) * half], (((1,), (1,)), ((), ())),
                            preferred_element_type=jnp.float32)
        scores.append(s)
        n_groups = n_keys // SUBLANES
        pow2 = n_keys % SUBLANES == 0 and n_groups & (n_groups - 1) == 0
        tops.append(_extract_top_sorted(s, topk) if pow2 else _extract_top(s, topk))
    exps = [[jnp.exp(t - top[0]) for t in top] for top in tops]
    sums, masks = _pair_candidates(tops[0], tops[1], jnp.add)
    prods, _ = _pair_candidates(exps[0], exps[1], jnp.multiply)
    cand = jnp.concatenate([s if m is None else jnp.where(m, s, NEG_INF) for s, m in zip(sums, masks)], axis=0)
    cand_e = jnp.concatenate(prods, axis=0)
    groups = -(-cand.shape[0] // SUBLANES)
    pad_rows = SUBLANES * (1 << (groups - 1).bit_length()) - cand.shape[0]
    padded = jnp.concatenate([cand, jnp.full((pad_rows, cand.shape[1]), NEG_INF, cand.dtype)], axis=0)
    tau = _extract_top_sorted(padded, topk)[-1]
    z = jnp.sum(jnp.where(cand >= tau, cand_e, 0.0), axis=0, keepdims=True)
    s_ref[...] = jnp.concatenate(scores, axis=0)
    e_ref[...] = jnp.concatenate([jnp.exp(scores[0] - tops[0][0]) / z, jnp.exp(scores[1] - tops[1][0])], axis=0)
    tau_ref[0] = tau


def _peer_topk(hq, keys):
    t = hq.shape[0]
    heads, _, n_keys, half = keys.shape
    tn = _blk(t, 512)
    kern = functools.partial(_peer_topk_kernel, n_keys=n_keys, topk=PEER_TOPK)
    st = jax.ShapeDtypeStruct((heads * 2 * n_keys, t), jnp.float32)
    blk = pl.BlockSpec((2 * n_keys, tn), lambda i, h: (h, i))
    return pl.pallas_call(
        kern, out_shape=(st, st, jax.ShapeDtypeStruct((heads, 1, t), jnp.float32)), grid=(t // tn, heads),
        in_specs=[pl.BlockSpec((tn, 2 * half), lambda i, h: (i, h)),
                  pl.BlockSpec((1, 2, n_keys, half), lambda i, h: (h, 0, 0, 0))],
        out_specs=(blk, blk, pl.BlockSpec((1, 1, tn), lambda i, h: (h, 0, i))),
        compiler_params=_params("parallel", "parallel"), name="peer_topk",
    )(hq, keys)


def _peer_dense_kernel(h_ref, u_ref, v_ref, s_ref, e_ref, tau_ref, o_ref, ga_ref, *, heads, n_keys, n_blocks):
    j = pl.program_id(1)

    @pl.when(j == 0)
    def _():
        ga_ref[1] = jnp.zeros(ga_ref.shape[1:], ga_ref.dtype)
        o_ref[...] = jnp.zeros_like(o_ref)

    for parity in range(2):
        pl.when(j % 2 == parity)(functools.partial(
            _peer_dense_step, h_ref, u_ref, v_ref, s_ref, e_ref, tau_ref, o_ref, ga_ref,
            jnp.minimum(j, n_blocks - 1), parity, heads, n_keys))


def _peer_dense_step(h_ref, u_ref, v_ref, s_ref, e_ref, tau_ref, o_ref, ga_ref, jb, slot_now, heads, n_keys):
    tm, te = ga_ref.shape[1:]
    d = o_ref.shape[1]
    n_i = te // n_keys
    mxu = MXU_TILE
    n_tok_tiles = tm // mxu
    rt = min(te, PEER_KEY_ROWS)

    def link_from(x):
        bits = pltpu.bitcast(x[0:8, 0:LANES], jnp.uint32)
        zero = lax.shift_right_logical(lax.shift_right_logical(bits, jnp.uint32(16)), jnp.uint32(16))
        return pltpu.bitcast(zero, jnp.float32)

    pre_tiles = {}

    k_parts = PEER_KEY_SPLIT
    kw = d // k_parts
    done_tiles = set()

    def pre_tile(p, kk):
        mi, ni = divmod(p, n_tok_tiles)
        part = jnp.dot(u_ref[mi * rt:(mi + 1) * rt, kk * kw:(kk + 1) * kw],
                       h_ref[kk * kw:(kk + 1) * kw, ni * mxu:(ni + 1) * mxu], preferred_element_type=jnp.float32)
        pre_tiles[mi, ni] = part if kk == 0 else pre_tiles[mi, ni] + part
        if kk == k_parts - 1:
            done_tiles.add((mi, ni))
        return link_from(part)

    def value_chunk(c):
        cols = slice(c * mxu, (c + 1) * mxu)
        acc = o_ref[:, cols] + jnp.dot(ga_ref[1 - slot_now], v_ref[:, cols], preferred_element_type=jnp.float32)
        o_ref[:, cols] = acc
        return link_from(acc)

    n_pre = (te // rt) * n_tok_tiles
    chunks = [functools.partial(pre_tile, p, kk) for p in range(n_pre) for kk in range(k_parts)]
    chunks += [functools.partial(value_chunk, c) for c in range(d // mxu)]
    results = []

    def finalize(il):
        mi, off = divmod(il * n_keys, rt)
        while any((mi, ni) not in done_tiles for ni in range(n_tok_tiles)):
            results.append(chunks.pop(0)())
        p = jnp.concatenate([pre_tiles[mi, ni][off:off + n_keys, :] for ni in range(n_tok_tiles)], axis=1)
        act = 0.5 * p * (1.0 + lax.erf(p * (2.0 ** -0.5)))
        ga_ref[slot_now, :, il * n_keys:(il + 1) * n_keys] = (gates.pop(il) * act).astype(ga_ref.dtype).T

    pieces = [(il, h) for il in range(n_i) for h in range(heads)]
    costs = [rt * kw * mxu] * (n_pre * k_parts) + [tm * te * mxu] * (d // mxu)
    ends = [sum(costs[:c + 1]) for c in range(len(costs))]
    slot = ends[-1] / len(pieces)
    linked = 0
    gates = {}
    link = link_from(s_ref[0:8, 0:LANES])
    for k, (il, h) in enumerate(pieces):
        while len(results) < len(costs) and ends[len(results)] - costs[len(results)] < (k + 1) * slot:
            results.append(chunks.pop(0)())
        while linked < len(results) and ends[linked] <= k * slot:
            link, linked = link + results[linked], linked + 1
        for _ in range(PEER_LINK_ROLLS):
            link = pltpu.roll(link, 1, 1)
        base = h * 2 * n_keys
        a_row = s_ref[pl.ds(base + jb * n_i + il, 1), :] + jnp.concatenate([link[0:1, :]] * (tm // LANES), axis=1)
        ea_row = e_ref[pl.ds(base + jb * n_i + il, 1), :]
        b_tile = s_ref[base + n_keys:base + 2 * n_keys, :]
        eb_tile = e_ref[base + n_keys:base + 2 * n_keys, :]
        term = jnp.where(a_row + b_tile >= tau_ref[h], ea_row * eb_tile, 0.0)
        gates[il] = term if h == 0 else gates[il] + term
        if h == heads - 1:
            finalize(il)
    while chunks:
        chunks.pop(0)()


def _peer_dense(h_bf_t, u_bf, v_bf, scores_t, exps_t, tau, heads, n_keys):
    d, t = h_bf_t.shape
    e = u_bf.shape[0]
    tm, te = _blk(t, PEER_TOKEN_BLOCK), _blk(e, PEER_EXPERT_BLOCK)
    nb = e // te
    kern = functools.partial(_peer_dense_kernel, heads=heads, n_keys=n_keys, n_blocks=nb)
    rows = scores_t.shape[0]
    once = pl.Buffered(1)
    return pl.pallas_call(
        kern, out_shape=jax.ShapeDtypeStruct((t, d), jnp.float32), grid=(t // tm, nb + 1),
        in_specs=[pl.BlockSpec((d, tm), lambda i, j: (0, i), pipeline_mode=once),
                  pl.BlockSpec((te, d), lambda i, j: (jnp.minimum(j, nb - 1), 0)),
                  pl.BlockSpec((te, d), lambda i, j: (jnp.maximum(j - 1, 0), 0)),
                  pl.BlockSpec((rows, tm), lambda i, j: (0, i), pipeline_mode=once),
                  pl.BlockSpec((rows, tm), lambda i, j: (0, i), pipeline_mode=once),
                  pl.BlockSpec((heads, 1, tm), lambda i, j: (0, 0, i))],
        out_specs=pl.BlockSpec((tm, d), lambda i, j: (i, 0)),
        scratch_shapes=[pltpu.VMEM((2, tm, te), jnp.bfloat16)],
        compiler_params=_params("parallel", "arbitrary"), name="peer_dense",
    )(h_bf_t, u_bf, v_bf, scores_t, exps_t, tau)


def kernel(x, positions, w_in, b_gate, lambda_q1, lambda_k1, lambda_q2, lambda_k2, subln_g, w_sb_branch,
           w_da_branch, w_out, ln1_g, ln1_b, peer_w_q, peer_sub_keys, peer_u, peer_v, ln2_g, ln2_b):
    batch, seq, d = x.shape
    depth = w_in.shape[0]
    hd = lambda_q1.shape[-1]
    sb_w, da_w = w_sb_branch.shape[1], w_da_branch.shape[1]
    sb_heads, da_heads = sb_w // hd, da_w // (2 * hd)
    n_qkv = 3 * sb_w + 3 * da_w
    alpha = (2 * depth) ** 0.25
    bf = jnp.bfloat16
    t = batch * seq

    cos, sin = _rope_tables(positions, hd)
    col_scale = np.ones((1, n_qkv), np.float32)
    col_scale[:, :sb_w] = hd ** -0.5 * math.log2(math.e)
    col_scale[:, 3 * sb_w:3 * sb_w + da_w] = hd ** -0.5 * math.log2(math.e)
    col_scale = jnp.asarray(col_scale)
    h = x.reshape(t, d)
    h_bf = _to_bf16(h)
    for l in range(depth):
        lam_init = 0.8 - 0.6 * math.exp(-0.3 * l)
        w_in_bf = _to_bf16(w_in[l])
        qkv = _qkv_proj(h_bf, w_in_bf, col_scale, cos, sin, n_qkv, (3 * sb_w, 3 * sb_w + 2 * da_w), hd)
        gate = _gate_proj(h_bf, w_in_bf, b_gate[l].reshape(1, -1), n_qkv, 2 * d)
        o_sb = _sb_attention(qkv, batch, seq, sb_heads, hd)
        o_da = _da_attention(qkv, (lambda_q1[l], lambda_k1[l], lambda_q2[l], lambda_k2[l]), subln_g[l],
                             batch, seq, da_heads, hd, 3 * sb_w, lam_init)
        merged = _merge(o_sb, o_da, _to_bf16(w_sb_branch[l]), _to_bf16(w_da_branch[l]), gate)
        mix = _matmul(merged, _to_bf16(w_out[l]), jnp.float32, "out_proj")
        h, h_bf, h_bf_t = _residual_ln(h, mix, ln1_g[l], ln1_b[l], alpha, True)

        heads, _, n_keys, half = peer_sub_keys[l].shape
        hq = _matmul(h_bf, _to_bf16(peer_w_q[l]), bf, "peer_query")
        scores_t, exps_t, tau = _peer_topk(hq, peer_sub_keys[l].astype(bf))
        ffn = _peer_dense(h_bf_t, _to_bf16(peer_u[l]), _to_bf16(peer_v[l]), scores_t, exps_t, tau, heads, n_keys)
        outs = _residual_ln(h, ffn, ln2_g[l], ln2_b[l], alpha, l + 1 < depth)
        h = outs[0]
        if l + 1 < depth:
            h_bf = outs[1]
    return h.reshape(batch, seq, d)
```

```python
import functools
import math

import numpy as np
import jax
import jax.numpy as jnp
from jax import lax
from jax.experimental import pallas as pl
from jax.experimental.pallas import tpu as pltpu

LN_EPS = 1e-5
ROPE_THETA = 10000.0
PEER_TOPK = 16
V7X_VMEM_LIMIT_BYTES = 56 * 1024 * 1024
LANES = 128
SUBLANES = 8
MXU_TILE = 256
PEER_TOKEN_BLOCK = 512
PEER_EXPERT_BLOCK = 512
PEER_KEY_ROWS = 256
PEER_KEY_SPLIT = 4
PEER_LINK_ROLLS = 1
NEG_INF = float("-inf")


def _params(*sem):
    return pltpu.CompilerParams(dimension_semantics=sem, vmem_limit_bytes=V7X_VMEM_LIMIT_BYTES)


def _blk(n, pref):
    b = min(n, pref)
    while n % b:
        b //= 2
    return b


def _cast_kernel(x_ref, o_ref):
    o_ref[...] = x_ref[...].astype(o_ref.dtype)


def _to_bf16(a):
    m, n = a.shape
    rows = max(2 * SUBLANES, (8 * 1024 * 1024) // (4 * n))
    bm = _blk(m, 1 << (rows.bit_length() - 1))
    spec = pl.BlockSpec((bm, n), lambda i: (i, 0))
    return pl.pallas_call(
        _cast_kernel, out_shape=jax.ShapeDtypeStruct((m, n), jnp.bfloat16), grid=(m // bm,),
        in_specs=[spec], out_specs=spec, compiler_params=_params("parallel"), name="to_bf16",
    )(a)


def _rope_kernel(pos_ref, freq_ref, cos_ref, sin_ref):
    ang = pos_ref[...].astype(jnp.float32) * freq_ref[...]
    hd = freq_ref.shape[1]
    lane = lax.broadcasted_iota(jnp.int32, ang.shape, 1)
    cos_ref[...] = jnp.cos(ang)
    s = jnp.sin(ang)
    sin_ref[...] = jnp.where(lane < hd // 2, -s, s)


def _rope_tables(positions, hd):
    t = positions.size
    bm = _blk(t, 2048)
    inv = ROPE_THETA ** (-np.arange(0, hd, 2, dtype=np.float32) / np.float32(hd))
    freq = jnp.asarray(np.concatenate([inv, inv]).astype(np.float32).reshape(1, hd))
    out = jax.ShapeDtypeStruct((t, hd), jnp.float32)
    return pl.pallas_call(
        _rope_kernel, out_shape=(out, out), grid=(t // bm,),
        in_specs=[pl.BlockSpec((bm, 1), lambda i: (i, 0)), pl.BlockSpec((1, hd), lambda i: (0, 0))],
        out_specs=(pl.BlockSpec((bm, hd), lambda i: (i, 0)), pl.BlockSpec((bm, hd), lambda i: (i, 0))),
        compiler_params=_params("parallel"), name="rope_tables",
    )(positions.reshape(t, 1), freq)


def _qkv_kernel(x_ref, w_ref, cs_ref, cos_ref, sin_ref, o_ref, *, rope_lo, rope_hi, hd):
    acc = jnp.dot(x_ref[...], w_ref[...], preferred_element_type=jnp.float32) * cs_ref[...]
    j = pl.program_id(1)
    is_rope = jnp.logical_and(j >= rope_lo, j < rope_hi)

    @pl.when(is_rope)
    def _():
        c = cos_ref[...]
        s = sin_ref[...]
        for t in range(acc.shape[1] // hd):
            a = acc[:, t * hd:(t + 1) * hd]
            o_ref[:, t * hd:(t + 1) * hd] = (a * c + pltpu.roll(a, hd // 2, 1) * s).astype(o_ref.dtype)

    @pl.when(jnp.logical_not(is_rope))
    def _():
        o_ref[...] = acc.astype(o_ref.dtype)


def _qkv_proj(x, w, col_scale, cos, sin, n_out, rope_cols, hd):
    m, k = x.shape
    bm, bn = _blk(m, 1024), _blk(math.gcd(n_out, rope_cols[0], rope_cols[1]), 1024)
    kern = functools.partial(_qkv_kernel, rope_lo=rope_cols[0] // bn, rope_hi=rope_cols[1] // bn, hd=hd)
    return pl.pallas_call(
        kern, out_shape=jax.ShapeDtypeStruct((m, n_out), jnp.bfloat16), grid=(m // bm, n_out // bn),
        in_specs=[pl.BlockSpec((bm, k), lambda i, j: (i, 0)), pl.BlockSpec((k, bn), lambda i, j: (0, j)),
                  pl.BlockSpec((1, bn), lambda i, j: (0, j)),
                  pl.BlockSpec((bm, hd), lambda i, j: (i, 0)), pl.BlockSpec((bm, hd), lambda i, j: (i, 0))],
        out_specs=pl.BlockSpec((bm, bn), lambda i, j: (i, j)),
        compiler_params=_params("parallel", "parallel"), name="qkv_proj",
    )(x, w, col_scale, cos, sin)


def _gate_kernel(x_ref, w_ref, b_ref, o_ref):
    acc = jnp.dot(x_ref[...], w_ref[...], preferred_element_type=jnp.float32)
    o_ref[...] = jax.nn.sigmoid(acc + b_ref[...]).astype(o_ref.dtype)


def _gate_proj(x, w, bias, col0, n_out):
    m, k = x.shape
    bm, bn = _blk(m, 1024), _blk(math.gcd(n_out, col0), 1024)
    off = col0 // bn
    return pl.pallas_call(
        _gate_kernel, out_shape=jax.ShapeDtypeStruct((m, n_out), jnp.float32), grid=(m // bm, n_out // bn),
        in_specs=[pl.BlockSpec((bm, k), lambda i, j: (i, 0)), pl.BlockSpec((k, bn), lambda i, j: (0, j + off)),
                  pl.BlockSpec((1, bn), lambda i, j: (0, j))],
        out_specs=pl.BlockSpec((bm, bn), lambda i, j: (i, j)),
        compiler_params=_params("parallel", "parallel"), name="gate_proj",
    )(x, w, bias)


def _mm_kernel(a_ref, b_ref, o_ref):
    o_ref[...] = jnp.dot(a_ref[...], b_ref[...], preferred_element_type=jnp.float32).astype(o_ref.dtype)


def _matmul(a, b, out_dtype, name):
    m, k = a.shape
    n = b.shape[1]
    bm, bn = _blk(m, 1024), _blk(n, 1024)
    return pl.pallas_call(
        _mm_kernel, out_shape=jax.ShapeDtypeStruct((m, n), out_dtype), grid=(m // bm, n // bn),
        in_specs=[pl.BlockSpec((bm, k), lambda i, j: (i, 0)), pl.BlockSpec((k, bn), lambda i, j: (0, j))],
        out_specs=pl.BlockSpec((bm, bn), lambda i, j: (i, j)),
        compiler_params=_params("parallel", "parallel"), name=name,
    )(a, b)


def _merge_kernel(osb_ref, oda_ref, wsb_ref, wda_ref, gsb_ref, gda_ref, o_ref):
    ysb = jnp.dot(osb_ref[...], wsb_ref[...], preferred_element_type=jnp.float32)
    yda = jnp.dot(oda_ref[...], wda_ref[...], preferred_element_type=jnp.float32)
    o_ref[...] = (gsb_ref[...] * ysb + gda_ref[...] * yda).astype(o_ref.dtype)


def _merge(o_sb, o_da, w_sb, w_da, gate):
    m = o_sb.shape[0]
    d = w_sb.shape[1]
    bm, bn = _blk(m, 1024), _blk(d, 512)
    goff = d // bn
    return pl.pallas_call(
        _merge_kernel, out_shape=jax.ShapeDtypeStruct((m, d), jnp.bfloat16), grid=(m // bm, d // bn),
        in_specs=[pl.BlockSpec((bm, o_sb.shape[1]), lambda i, j: (i, 0)),
                  pl.BlockSpec((bm, o_da.shape[1]), lambda i, j: (i, 0)),
                  pl.BlockSpec((w_sb.shape[0], bn), lambda i, j: (0, j)),
                  pl.BlockSpec((w_da.shape[0], bn), lambda i, j: (0, j)),
                  pl.BlockSpec((bm, bn), lambda i, j: (i, j)),
                  pl.BlockSpec((bm, bn), lambda i, j: (i, j + goff))],
        out_specs=pl.BlockSpec((bm, bn), lambda i, j: (i, j)),
        compiler_params=_params("parallel", "parallel"), name="branch_merge",
    )(o_sb, o_da, w_sb, w_da, gate, gate)


def _ln_rows(res_ref, y_ref, g_ref, b_ref, alpha):
    z = alpha * res_ref[...] + y_ref[...]
    mu = jnp.mean(z, axis=-1, keepdims=True)
    zc = z - mu
    var = jnp.mean(zc * zc, axis=-1, keepdims=True)
    return zc * lax.rsqrt(var + LN_EPS) * g_ref[...] + b_ref[...]


def _ln_kernel(res_ref, y_ref, g_ref, b_ref, o_ref, *low_ref, alpha):
    out = _ln_rows(res_ref, y_ref, g_ref, b_ref, alpha)
    o_ref[...] = out
    if low_ref:
        low_ref[0][...] = out.astype(low_ref[0].dtype)


def _ln_peer_kernel(res_ref, y_ref, g_ref, b_ref, wq_ref, o_ref, low_t_ref, hq_ref, *, alpha):
    out = _ln_rows(res_ref, y_ref, g_ref, b_ref, alpha)
    o_ref[...] = out
    low = out.astype(low_t_ref.dtype)
    low_t_ref[...] = low.T
    hq_ref[...] = jnp.dot(low, wq_ref[...], preferred_element_type=jnp.float32).astype(hq_ref.dtype)


def _residual_ln(res, y, g, b, alpha, with_bf16):
    m, d = res.shape
    bm = _blk(m, 256)
    row = pl.BlockSpec((bm, d), lambda i: (i, 0))
    vec = pl.BlockSpec((1, d), lambda i: (0, 0))
    out_shape = [jax.ShapeDtypeStruct((m, d), jnp.float32)] + [jax.ShapeDtypeStruct((m, d), jnp.bfloat16)] * with_bf16
    return pl.pallas_call(
        functools.partial(_ln_kernel, alpha=alpha), out_shape=tuple(out_shape),
        grid=(m // bm,), in_specs=[row, row, vec, vec], out_specs=(row,) * len(out_shape),
        compiler_params=_params("parallel"), name="residual_layernorm",
    )(res, y, g.reshape(1, d), b.reshape(1, d))


def _residual_ln_peer_query(res, y, g, b, alpha, w_q):
    m, d = res.shape
    nq = w_q.shape[1]
    bm = _blk(m, 256)
    row = pl.BlockSpec((bm, d), lambda i: (i, 0))
    vec = pl.BlockSpec((1, d), lambda i: (0, 0))
    return pl.pallas_call(
        functools.partial(_ln_peer_kernel, alpha=alpha),
        out_shape=(jax.ShapeDtypeStruct((m, d), jnp.float32), jax.ShapeDtypeStruct((d, m), jnp.bfloat16),
                   jax.ShapeDtypeStruct((m, nq), jnp.bfloat16)),
        grid=(m // bm,),
        in_specs=[row, row, vec, vec, pl.BlockSpec((d, nq), lambda i: (0, 0), pipeline_mode=pl.Buffered(1))],
        out_specs=(row, pl.BlockSpec((d, bm), lambda i: (0, i)), pl.BlockSpec((bm, nq), lambda i: (i, 0))),
        compiler_params=_params("parallel"), name="layernorm_peer_query",
    )(res, y, g.reshape(1, d), b.reshape(1, d), w_q)


SB_SKIP_LOG2 = -100.0 * math.log2(math.e)


def _sb_kernel(q_ref, k_ref, v_ref, tri_ref, o_ref, c_ref, acc_ref, *, blk, hd, hps):
    qi = pl.program_id(2)
    row = lax.broadcasted_iota(jnp.int32, (blk, blk), 0)
    col = lax.broadcasted_iota(jnp.int32, (blk, blk), 1)
    valid_diag = col < row
    tri = tri_ref[...]

    def visit(j, valid):
        start = pl.multiple_of(j * blk, blk)
        zs, log_fails = [], []
        for g in range(hps):
            q = q_ref[:, g * hd:(g + 1) * hd]
            k = k_ref[pl.ds(start, blk), g * hd:(g + 1) * hd]
            z = lax.dot_general(q, k, (((1,), (1,)), ((), ())), preferred_element_type=jnp.float32)
            log_fail = -(jnp.maximum(z, 0.0) + jnp.log2(1.0 + jnp.exp2(-jnp.abs(z))))
            if valid is not None:
                log_fail = jnp.where(valid, log_fail, 0.0)
            zs.append(z)
            log_fails.append(log_fail)
        sums_all = jnp.dot(jnp.concatenate([lf.astype(jnp.bfloat16) for lf in log_fails], axis=0), tri,
                           preferred_element_type=jnp.float32)
        for g in range(hps):
            v = v_ref[pl.ds(start, blk), g * hd:(g + 1) * hd]
            sums = sums_all[g * blk:(g + 1) * blk]
            c = c_ref[g]
            w = jnp.exp2(zs[g] + log_fails[g] + sums[:, :blk] + c)
            if valid is not None:
                w = jnp.where(valid, w, 0.0)
            acc_ref[g] += jnp.dot(w.astype(v.dtype), v, preferred_element_type=jnp.float32)
            c_ref[g] = c + sums[:, blk:]

    def bound():
        return jnp.max(c_ref[:, :, :hd])

    c_ref[...] = jnp.zeros_like(c_ref)
    acc_ref[...] = jnp.zeros_like(acc_ref)
    visit(qi, valid_diag)

    def cond(state):
        j, worst = state
        return jnp.logical_and(j >= 0, worst > SB_SKIP_LOG2)

    def body(state):
        j, _ = state
        visit(j, None)
        return j - 1, bound()

    lax.while_loop(cond, body, (qi - 1, bound()))
    for g in range(hps):
        o_ref[:, g * hd:(g + 1) * hd] = acc_ref[g].astype(o_ref.dtype)


def _sb_attention(qkv, batch, seq, heads, hd):
    blk = _blk(seq, 256)
    nq = seq // blk
    hps = max(h for h in (4, 2, 1) if heads % h == 0)
    hg, w = heads // hps, hps * hd
    tri = np.concatenate([np.tril(np.ones((blk, blk), np.float32), -1), np.ones((blk, blk), np.float32)], axis=1)
    kern = functools.partial(_sb_kernel, blk=blk, hd=hd, hps=hps)
    return pl.pallas_call(
        kern, out_shape=jax.ShapeDtypeStruct((batch * seq, heads * hd), jnp.bfloat16), grid=(batch, hg, nq),
        in_specs=[pl.BlockSpec((blk, w), lambda b, h, i: (b * nq + i, h)),
                  pl.BlockSpec((seq, w), lambda b, h, i: (b, hg + h)),
                  pl.BlockSpec((seq, w), lambda b, h, i: (b, 2 * hg + h)),
                  pl.BlockSpec((blk, 2 * blk), lambda b, h, i: (0, 0))],
        out_specs=pl.BlockSpec((blk, w), lambda b, h, i: (b * nq + i, h)),
        scratch_shapes=[pltpu.VMEM((hps, blk, blk), jnp.float32), pltpu.VMEM((hps, blk, hd), jnp.float32)],
        compiler_params=_params("parallel", "parallel", "parallel"), name="stick_breaking_attention",
    )(qkv, qkv, qkv, jnp.asarray(tri, jnp.bfloat16))


def _da_kernel(lq1_ref, lk1_ref, lq2_ref, lk2_ref, g_ref, q1_ref, q2_ref, k1_ref, k2_ref, v_ref, o_ref,
               m_ref, l_ref, acc_ref, *, blk, hd, lam_init):
    qi = pl.program_id(2)
    row = lax.broadcasted_iota(jnp.int32, (blk, blk), 0)
    col = lax.broadcasted_iota(jnp.int32, (blk, blk), 1)
    causal_diag = col <= row
    qs = (q1_ref[...], q2_ref[...])
    ks = (k1_ref, k2_ref)

    m_ref[...] = jnp.full_like(m_ref, NEG_INF)
    l_ref[...] = jnp.zeros_like(l_ref)
    acc_ref[...] = jnp.zeros_like(acc_ref)

    def visit(j, mask):
        start = pl.multiple_of(j * blk, blk)
        v = v_ref[pl.ds(start, blk), :]
        probs, corrs = [], []
        for s in range(2):
            k = ks[s][pl.ds(start, blk), :]
            sc = lax.dot_general(qs[s], k, (((1,), (1,)), ((), ())), preferred_element_type=jnp.float32)
            if mask is not None:
                sc = jnp.where(mask, sc, NEG_INF)
            m_old = m_ref[s]
            m_new = jnp.maximum(m_old, jnp.max(sc, axis=-1, keepdims=True))
            p = jnp.exp2(sc - jnp.concatenate([m_new] * (blk // hd), axis=1))
            corr = jnp.exp2(m_old - m_new)
            l_ref[s] = corr * l_ref[s] + jnp.sum(p, axis=-1, keepdims=True)
            m_ref[s] = m_new
            probs.append(p.astype(v.dtype))
            corrs.append(jnp.concatenate([corr, corr], axis=1))
        pv = jnp.dot(jnp.concatenate(probs, axis=0), v, preferred_element_type=jnp.float32)
        for s in range(2):
            acc_ref[s] = corrs[s] * acc_ref[s] + pv[s * blk:(s + 1) * blk]

    def body(j, carry):
        visit(j, None)
        return carry

    lax.fori_loop(0, qi, body, 0)
    visit(qi, causal_diag)

    lam = (jnp.exp(jnp.sum(lq1_ref[...] * lk1_ref[...], axis=-1, keepdims=True))
           - jnp.exp(jnp.sum(lq2_ref[...] * lk2_ref[...], axis=-1, keepdims=True)) + lam_init)
    l1 = jnp.concatenate([l_ref[0], l_ref[0]], axis=1)
    l2 = jnp.concatenate([l_ref[1], l_ref[1]], axis=1)
    o = acc_ref[0] / l1 - lam * (acc_ref[1] / l2)
    o = o * lax.rsqrt(jnp.mean(o * o, axis=-1, keepdims=True) + LN_EPS) * g_ref[...] * (1.0 - lam_init)
    o_ref[...] = o.astype(o_ref.dtype)


def _da_attention(qkv, lambdas, subln_g, batch, seq, heads, hd, col0, lam_init):
    blk = _blk(seq, 512)
    nq = seq // blk
    qb, kb, vb = col0 // hd, col0 // hd + 2 * heads, col0 // (2 * hd) + 2 * heads
    vec = pl.BlockSpec((1, hd), lambda b, h, i: (0, 0))
    kern = functools.partial(_da_kernel, blk=blk, hd=hd, lam_init=lam_init)
    return pl.pallas_call(
        kern, out_shape=jax.ShapeDtypeStruct((batch * seq, heads * 2 * hd), jnp.bfloat16), grid=(batch, heads, nq),
        in_specs=[vec, vec, vec, vec, pl.BlockSpec((1, 2 * hd), lambda b, h, i: (0, 0)),
                  pl.BlockSpec((blk, hd), lambda b, h, i: (b * nq + i, qb + 2 * h)),
                  pl.BlockSpec((blk, hd), lambda b, h, i: (b * nq + i, qb + 2 * h + 1)),
                  pl.BlockSpec((seq, hd), lambda b, h, i: (b, kb + 2 * h)),
                  pl.BlockSpec((seq, hd), lambda b, h, i: (b, kb + 2 * h + 1)),
                  pl.BlockSpec((seq, 2 * hd), lambda b, h, i: (b, vb + h))],
        out_specs=pl.BlockSpec((blk, 2 * hd), lambda b, h, i: (b * nq + i, h)),
        scratch_shapes=[pltpu.VMEM((2, blk, hd), jnp.float32), pltpu.VMEM((2, blk, hd), jnp.float32),
                        pltpu.VMEM((2, blk, 2 * hd), jnp.float32)],
        compiler_params=_params("parallel", "parallel", "parallel"), name="differential_attention",
    )(*[l.reshape(1, hd) for l in lambdas], subln_g.reshape(1, 2 * hd), qkv, qkv, qkv, qkv, qkv)


def _extract_top(x, n):
    rows = lax.broadcasted_iota(jnp.int32, x.shape, 0).astype(jnp.float32)
    tops = []
    for _ in range(n):
        m = jnp.max(x, axis=0, keepdims=True)
        tops.append(m)
        first = jnp.min(jnp.where(x == m, rows, float(x.shape[0])), axis=0, keepdims=True)
        x = jnp.where(rows == first, NEG_INF, x)
    return tops


def _sorting_network(n):
    pairs, p = [], 1
    while p < n:
        k = p
        while k >= 1:
            for j in range(k % p, n - k, 2 * k):
                for i in range(min(k, n - j - k)):
                    if (i + j) // (2 * p) == (i + j + k) // (2 * p):
                        pairs.append((i + j, i + j + k))
            k //= 2
        p *= 2
    return pairs


def _extract_top_sorted(x, n):
    groups = [x[g * SUBLANES:(g + 1) * SUBLANES] for g in range(x.shape[0] // SUBLANES)]
    for a, b in _sorting_network(len(groups)):
        groups[a], groups[b] = jnp.maximum(groups[a], groups[b]), jnp.minimum(groups[a], groups[b])
    lane_list = lax.broadcasted_iota(jnp.int32, groups[0].shape, 0).astype(jnp.float32)
    tops = []
    for t in range(n):
        head = groups[0]
        m = jnp.max(head, axis=0, keepdims=True)
        tops.append(m)
        if t == n - 1:
            break
        first = jnp.min(jnp.where(head == m, lane_list, float(SUBLANES)), axis=0, keepdims=True)
        popped = lane_list == first
        live = min(len(groups), n - t)
        for g in range(live):
            below = groups[g + 1] if g + 1 < len(groups) else NEG_INF
            groups[g] = jnp.where(popped, below, groups[g])
    return tops


def _pair_candidates(top_a, top_b, combine):
    k = len(top_a)
    sub = SUBLANES
    b_all = jnp.concatenate(top_b, axis=0)
    pieces, masks, singles = [], [], []
    for p in range(k):
        n_q = k // (p + 1)
        if n_q == 1:
            singles.append(top_a[p])
            continue
        rows = -(-n_q // sub) * sub
        pieces.append(combine(top_a[p], b_all[0:rows]))
        masks.append(lax.broadcasted_iota(jnp.int32, (rows, 1), 0) < n_q)
    if singles:
        pieces.append(combine(jnp.concatenate(singles, axis=0), top_b[0]))
        masks.append(None)
    return pieces, masks


def _peer_topk_kernel(q_ref, keys_ref, s_ref, e_ref, tau_ref, *, n_keys, topk):
    half = q_ref.shape[1] // 2
    scores, tops = [], []
    for c in range(2):
        s = lax.dot_general(keys_ref[0, c], q_ref[:, c * half:(c + 1) * half], (((1,), (1,)), ((), ())),
                            preferred_element_type=jnp.float32)
        scores.append(s)
        n_groups = n_keys // SUBLANES
        pow2 = n_keys % SUBLANES == 0 and n_groups & (n_groups - 1) == 0
        tops.append(_extract_top_sorted(s, topk) if pow2 else _extract_top(s, topk))
    exps = [[jnp.exp(t - top[0]) for t in top] for top in tops]
    sums, masks = _pair_candidates(tops[0], tops[1], jnp.add)
    prods, _ = _pair_candidates(exps[0], exps[1], jnp.multiply)
    cand = jnp.concatenate([s if m is None else jnp.where(m, s, NEG_INF) for s, m in zip(sums, masks)], axis=0)
    cand_e = jnp.concatenate(prods, axis=0)
    groups = -(-cand.shape[0] // SUBLANES)
    pad_rows = SUBLANES * (1 << (groups - 1).bit_length()) - cand.shape[0]
    padded = jnp.concatenate([cand, jnp.full((pad_rows, cand.shape[1]), NEG_INF, cand.dtype)], axis=0)
    tau = _extract_top_sorted(padded, topk)[-1]
    z = jnp.sum(jnp.where(cand >= tau, cand_e, 0.0), axis=0, keepdims=True)
    s_ref[...] = jnp.concatenate(scores, axis=0)
    e_ref[...] = jnp.concatenate([jnp.exp(scores[0] - tops[0][0]) / z, jnp.exp(scores[1] - tops[1][0])], axis=0)
    tau_ref[0] = tau


def _peer_topk(hq, keys):
    t = hq.shape[0]
    heads, _, n_keys, half = keys.shape
    tn = _blk(t, 512)
    kern = functools.partial(_peer_topk_kernel, n_keys=n_keys, topk=PEER_TOPK)
    st = jax.ShapeDtypeStruct((heads * 2 * n_keys, t), jnp.float32)
    blk = pl.BlockSpec((2 * n_keys, tn), lambda i, h: (h, i))
    return pl.pallas_call(
        kern, out_shape=(st, st, jax.ShapeDtypeStruct((heads, 1, t), jnp.float32)), grid=(t // tn, heads),
        in_specs=[pl.BlockSpec((tn, 2 * half), lambda i, h: (i, h)),
                  pl.BlockSpec((1, 2, n_keys, half), lambda i, h: (h, 0, 0, 0))],
        out_specs=(blk, blk, pl.BlockSpec((1, 1, tn), lambda i, h: (h, 0, i))),
        compiler_params=_params("parallel", "parallel"), name="peer_topk",
    )(hq, keys)


def _peer_dense_kernel(h_ref, u_ref, v_ref, s_ref, e_ref, tau_ref, o_ref, ga_ref, *, heads, n_keys, n_blocks):
    j = pl.program_id(1)

    @pl.when(j == 0)
    def _():
        ga_ref[1] = jnp.zeros(ga_ref.shape[1:], ga_ref.dtype)
        o_ref[...] = jnp.zeros_like(o_ref)

    for parity in range(2):
        pl.when(j % 2 == parity)(functools.partial(
            _peer_dense_step, h_ref, u_ref, v_ref, s_ref, e_ref, tau_ref, o_ref, ga_ref,
            jnp.minimum(j, n_blocks - 1), parity, heads, n_keys))


def _peer_dense_step(h_ref, u_ref, v_ref, s_ref, e_ref, tau_ref, o_ref, ga_ref, jb, slot_now, heads, n_keys):
    tm, te = ga_ref.shape[1:]
    d = o_ref.shape[1]
    n_i = te // n_keys
    mxu = MXU_TILE
    n_tok_tiles = tm // mxu
    rt = min(te, PEER_KEY_ROWS)

    def link_from(x):
        bits = pltpu.bitcast(x[0:8, 0:LANES], jnp.uint32)
        zero = lax.shift_right_logical(lax.shift_right_logical(bits, jnp.uint32(16)), jnp.uint32(16))
        return pltpu.bitcast(zero, jnp.float32)

    pre_tiles = {}

    k_parts = PEER_KEY_SPLIT
    kw = d // k_parts
    done_tiles = set()

    def pre_tile(p, kk):
        mi, ni = divmod(p, n_tok_tiles)
        part = jnp.dot(u_ref[mi * rt:(mi + 1) * rt, kk * kw:(kk + 1) * kw],
                       h_ref[kk * kw:(kk + 1) * kw, ni * mxu:(ni + 1) * mxu], preferred_element_type=jnp.float32)
        pre_tiles[mi, ni] = part if kk == 0 else pre_tiles[mi, ni] + part
        if kk == k_parts - 1:
            done_tiles.add((mi, ni))
        return link_from(part)

    def value_chunk(c):
        cols = slice(c * mxu, (c + 1) * mxu)
        acc = o_ref[:, cols] + jnp.dot(ga_ref[1 - slot_now], v_ref[:, cols], preferred_element_type=jnp.float32)
        o_ref[:, cols] = acc
        return link_from(acc)

    n_pre = (te // rt) * n_tok_tiles
    chunks = [functools.partial(pre_tile, p, kk) for p in range(n_pre) for kk in range(k_parts)]
    chunks += [functools.partial(value_chunk, c) for c in range(d // mxu)]
    results = []

    def finalize(il):
        mi, off = divmod(il * n_keys, rt)
        while any((mi, ni) not in done_tiles for ni in range(n_tok_tiles)):
            results.append(chunks.pop(0)())
        p = jnp.concatenate([pre_tiles[mi, ni][off:off + n_keys, :] for ni in range(n_tok_tiles)], axis=1)
        act = 0.5 * p * (1.0 + lax.erf(p * (2.0 ** -0.5)))
        ga_ref[slot_now, :, il * n_keys:(il + 1) * n_keys] = (gates.pop(il) * act).astype(ga_ref.dtype).T

    pieces = [(il, h) for il in range(n_i) for h in range(heads)]
    costs = [rt * kw * mxu] * (n_pre * k_parts) + [tm * te * mxu] * (d // mxu)
    ends = [sum(costs[:c + 1]) for c in range(len(costs))]
    slot = ends[-1] / len(pieces)
    linked = 0
    gates = {}
    link = link_from(s_ref[0:8, 0:LANES])
    for k, (il, h) in enumerate(pieces):
        while len(results) < len(costs) and ends[len(results)] - costs[len(results)] < (k + 1) * slot:
            results.append(chunks.pop(0)())
        while linked < len(results) and ends[linked] <= k * slot:
            link, linked = link + results[linked], linked + 1
        for _ in range(PEER_LINK_ROLLS):
            link = pltpu.roll(link, 1, 1)
        base = h * 2 * n_keys
        a_row = s_ref[pl.ds(base + jb * n_i + il, 1), :] + jnp.concatenate([link[0:1, :]] * (tm // LANES), axis=1)
        ea_row = e_ref[pl.ds(base + jb * n_i + il, 1), :]
        b_tile = s_ref[base + n_keys:base + 2 * n_keys, :]
        eb_tile = e_ref[base + n_keys:base + 2 * n_keys, :]
        term = jnp.where(a_row + b_tile >= tau_ref[h], ea_row * eb_tile, 0.0)
        gates[il] = term if h == 0 else gates[il] + term
        if h == heads - 1:
            finalize(il)
    while chunks:
        chunks.pop(0)()


def _peer_dense(h_bf_t, u_bf, v_bf, scores_t, exps_t, tau, heads, n_keys):
    d, t = h_bf_t.shape
    e = u_bf.shape[0]
    tm, te = _blk(t, PEER_TOKEN_BLOCK), _blk(e, PEER_EXPERT_BLOCK)
    nb = e // te
    kern = functools.partial(_peer_dense_kernel, heads=heads, n_keys=n_keys, n_blocks=nb)
    rows = scores_t.shape[0]
    once = pl.Buffered(1)
    return pl.pallas_call(
        kern, out_shape=jax.ShapeDtypeStruct((t, d), jnp.float32), grid=(t // tm, nb + 1),
        in_specs=[pl.BlockSpec((d, tm), lambda i, j: (0, i), pipeline_mode=once),
                  pl.BlockSpec((te, d), lambda i, j: (jnp.minimum(j, nb - 1), 0)),
                  pl.BlockSpec((te, d), lambda i, j: (jnp.maximum(j - 1, 0), 0)),
                  pl.BlockSpec((rows, tm), lambda i, j: (0, i), pipeline_mode=once),
                  pl.BlockSpec((rows, tm), lambda i, j: (0, i), pipeline_mode=once),
                  pl.BlockSpec((heads, 1, tm), lambda i, j: (0, 0, i))],
        out_specs=pl.BlockSpec((tm, d), lambda i, j: (i, 0)),
        scratch_shapes=[pltpu.VMEM((2, tm, te), jnp.bfloat16)],
        compiler_params=_params("parallel", "arbitrary"), name="peer_dense",
    )(h_bf_t, u_bf, v_bf, scores_t, exps_t, tau)


def kernel(x, positions, w_in, b_gate, lambda_q1, lambda_k1, lambda_q2, lambda_k2, subln_g, w_sb_branch,
           w_da_branch, w_out, ln1_g, ln1_b, peer_w_q, peer_sub_keys, peer_u, peer_v, ln2_g, ln2_b):
    batch, seq, d = x.shape
    depth = w_in.shape[0]
    hd = lambda_q1.shape[-1]
    sb_w, da_w = w_sb_branch.shape[1], w_da_branch.shape[1]
    sb_heads, da_heads = sb_w // hd, da_w // (2 * hd)
    n_qkv = 3 * sb_w + 3 * da_w
    alpha = (2 * depth) ** 0.25
    bf = jnp.bfloat16
    t = batch * seq

    cos, sin = _rope_tables(positions, hd)
    col_scale = np.ones((1, n_qkv), np.float32)
    col_scale[:, :sb_w] = hd ** -0.5 * math.log2(math.e)
    col_scale[:, 3 * sb_w:3 * sb_w + da_w] = hd ** -0.5 * math.log2(math.e)
    col_scale = jnp.asarray(col_scale)
    h = x.reshape(t, d)
    h_bf = _to_bf16(h)
    for l in range(depth):
        lam_init = 0.8 - 0.6 * math.exp(-0.3 * l)
        w_in_bf = _to_bf16(w_in[l])
        qkv = _qkv_proj(h_bf, w_in_bf, col_scale, cos, sin, n_qkv, (3 * sb_w, 3 * sb_w + 2 * da_w), hd)
        gate = _gate_proj(h_bf, w_in_bf, b_gate[l].reshape(1, -1), n_qkv, 2 * d)
        o_sb = _sb_attention(qkv, batch, seq, sb_heads, hd)
        o_da = _da_attention(qkv, (lambda_q1[l], lambda_k1[l], lambda_q2[l], lambda_k2[l]), subln_g[l],
                             batch, seq, da_heads, hd, 3 * sb_w, lam_init)
        merged = _merge(o_sb, o_da, _to_bf16(w_sb_branch[l]), _to_bf16(w_da_branch[l]), gate)
        mix = _matmul(merged, _to_bf16(w_out[l]), jnp.float32, "out_proj")
        h, h_bf_t, hq = _residual_ln_peer_query(h, mix, ln1_g[l], ln1_b[l], alpha, _to_bf16(peer_w_q[l]))

        heads, _, n_keys, half = peer_sub_keys[l].shape
        scores_t, exps_t, tau = _peer_topk(hq, peer_sub_keys[l].astype(bf))
        ffn = _peer_dense(h_bf_t, _to_bf16(peer_u[l]), _to_bf16(peer_v[l]), scores_t, exps_t, tau, heads, n_keys)
        outs = _residual_ln(h, ffn, ln2_g[l], ln2_b[l], alpha, l + 1 < depth)
        h = outs[0]
        if l + 1 < depth:
            h_bf = outs[1]
    return h.reshape(batch, seq, d)
```

```python
import functools
import math

import numpy as np
import jax
import jax.numpy as jnp
from jax import lax
from jax.experimental import pallas as pl
from jax.experimental.pallas import tpu as pltpu

LN_EPS = 1e-5
ROPE_THETA = 10000.0
PEER_TOPK = 16
V7X_VMEM_LIMIT_BYTES = 56 * 1024 * 1024
LANES = 128
SUBLANES = 8
MXU_TILE = 256
PEER_TOKEN_BLOCK = 512
PEER_EXPERT_BLOCK = 512
PEER_KEY_ROWS = 256
PEER_KEY_SPLIT = 4
PEER_LINK_ROLLS = 1
NEG_INF = float("-inf")


def _params(*sem):
    return pltpu.CompilerParams(dimension_semantics=sem, vmem_limit_bytes=V7X_VMEM_LIMIT_BYTES)


def _blk(n, pref):
    b = min(n, pref)
    while n % b:
        b //= 2
    return b


def _cast_kernel(x_ref, o_ref):
    o_ref[...] = x_ref[...].astype(o_ref.dtype)


def _to_bf16(a):
    m, n = a.shape
    rows = max(2 * SUBLANES, (8 * 1024 * 1024) // (4 * n))
    bm = _blk(m, 1 << (rows.bit_length() - 1))
    spec = pl.BlockSpec((bm, n), lambda i: (i, 0))
    return pl.pallas_call(
        _cast_kernel, out_shape=jax.ShapeDtypeStruct((m, n), jnp.bfloat16), grid=(m // bm,),
        in_specs=[spec], out_specs=spec, compiler_params=_params("parallel"), name="to_bf16",
    )(a)


def _side_cast_plan(arrays, grid):
    steps, nj = grid[0] * grid[1], grid[1]
    specs, shapes, blocks = [], [], []
    for a in arrays:
        r, c = a.shape
        nb = 1
        while nb * 2 <= steps and r % (nb * 2) == 0 and (r // (nb * 2)) % (2 * SUBLANES) == 0:
            nb *= 2
        specs.append(pl.BlockSpec((r // nb, c), lambda i, j, nb=nb: (jnp.minimum(i * nj + j, nb - 1), 0)))
        shapes.append(jax.ShapeDtypeStruct((r, c), jnp.bfloat16))
        blocks.append(nb)
    return specs, shapes, tuple(blocks)


def _side_casts(src_refs, dst_refs, side_blocks):
    step = pl.program_id(0) * pl.num_programs(1) + pl.program_id(1)
    for src, dst, nb in zip(src_refs, dst_refs, side_blocks):
        @pl.when(step < nb)
        def _(src=src, dst=dst):
            dst[...] = src[...].astype(dst.dtype)


def _rope_kernel(pos_ref, freq_ref, cos_ref, sin_ref):
    ang = pos_ref[...].astype(jnp.float32) * freq_ref[...]
    hd = freq_ref.shape[1]
    lane = lax.broadcasted_iota(jnp.int32, ang.shape, 1)
    cos_ref[...] = jnp.cos(ang)
    s = jnp.sin(ang)
    sin_ref[...] = jnp.where(lane < hd // 2, -s, s)


def _rope_tables(positions, hd):
    t = positions.size
    bm = _blk(t, 2048)
    inv = ROPE_THETA ** (-np.arange(0, hd, 2, dtype=np.float32) / np.float32(hd))
    freq = jnp.asarray(np.concatenate([inv, inv]).astype(np.float32).reshape(1, hd))
    out = jax.ShapeDtypeStruct((t, hd), jnp.float32)
    return pl.pallas_call(
        _rope_kernel, out_shape=(out, out), grid=(t // bm,),
        in_specs=[pl.BlockSpec((bm, 1), lambda i: (i, 0)), pl.BlockSpec((1, hd), lambda i: (0, 0))],
        out_specs=(pl.BlockSpec((bm, hd), lambda i: (i, 0)), pl.BlockSpec((bm, hd), lambda i: (i, 0))),
        compiler_params=_params("parallel"), name="rope_tables",
    )(positions.reshape(t, 1), freq)


def _qkv_kernel(x_ref, w_ref, cs_ref, cos_ref, sin_ref, *refs, rope_lo, rope_hi, hd, side_blocks):
    n = len(side_blocks)
    o_ref = refs[n]
    _side_casts(refs[:n], refs[n + 1:], side_blocks)
    acc = jnp.dot(x_ref[...], w_ref[...], preferred_element_type=jnp.float32) * cs_ref[...]
    j = pl.program_id(1)
    is_rope = jnp.logical_and(j >= rope_lo, j < rope_hi)

    @pl.when(is_rope)
    def _():
        c = cos_ref[...]
        s = sin_ref[...]
        for t in range(acc.shape[1] // hd):
            a = acc[:, t * hd:(t + 1) * hd]
            o_ref[:, t * hd:(t + 1) * hd] = (a * c + pltpu.roll(a, hd // 2, 1) * s).astype(o_ref.dtype)

    @pl.when(jnp.logical_not(is_rope))
    def _():
        o_ref[...] = acc.astype(o_ref.dtype)


def _qkv_proj(x, w, col_scale, cos, sin, n_out, rope_cols, hd, side):
    m, k = x.shape
    bm, bn = _blk(m, 1024), _blk(math.gcd(n_out, rope_cols[0], rope_cols[1]), 1024)
    grid = (m // bm, n_out // bn)
    side_specs, side_shapes, side_blocks = _side_cast_plan(side, grid)
    kern = functools.partial(_qkv_kernel, rope_lo=rope_cols[0] // bn, rope_hi=rope_cols[1] // bn, hd=hd,
                             side_blocks=side_blocks)
    return pl.pallas_call(
        kern, out_shape=(jax.ShapeDtypeStruct((m, n_out), jnp.bfloat16), *side_shapes), grid=grid,
        in_specs=[pl.BlockSpec((bm, k), lambda i, j: (i, 0)), pl.BlockSpec((k, bn), lambda i, j: (0, j)),
                  pl.BlockSpec((1, bn), lambda i, j: (0, j)),
                  pl.BlockSpec((bm, hd), lambda i, j: (i, 0)), pl.BlockSpec((bm, hd), lambda i, j: (i, 0)),
                  *side_specs],
        out_specs=(pl.BlockSpec((bm, bn), lambda i, j: (i, j)), *side_specs),
        compiler_params=_params("arbitrary", "arbitrary"), name="qkv_proj",
    )(x, w, col_scale, cos, sin, *side)


def _gate_kernel(x_ref, w_ref, b_ref, *refs, side_blocks):
    n = len(side_blocks)
    o_ref = refs[n]
    acc = jnp.dot(x_ref[...], w_ref[...], preferred_element_type=jnp.float32)
    o_ref[...] = jax.nn.sigmoid(acc + b_ref[...]).astype(o_ref.dtype)
    _side_casts(refs[:n], refs[n + 1:], side_blocks)


def _gate_proj(x, w, bias, col0, n_out, side):
    m, k = x.shape
    bm, bn = _blk(m, 1024), _blk(math.gcd(n_out, col0), 1024)
    off = col0 // bn
    grid = (m // bm, n_out // bn)
    side_specs, side_shapes, side_blocks = _side_cast_plan(side, grid)
    return pl.pallas_call(
        functools.partial(_gate_kernel, side_blocks=side_blocks),
        out_shape=(jax.ShapeDtypeStruct((m, n_out), jnp.float32), *side_shapes), grid=grid,
        in_specs=[pl.BlockSpec((bm, k), lambda i, j: (i, 0)), pl.BlockSpec((k, bn), lambda i, j: (0, j + off)),
                  pl.BlockSpec((1, bn), lambda i, j: (0, j)), *side_specs],
        out_specs=(pl.BlockSpec((bm, bn), lambda i, j: (i, j)), *side_specs),
        compiler_params=_params("arbitrary", "arbitrary"), name="gate_proj",
    )(x, w, bias, *side)


def _mm_kernel(a_ref, b_ref, o_ref):
    o_ref[...] = jnp.dot(a_ref[...], b_ref[...], preferred_element_type=jnp.float32).astype(o_ref.dtype)


def _matmul(a, b, out_dtype, name):
    m, k = a.shape
    n = b.shape[1]
    bm, bn = _blk(m, 1024), _blk(n, 1024)
    return pl.pallas_call(
        _mm_kernel, out_shape=jax.ShapeDtypeStruct((m, n), out_dtype), grid=(m // bm, n // bn),
        in_specs=[pl.BlockSpec((bm, k), lambda i, j: (i, 0)), pl.BlockSpec((k, bn), lambda i, j: (0, j))],
        out_specs=pl.BlockSpec((bm, bn), lambda i, j: (i, j)),
        compiler_params=_params("parallel", "parallel"), name=name,
    )(a, b)


def _merge_kernel(osb_ref, oda_ref, wsb_ref, wda_ref, gsb_ref, gda_ref, *refs, side_blocks):
    n = len(side_blocks)
    o_ref = refs[n]
    ysb = jnp.dot(osb_ref[...], wsb_ref[...], preferred_element_type=jnp.float32)
    yda = jnp.dot(oda_ref[...], wda_ref[...], preferred_element_type=jnp.float32)
    o_ref[...] = (gsb_ref[...] * ysb + gda_ref[...] * yda).astype(o_ref.dtype)
    _side_casts(refs[:n], refs[n + 1:], side_blocks)


def _merge(o_sb, o_da, w_sb, w_da, gate, side):
    m = o_sb.shape[0]
    d = w_sb.shape[1]
    bm, bn = _blk(m, 1024), _blk(d, 512)
    goff = d // bn
    grid = (m // bm, d // bn)
    side_specs, side_shapes, side_blocks = _side_cast_plan(side, grid)
    return pl.pallas_call(
        functools.partial(_merge_kernel, side_blocks=side_blocks),
        out_shape=(jax.ShapeDtypeStruct((m, d), jnp.bfloat16), *side_shapes), grid=grid,
        in_specs=[pl.BlockSpec((bm, o_sb.shape[1]), lambda i, j: (i, 0)),
                  pl.BlockSpec((bm, o_da.shape[1]), lambda i, j: (i, 0)),
                  pl.BlockSpec((w_sb.shape[0], bn), lambda i, j: (0, j)),
                  pl.BlockSpec((w_da.shape[0], bn), lambda i, j: (0, j)),
                  pl.BlockSpec((bm, bn), lambda i, j: (i, j)),
                  pl.BlockSpec((bm, bn), lambda i, j: (i, j + goff)), *side_specs],
        out_specs=(pl.BlockSpec((bm, bn), lambda i, j: (i, j)), *side_specs),
        compiler_params=_params("arbitrary", "arbitrary"), name="branch_merge",
    )(o_sb, o_da, w_sb, w_da, gate, gate, *side)


def _ln_rows(res_ref, y_ref, g_ref, b_ref, alpha):
    z = alpha * res_ref[...] + y_ref[...]
    mu = jnp.mean(z, axis=-1, keepdims=True)
    zc = z - mu
    var = jnp.mean(zc * zc, axis=-1, keepdims=True)
    return zc * lax.rsqrt(var + LN_EPS) * g_ref[...] + b_ref[...]


def _ln_kernel(res_ref, y_ref, g_ref, b_ref, o_ref, *low_ref, alpha):
    out = _ln_rows(res_ref, y_ref, g_ref, b_ref, alpha)
    o_ref[...] = out
    if low_ref:
        low_ref[0][...] = out.astype(low_ref[0].dtype)


def _ln_peer_kernel(res_ref, y_ref, g_ref, b_ref, wq_ref, o_ref, low_t_ref, hq_ref, *, alpha):
    out = _ln_rows(res_ref, y_ref, g_ref, b_ref, alpha)
    o_ref[...] = out
    low = out.astype(low_t_ref.dtype)
    low_t_ref[...] = low.T
    hq_ref[...] = jnp.dot(low, wq_ref[...], preferred_element_type=jnp.float32).astype(hq_ref.dtype)


def _residual_ln(res, y, g, b, alpha, with_bf16):
    m, d = res.shape
    bm = _blk(m, 256)
    row = pl.BlockSpec((bm, d), lambda i: (i, 0))
    vec = pl.BlockSpec((1, d), lambda i: (0, 0))
    out_shape = [jax.ShapeDtypeStruct((m, d), jnp.float32)] + [jax.ShapeDtypeStruct((m, d), jnp.bfloat16)] * with_bf16
    return pl.pallas_call(
        functools.partial(_ln_kernel, alpha=alpha), out_shape=tuple(out_shape),
        grid=(m // bm,), in_specs=[row, row, vec, vec], out_specs=(row,) * len(out_shape),
        compiler_params=_params("parallel"), name="residual_layernorm",
    )(res, y, g.reshape(1, d), b.reshape(1, d))


def _residual_ln_peer_query(res, y, g, b, alpha, w_q):
    m, d = res.shape
    nq = w_q.shape[1]
    bm = _blk(m, 256)
    row = pl.BlockSpec((bm, d), lambda i: (i, 0))
    vec = pl.BlockSpec((1, d), lambda i: (0, 0))
    return pl.pallas_call(
        functools.partial(_ln_peer_kernel, alpha=alpha),
        out_shape=(jax.ShapeDtypeStruct((m, d), jnp.float32), jax.ShapeDtypeStruct((d, m), jnp.bfloat16),
                   jax.ShapeDtypeStruct((m, nq), jnp.bfloat16)),
        grid=(m // bm,),
        in_specs=[row, row, vec, vec, pl.BlockSpec((d, nq), lambda i: (0, 0), pipeline_mode=pl.Buffered(1))],
        out_specs=(row, pl.BlockSpec((d, bm), lambda i: (0, i)), pl.BlockSpec((bm, nq), lambda i: (i, 0))),
        compiler_params=_params("parallel"), name="layernorm_peer_query",
    )(res, y, g.reshape(1, d), b.reshape(1, d), w_q)


SB_SKIP_LOG2 = -100.0 * math.log2(math.e)


def _sb_kernel(q_ref, k_ref, v_ref, tri_ref, o_ref, c_ref, acc_ref, *, blk, hd, hps):
    qi = pl.program_id(2)
    row = lax.broadcasted_iota(jnp.int32, (blk, blk), 0)
    col = lax.broadcasted_iota(jnp.int32, (blk, blk), 1)
    valid_diag = col < row
    tri = tri_ref[...]

    def visit(j, valid):
        start = pl.multiple_of(j * blk, blk)
        zs, log_fails = [], []
        for g in range(hps):
            q = q_ref[:, g * hd:(g + 1) * hd]
            k = k_ref[pl.ds(start, blk), g * hd:(g + 1) * hd]
            z = lax.dot_general(q, k, (((1,), (1,)), ((), ())), preferred_element_type=jnp.float32)
            log_fail = -(jnp.maximum(z, 0.0) + jnp.log2(1.0 + jnp.exp2(-jnp.abs(z))))
            if valid is not None:
                log_fail = jnp.where(valid, log_fail, 0.0)
            zs.append(z)
            log_fails.append(log_fail)
        sums_all = jnp.dot(jnp.concatenate([lf.astype(jnp.bfloat16) for lf in log_fails], axis=0), tri,
                           preferred_element_type=jnp.float32)
        for g in range(hps):
            v = v_ref[pl.ds(start, blk), g * hd:(g + 1) * hd]
            sums = sums_all[g * blk:(g + 1) * blk]
            c = c_ref[g]
            w = jnp.exp2(zs[g] + log_fails[g] + sums[:, :blk] + c)
            if valid is not None:
                w = jnp.where(valid, w, 0.0)
            acc_ref[g] += jnp.dot(w.astype(v.dtype), v, preferred_element_type=jnp.float32)
            c_ref[g] = c + sums[:, blk:]

    def bound():
        return jnp.max(c_ref[:, :, :hd])

    c_ref[...] = jnp.zeros_like(c_ref)
    acc_ref[...] = jnp.zeros_like(acc_ref)
    visit(qi, valid_diag)

    def cond(state):
        j, worst = state
        return jnp.logical_and(j >= 0, worst > SB_SKIP_LOG2)

    def body(state):
        j, _ = state
        visit(j, None)
        return j - 1, bound()

    lax.while_loop(cond, body, (qi - 1, bound()))
    for g in range(hps):
        o_ref[:, g * hd:(g + 1) * hd] = acc_ref[g].astype(o_ref.dtype)


def _sb_attention(qkv, batch, seq, heads, hd):
    blk = _blk(seq, 256)
    nq = seq // blk
    hps = max(h for h in (4, 2, 1) if heads % h == 0)
    hg, w = heads // hps, hps * hd
    tri = np.concatenate([np.tril(np.ones((blk, blk), np.float32), -1), np.ones((blk, blk), np.float32)], axis=1)
    kern = functools.partial(_sb_kernel, blk=blk, hd=hd, hps=hps)
    return pl.pallas_call(
        kern, out_shape=jax.ShapeDtypeStruct((batch * seq, heads * hd), jnp.bfloat16), grid=(batch, hg, nq),
        in_specs=[pl.BlockSpec((blk, w), lambda b, h, i: (b * nq + i, h)),
                  pl.BlockSpec((seq, w), lambda b, h, i: (b, hg + h)),
                  pl.BlockSpec((seq, w), lambda b, h, i: (b, 2 * hg + h)),
                  pl.BlockSpec((blk, 2 * blk), lambda b, h, i: (0, 0))],
        out_specs=pl.BlockSpec((blk, w), lambda b, h, i: (b * nq + i, h)),
        scratch_shapes=[pltpu.VMEM((hps, blk, blk), jnp.float32), pltpu.VMEM((hps, blk, hd), jnp.float32)],
        compiler_params=_params("parallel", "parallel", "parallel"), name="stick_breaking_attention",
    )(qkv, qkv, qkv, jnp.asarray(tri, jnp.bfloat16))


def _da_kernel(lq1_ref, lk1_ref, lq2_ref, lk2_ref, g_ref, q1_ref, q2_ref, k1_ref, k2_ref, v_ref, o_ref,
               m_ref, l_ref, acc_ref, *, blk, hd, lam_init):
    qi = pl.program_id(2)
    row = lax.broadcasted_iota(jnp.int32, (blk, blk), 0)
    col = lax.broadcasted_iota(jnp.int32, (blk, blk), 1)
    causal_diag = col <= row
    qs = (q1_ref[...], q2_ref[...])
    ks = (k1_ref, k2_ref)

    m_ref[...] = jnp.full_like(m_ref, NEG_INF)
    l_ref[...] = jnp.zeros_like(l_ref)
    acc_ref[...] = jnp.zeros_like(acc_ref)

    def visit(j, mask):
        start = pl.multiple_of(j * blk, blk)
        v = v_ref[pl.ds(start, blk), :]
        probs, corrs = [], []
        for s in range(2):
            k = ks[s][pl.ds(start, blk), :]
            sc = lax.dot_general(qs[s], k, (((1,), (1,)), ((), ())), preferred_element_type=jnp.float32)
            if mask is not None:
                sc = jnp.where(mask, sc, NEG_INF)
            m_old = m_ref[s]
            m_new = jnp.maximum(m_old, jnp.max(sc, axis=-1, keepdims=True))
            p = jnp.exp2(sc - jnp.concatenate([m_new] * (blk // hd), axis=1))
            corr = jnp.exp2(m_old - m_new)
            l_ref[s] = corr * l_ref[s] + jnp.sum(p, axis=-1, keepdims=True)
            m_ref[s] = m_new
            probs.append(p.astype(v.dtype))
            corrs.append(jnp.concatenate([corr, corr], axis=1))
        pv = jnp.dot(jnp.concatenate(probs, axis=0), v, preferred_element_type=jnp.float32)
        for s in range(2):
            acc_ref[s] = corrs[s] * acc_ref[s] + pv[s * blk:(s + 1) * blk]

    def body(j, carry):
        visit(j, None)
        return carry

    lax.fori_loop(0, qi, body, 0)
    visit(qi, causal_diag)

    lam = (jnp.exp(jnp.sum(lq1_ref[...] * lk1_ref[...], axis=-1, keepdims=True))
           - jnp.exp(jnp.sum(lq2_ref[...] * lk2_ref[...], axis=-1, keepdims=True)) + lam_init)
    l1 = jnp.concatenate([l_ref[0], l_ref[0]], axis=1)
    l2 = jnp.concatenate([l_ref[1], l_ref[1]], axis=1)
    o = acc_ref[0] / l1 - lam * (acc_ref[1] / l2)
    o = o * lax.rsqrt(jnp.mean(o * o, axis=-1, keepdims=True) + LN_EPS) * g_ref[...] * (1.0 - lam_init)
    o_ref[...] = o.astype(o_ref.dtype)


def _da_attention(qkv, lambdas, subln_g, batch, seq, heads, hd, col0, lam_init):
    blk = _blk(seq, 512)
    nq = seq // blk
    qb, kb, vb = col0 // hd, col0 // hd + 2 * heads, col0 // (2 * hd) + 2 * heads
    vec = pl.BlockSpec((1, hd), lambda b, h, i: (0, 0))
    kern = functools.partial(_da_kernel, blk=blk, hd=hd, lam_init=lam_init)
    return pl.pallas_call(
        kern, out_shape=jax.ShapeDtypeStruct((batch * seq, heads * 2 * hd), jnp.bfloat16), grid=(batch, heads, nq),
        in_specs=[vec, vec, vec, vec, pl.BlockSpec((1, 2 * hd), lambda b, h, i: (0, 0)),
                  pl.BlockSpec((blk, hd), lambda b, h, i: (b * nq + i, qb + 2 * h)),
                  pl.BlockSpec((blk, hd), lambda b, h, i: (b * nq + i, qb + 2 * h + 1)),
                  pl.BlockSpec((seq, hd), lambda b, h, i: (b, kb + 2 * h)),
                  pl.BlockSpec((seq, hd), lambda b, h, i: (b, kb + 2 * h + 1)),
                  pl.BlockSpec((seq, 2 * hd), lambda b, h, i: (b, vb + h))],
        out_specs=pl.BlockSpec((blk, 2 * hd), lambda b, h, i: (b * nq + i, h)),
        scratch_shapes=[pltpu.VMEM((2, blk, hd), jnp.float32), pltpu.VMEM((2, blk, hd), jnp.float32),
                        pltpu.VMEM((2, blk, 2 * hd), jnp.float32)],
        compiler_params=_params("parallel", "parallel", "parallel"), name="differential_attention",
    )(*[l.reshape(1, hd) for l in lambdas], subln_g.reshape(1, 2 * hd), qkv, qkv, qkv, qkv, qkv)


def _extract_top(x, n):
    rows = lax.broadcasted_iota(jnp.int32, x.shape, 0).astype(jnp.float32)
    tops = []
    for _ in range(n):
        m = jnp.max(x, axis=0, keepdims=True)
        tops.append(m)
        first = jnp.min(jnp.where(x == m, rows, float(x.shape[0])), axis=0, keepdims=True)
        x = jnp.where(rows == first, NEG_INF, x)
    return tops


def _sorting_network(n):
    pairs, p = [], 1
    while p < n:
        k = p
        while k >= 1:
            for j in range(k % p, n - k, 2 * k):
                for i in range(min(k, n - j - k)):
                    if (i + j) // (2 * p) == (i + j + k) // (2 * p):
                        pairs.append((i + j, i + j + k))
            k //= 2
        p *= 2
    return pairs


def _extract_top_sorted(x, n):
    groups = [x[g * SUBLANES:(g + 1) * SUBLANES] for g in range(x.shape[0] // SUBLANES)]
    for a, b in _sorting_network(len(groups)):
        groups[a], groups[b] = jnp.maximum(groups[a], groups[b]), jnp.minimum(groups[a], groups[b])
    lane_list = lax.broadcasted_iota(jnp.int32, groups[0].shape, 0).astype(jnp.float32)
    tops = []
    for t in range(n):
        head = groups[0]
        m = jnp.max(head, axis=0, keepdims=True)
        tops.append(m)
        if t == n - 1:
            break
        first = jnp.min(jnp.where(head == m, lane_list, float(SUBLANES)), axis=0, keepdims=True)
        popped = lane_list == first
        live = min(len(groups), n - t)
        for g in range(live):
            below = groups[g + 1] if g + 1 < len(groups) else NEG_INF
            groups[g] = jnp.where(popped, below, groups[g])
    return tops


def _pair_candidates(top_a, top_b, combine):
    k = len(top_a)
    sub = SUBLANES
    b_all = jnp.concatenate(top_b, axis=0)
    pieces, masks, singles = [], [], []
    for p in range(k):
        n_q = k // (p + 1)
        if n_q == 1:
            singles.append(top_a[p])
            continue
        rows = -(-n_q // sub) * sub
        pieces.append(combine(top_a[p], b_all[0:rows]))
        masks.append(lax.broadcasted_iota(jnp.int32, (rows, 1), 0) < n_q)
    if singles:
        pieces.append(combine(jnp.concatenate(singles, axis=0), top_b[0]))
        masks.append(None)
    return pieces, masks


def _peer_topk_kernel(q_ref, keys_ref, s_ref, e_ref, tau_ref, *, n_keys, topk):
    half = q_ref.shape[1] // 2
    scores, tops = [], []
    for c in range(2):
        s = lax.dot_general(keys_ref[0, c], q_ref[:, c * half:(c + 1) * half], (((1,), (1,)), ((), ())),
                            preferred_element_type=jnp.float32)
        scores.append(s)
        n_groups = n_keys // SUBLANES
        pow2 = n_keys % SUBLANES == 0 and n_groups & (n_groups - 1) == 0
        tops.append(_extract_top_sorted(s, topk) if pow2 else _extract_top(s, topk))
    exps = [[jnp.exp(t - top[0]) for t in top] for top in tops]
    sums, masks = _pair_candidates(tops[0], tops[1], jnp.add)
    prods, _ = _pair_candidates(exps[0], exps[1], jnp.multiply)
    cand = jnp.concatenate([s if m is None else jnp.where(m, s, NEG_INF) for s, m in zip(sums, masks)], axis=0)
    cand_e = jnp.concatenate(prods, axis=0)
    groups = -(-cand.shape[0] // SUBLANES)
    pad_rows = SUBLANES * (1 << (groups - 1).bit_length()) - cand.shape[0]
    padded = jnp.concatenate([cand, jnp.full((pad_rows, cand.shape[1]), NEG_INF, cand.dtype)], axis=0)
    tau = _extract_top_sorted(padded, topk)[-1]
    z = jnp.sum(jnp.where(cand >= tau, cand_e, 0.0), axis=0, keepdims=True)
    s_ref[...] = jnp.concatenate(scores, axis=0)
    e_ref[...] = jnp.concatenate([jnp.exp(scores[0] - tops[0][0]) / z, jnp.exp(scores[1] - tops[1][0])], axis=0)
    tau_ref[0] = tau


def _peer_topk(hq, keys):
    t = hq.shape[0]
    heads, _, n_keys, half = keys.shape
    tn = _blk(t, 512)
    kern = functools.partial(_peer_topk_kernel, n_keys=n_keys, topk=PEER_TOPK)
    st = jax.ShapeDtypeStruct((heads * 2 * n_keys, t), jnp.float32)
    blk = pl.BlockSpec((2 * n_keys, tn), lambda i, h: (h, i))
    return pl.pallas_call(
        kern, out_shape=(st, st, jax.ShapeDtypeStruct((heads, 1, t), jnp.float32)), grid=(t // tn, heads),
        in_specs=[pl.BlockSpec((tn, 2 * half), lambda i, h: (i, h)),
                  pl.BlockSpec((1, 2, n_keys, half), lambda i, h: (h, 0, 0, 0))],
        out_specs=(blk, blk, pl.BlockSpec((1, 1, tn), lambda i, h: (h, 0, i))),
        compiler_params=_params("parallel", "parallel"), name="peer_topk",
    )(hq, keys)


def _peer_dense_kernel(h_ref, u_ref, v_ref, s_ref, e_ref, tau_ref, o_ref, ga_ref, *, heads, n_keys, n_blocks):
    j = pl.program_id(1)

    @pl.when(j == 0)
    def _():
        ga_ref[1] = jnp.zeros(ga_ref.shape[1:], ga_ref.dtype)
        o_ref[...] = jnp.zeros_like(o_ref)

    for parity in range(2):
        pl.when(j % 2 == parity)(functools.partial(
            _peer_dense_step, h_ref, u_ref, v_ref, s_ref, e_ref, tau_ref, o_ref, ga_ref,
            jnp.minimum(j, n_blocks - 1), parity, heads, n_keys))


def _peer_dense_step(h_ref, u_ref, v_ref, s_ref, e_ref, tau_ref, o_ref, ga_ref, jb, slot_now, heads, n_keys):
    tm, te = ga_ref.shape[1:]
    d = o_ref.shape[1]
    n_i = te // n_keys
    mxu = MXU_TILE
    n_tok_tiles = tm // mxu
    rt = min(te, PEER_KEY_ROWS)

    def link_from(x):
        bits = pltpu.bitcast(x[0:8, 0:LANES], jnp.uint32)
        zero = lax.shift_right_logical(lax.shift_right_logical(bits, jnp.uint32(16)), jnp.uint32(16))
        return pltpu.bitcast(zero, jnp.float32)

    pre_tiles = {}

    k_parts = PEER_KEY_SPLIT
    kw = d // k_parts
    done_tiles = set()

    def pre_tile(p, kk):
        mi, ni = divmod(p, n_tok_tiles)
        part = jnp.dot(u_ref[mi * rt:(mi + 1) * rt, kk * kw:(kk + 1) * kw],
                       h_ref[kk * kw:(kk + 1) * kw, ni * mxu:(ni + 1) * mxu], preferred_element_type=jnp.float32)
        pre_tiles[mi, ni] = part if kk == 0 else pre_tiles[mi, ni] + part
        if kk == k_parts - 1:
            done_tiles.add((mi, ni))
        return link_from(part)

    def value_chunk(c):
        cols = slice(c * mxu, (c + 1) * mxu)
        acc = o_ref[:, cols] + jnp.dot(ga_ref[1 - slot_now], v_ref[:, cols], preferred_element_type=jnp.float32)
        o_ref[:, cols] = acc
        return link_from(acc)

    n_pre = (te // rt) * n_tok_tiles
    chunks = [functools.partial(pre_tile, p, kk) for p in range(n_pre) for kk in range(k_parts)]
    chunks += [functools.partial(value_chunk, c) for c in range(d // mxu)]
    results = []

    def finalize(il):
        mi, off = divmod(il * n_keys, rt)
        while any((mi, ni) not in done_tiles for ni in range(n_tok_tiles)):
            results.append(chunks.pop(0)())
        p = jnp.concatenate([pre_tiles[mi, ni][off:off + n_keys, :] for ni in range(n_tok_tiles)], axis=1)
        act = 0.5 * p * (1.0 + lax.erf(p * (2.0 ** -0.5)))
        ga_ref[slot_now, :, il * n_keys:(il + 1) * n_keys] = (gates.pop(il) * act).astype(ga_ref.dtype).T

    pieces = [(il, h) for il in range(n_i) for h in range(heads)]
    costs = [rt * kw * mxu] * (n_pre * k_parts) + [tm * te * mxu] * (d // mxu)
    ends = [sum(costs[:c + 1]) for c in range(len(costs))]
    slot = ends[-1] / len(pieces)
    linked = 0
    gates = {}
    link = link_from(s_ref[0:8, 0:LANES])
    for k, (il, h) in enumerate(pieces):
        while len(results) < len(costs) and ends[len(results)] - costs[len(results)] < (k + 1) * slot:
            results.append(chunks.pop(0)())
        while linked < len(results) and ends[linked] <= k * slot:
            link, linked = link + results[linked], linked + 1
        for _ in range(PEER_LINK_ROLLS):
            link = pltpu.roll(link, 1, 1)
        base = h * 2 * n_keys
        a_row = s_ref[pl.ds(base + jb * n_i + il, 1), :] + jnp.concatenate([link[0:1, :]] * (tm // LANES), axis=1)
        ea_row = e_ref[pl.ds(base + jb * n_i + il, 1), :]
        b_tile = s_ref[base + n_keys:base + 2 * n_keys, :]
        eb_tile = e_ref[base + n_keys:base + 2 * n_keys, :]
        term = jnp.where(a_row + b_tile >= tau_ref[h], ea_row * eb_tile, 0.0)
        gates[il] = term if h == 0 else gates[il] + term
        if h == heads - 1:
            finalize(il)
    while chunks:
        chunks.pop(0)()


def _peer_dense(h_bf_t, u_bf, v_bf, scores_t, exps_t, tau, heads, n_keys):
    d, t = h_bf_t.shape
    e = u_bf.shape[0]
    tm, te = _blk(t, PEER_TOKEN_BLOCK), _blk(e, PEER_EXPERT_BLOCK)
    nb = e // te
    kern = functools.partial(_peer_dense_kernel, heads=heads, n_keys=n_keys, n_blocks=nb)
    rows = scores_t.shape[0]
    once = pl.Buffered(1)
    return pl.pallas_call(
        kern, out_shape=jax.ShapeDtypeStruct((t, d), jnp.float32), grid=(t // tm, nb + 1),
        in_specs=[pl.BlockSpec((d, tm), lambda i, j: (0, i), pipeline_mode=once),
                  pl.BlockSpec((te, d), lambda i, j: (jnp.minimum(j, nb - 1), 0)),
                  pl.BlockSpec((te, d), lambda i, j: (jnp.maximum(j - 1, 0), 0)),
                  pl.BlockSpec((rows, tm), lambda i, j: (0, i), pipeline_mode=once),
                  pl.BlockSpec((rows, tm), lambda i, j: (0, i), pipeline_mode=once),
                  pl.BlockSpec((heads, 1, tm), lambda i, j: (0, 0, i))],
        out_specs=pl.BlockSpec((tm, d), lambda i, j: (i, 0)),
        scratch_shapes=[pltpu.VMEM((2, tm, te), jnp.bfloat16)],
        compiler_params=_params("parallel", "arbitrary"), name="peer_dense",
    )(h_bf_t, u_bf, v_bf, scores_t, exps_t, tau)


def kernel(x, positions, w_in, b_gate, lambda_q1, lambda_k1, lambda_q2, lambda_k2, subln_g, w_sb_branch,
           w_da_branch, w_out, ln1_g, ln1_b, peer_w_q, peer_sub_keys, peer_u, peer_v, ln2_g, ln2_b):
    batch, seq, d = x.shape
    depth = w_in.shape[0]
    hd = lambda_q1.shape[-1]
    sb_w, da_w = w_sb_branch.shape[1], w_da_branch.shape[1]
    sb_heads, da_heads = sb_w // hd, da_w // (2 * hd)
    n_qkv = 3 * sb_w + 3 * da_w
    alpha = (2 * depth) ** 0.25
    bf = jnp.bfloat16
    t = batch * seq

    cos, sin = _rope_tables(positions, hd)
    col_scale = np.ones((1, n_qkv), np.float32)
    col_scale[:, :sb_w] = hd ** -0.5 * math.log2(math.e)
    col_scale[:, 3 * sb_w:3 * sb_w + da_w] = hd ** -0.5 * math.log2(math.e)
    col_scale = jnp.asarray(col_scale)
    h = x.reshape(t, d)
    h_bf = _to_bf16(h)
    for l in range(depth):
        lam_init = 0.8 - 0.6 * math.exp(-0.3 * l)
        w_in_bf = _to_bf16(w_in[l])
        qkv, w_sb_bf, w_da_bf, w_out_bf, w_q_bf = _qkv_proj(
            h_bf, w_in_bf, col_scale, cos, sin, n_qkv, (3 * sb_w, 3 * sb_w + 2 * da_w), hd,
            (w_sb_branch[l], w_da_branch[l], w_out[l], peer_w_q[l]))
        gate, u_bf = _gate_proj(h_bf, w_in_bf, b_gate[l].reshape(1, -1), n_qkv, 2 * d, (peer_u[l],))
        o_sb = _sb_attention(qkv, batch, seq, sb_heads, hd)
        o_da = _da_attention(qkv, (lambda_q1[l], lambda_k1[l], lambda_q2[l], lambda_k2[l]), subln_g[l],
                             batch, seq, da_heads, hd, 3 * sb_w, lam_init)
        merged, v_bf = _merge(o_sb, o_da, w_sb_bf, w_da_bf, gate, (peer_v[l],))
        mix = _matmul(merged, w_out_bf, jnp.float32, "out_proj")
        h, h_bf_t, hq = _residual_ln_peer_query(h, mix, ln1_g[l], ln1_b[l], alpha, w_q_bf)

        heads, _, n_keys, half = peer_sub_keys[l].shape
        scores_t, exps_t, tau = _peer_topk(hq, peer_sub_keys[l].astype(bf))
        ffn = _peer_dense(h_bf_t, u_bf, v_bf, scores_t, exps_t, tau, heads, n_keys)
        outs = _residual_ln(h, ffn, ln2_g[l], ln2_b[l], alpha, l + 1 < depth)
        h = outs[0]
        if l + 1 < depth:
            h_bf = outs[1]
    return h.reshape(batch, seq, d)
```

```python
import functools
import math

import numpy as np
import jax
import jax.numpy as jnp
from jax import lax
from jax.experimental import pallas as pl
from jax.experimental.pallas import tpu as pltpu

LN_EPS = 1e-5
ROPE_THETA = 10000.0
PEER_TOPK = 16
V7X_VMEM_LIMIT_BYTES = 56 * 1024 * 1024
PEER_VMEM_LIMIT_BYTES = 60 * 1024 * 1024
LANES = 128
SUBLANES = 8
MXU_TILE = 256
PEER_TOKEN_BLOCK = 512
PEER_EXPERT_BLOCK = 512
PEER_KEY_ROWS = 256
PEER_KEY_SPLIT = 4
PEER_LINK_ROLLS = 1
NEG_INF = float("-inf")


def _params(*sem):
    return pltpu.CompilerParams(dimension_semantics=sem, vmem_limit_bytes=V7X_VMEM_LIMIT_BYTES)


def _blk(n, pref):
    b = min(n, pref)
    while n % b:
        b //= 2
    return b


def _cast_kernel(x_ref, o_ref):
    o_ref[...] = x_ref[...].astype(o_ref.dtype)


def _to_bf16(a):
    m, n = a.shape
    rows = max(2 * SUBLANES, (8 * 1024 * 1024) // (4 * n))
    bm = _blk(m, 1 << (rows.bit_length() - 1))
    spec = pl.BlockSpec((bm, n), lambda i: (i, 0))
    return pl.pallas_call(
        _cast_kernel, out_shape=jax.ShapeDtypeStruct((m, n), jnp.bfloat16), grid=(m // bm,),
        in_specs=[spec], out_specs=spec, compiler_params=_params("parallel"), name="to_bf16",
    )(a)


def _side_cast_plan(arrays, grid):
    steps, nj = grid[0] * grid[1], grid[1]
    specs, shapes, blocks = [], [], []
    for a in arrays:
        r, c = a.shape
        nb = 1
        while nb * 2 <= steps and r % (nb * 2) == 0 and (r // (nb * 2)) % (2 * SUBLANES) == 0:
            nb *= 2
        specs.append(pl.BlockSpec((r // nb, c), lambda i, j, nb=nb: (jnp.minimum(i * nj + j, nb - 1), 0)))
        shapes.append(jax.ShapeDtypeStruct((r, c), jnp.bfloat16))
        blocks.append(nb)
    return specs, shapes, tuple(blocks)


def _side_casts(src_refs, dst_refs, side_blocks):
    step = pl.program_id(0) * pl.num_programs(1) + pl.program_id(1)
    for src, dst, nb in zip(src_refs, dst_refs, side_blocks):
        @pl.when(step < nb)
        def _(src=src, dst=dst):
            dst[...] = src[...].astype(dst.dtype)


def _rope_kernel(pos_ref, freq_ref, cos_ref, sin_ref):
    ang = pos_ref[...].astype(jnp.float32) * freq_ref[...]
    hd = freq_ref.shape[1]
    lane = lax.broadcasted_iota(jnp.int32, ang.shape, 1)
    cos_ref[...] = jnp.cos(ang)
    s = jnp.sin(ang)
    sin_ref[...] = jnp.where(lane < hd // 2, -s, s)


def _rope_tables(positions, hd):
    t = positions.size
    bm = _blk(t, 2048)
    inv = ROPE_THETA ** (-np.arange(0, hd, 2, dtype=np.float32) / np.float32(hd))
    freq = jnp.asarray(np.concatenate([inv, inv]).astype(np.float32).reshape(1, hd))
    out = jax.ShapeDtypeStruct((t, hd), jnp.float32)
    return pl.pallas_call(
        _rope_kernel, out_shape=(out, out), grid=(t // bm,),
        in_specs=[pl.BlockSpec((bm, 1), lambda i: (i, 0)), pl.BlockSpec((1, hd), lambda i: (0, 0))],
        out_specs=(pl.BlockSpec((bm, hd), lambda i: (i, 0)), pl.BlockSpec((bm, hd), lambda i: (i, 0))),
        compiler_params=_params("parallel"), name="rope_tables",
    )(positions.reshape(t, 1), freq)


def _qkv_kernel(x_ref, w_ref, cs_ref, cos_ref, sin_ref, *refs, rope_lo, rope_hi, hd, side_blocks):
    n = len(side_blocks)
    o_ref = refs[n]
    _side_casts(refs[:n], refs[n + 1:], side_blocks)
    acc = jnp.dot(x_ref[...], w_ref[...], preferred_element_type=jnp.float32) * cs_ref[...]
    j = pl.program_id(1)
    is_rope = jnp.logical_and(j >= rope_lo, j < rope_hi)

    @pl.when(is_rope)
    def _():
        c = cos_ref[...]
        s = sin_ref[...]
        for t in range(acc.shape[1] // hd):
            a = acc[:, t * hd:(t + 1) * hd]
            o_ref[:, t * hd:(t + 1) * hd] = (a * c + pltpu.roll(a, hd // 2, 1) * s).astype(o_ref.dtype)

    @pl.when(jnp.logical_not(is_rope))
    def _():
        o_ref[...] = acc.astype(o_ref.dtype)


def _qkv_proj(x, w, col_scale, cos, sin, n_out, rope_cols, hd, side):
    m, k = x.shape
    bm, bn = _blk(m, 1024), _blk(math.gcd(n_out, rope_cols[0], rope_cols[1]), 1024)
    grid = (m // bm, n_out // bn)
    side_specs, side_shapes, side_blocks = _side_cast_plan(side, grid)
    kern = functools.partial(_qkv_kernel, rope_lo=rope_cols[0] // bn, rope_hi=rope_cols[1] // bn, hd=hd,
                             side_blocks=side_blocks)
    return pl.pallas_call(
        kern, out_shape=(jax.ShapeDtypeStruct((m, n_out), jnp.bfloat16), *side_shapes), grid=grid,
        in_specs=[pl.BlockSpec((bm, k), lambda i, j: (i, 0)), pl.BlockSpec((k, bn), lambda i, j: (0, j)),
                  pl.BlockSpec((1, bn), lambda i, j: (0, j)),
                  pl.BlockSpec((bm, hd), lambda i, j: (i, 0)), pl.BlockSpec((bm, hd), lambda i, j: (i, 0)),
                  *side_specs],
        out_specs=(pl.BlockSpec((bm, bn), lambda i, j: (i, j)), *side_specs),
        compiler_params=_params("arbitrary", "arbitrary"), name="qkv_proj",
    )(x, w, col_scale, cos, sin, *side)


def _gate_kernel(x_ref, w_ref, b_ref, *refs, side_blocks):
    n = len(side_blocks)
    o_ref = refs[n]
    acc = jnp.dot(x_ref[...], w_ref[...], preferred_element_type=jnp.float32)
    o_ref[...] = jax.nn.sigmoid(acc + b_ref[...]).astype(o_ref.dtype)
    _side_casts(refs[:n], refs[n + 1:], side_blocks)


def _gate_proj(x, w, bias, col0, n_out, side):
    m, k = x.shape
    bm, bn = _blk(m, 1024), _blk(math.gcd(n_out, col0), 1024)
    off = col0 // bn
    grid = (m // bm, n_out // bn)
    side_specs, side_shapes, side_blocks = _side_cast_plan(side, grid)
    return pl.pallas_call(
        functools.partial(_gate_kernel, side_blocks=side_blocks),
        out_shape=(jax.ShapeDtypeStruct((m, n_out), jnp.float32), *side_shapes), grid=grid,
        in_specs=[pl.BlockSpec((bm, k), lambda i, j: (i, 0)), pl.BlockSpec((k, bn), lambda i, j: (0, j + off)),
                  pl.BlockSpec((1, bn), lambda i, j: (0, j)), *side_specs],
        out_specs=(pl.BlockSpec((bm, bn), lambda i, j: (i, j)), *side_specs),
        compiler_params=_params("arbitrary", "arbitrary"), name="gate_proj",
    )(x, w, bias, *side)


def _mm_kernel(a_ref, b_ref, o_ref):
    o_ref[...] = jnp.dot(a_ref[...], b_ref[...], preferred_element_type=jnp.float32).astype(o_ref.dtype)


def _matmul(a, b, out_dtype, name):
    m, k = a.shape
    n = b.shape[1]
    bm, bn = _blk(m, 1024), _blk(n, 1024)
    return pl.pallas_call(
        _mm_kernel, out_shape=jax.ShapeDtypeStruct((m, n), out_dtype), grid=(m // bm, n // bn),
        in_specs=[pl.BlockSpec((bm, k), lambda i, j: (i, 0)), pl.BlockSpec((k, bn), lambda i, j: (0, j))],
        out_specs=pl.BlockSpec((bm, bn), lambda i, j: (i, j)),
        compiler_params=_params("parallel", "parallel"), name=name,
    )(a, b)


def _merge_kernel(osb_ref, oda_ref, wsb_ref, wda_ref, gsb_ref, gda_ref, *refs, side_blocks):
    n = len(side_blocks)
    o_ref = refs[n]
    ysb = jnp.dot(osb_ref[...], wsb_ref[...], preferred_element_type=jnp.float32)
    yda = jnp.dot(oda_ref[...], wda_ref[...], preferred_element_type=jnp.float32)
    o_ref[...] = (gsb_ref[...] * ysb + gda_ref[...] * yda).astype(o_ref.dtype)
    _side_casts(refs[:n], refs[n + 1:], side_blocks)


def _merge(o_sb, o_da, w_sb, w_da, gate, side):
    m = o_sb.shape[0]
    d = w_sb.shape[1]
    bm, bn = _blk(m, 1024), _blk(d, 512)
    goff = d // bn
    grid = (m // bm, d // bn)
    side_specs, side_shapes, side_blocks = _side_cast_plan(side, grid)
    return pl.pallas_call(
        functools.partial(_merge_kernel, side_blocks=side_blocks),
        out_shape=(jax.ShapeDtypeStruct((m, d), jnp.bfloat16), *side_shapes), grid=grid,
        in_specs=[pl.BlockSpec((bm, o_sb.shape[1]), lambda i, j: (i, 0)),
                  pl.BlockSpec((bm, o_da.shape[1]), lambda i, j: (i, 0)),
                  pl.BlockSpec((w_sb.shape[0], bn), lambda i, j: (0, j)),
                  pl.BlockSpec((w_da.shape[0], bn), lambda i, j: (0, j)),
                  pl.BlockSpec((bm, bn), lambda i, j: (i, j)),
                  pl.BlockSpec((bm, bn), lambda i, j: (i, j + goff)), *side_specs],
        out_specs=(pl.BlockSpec((bm, bn), lambda i, j: (i, j)), *side_specs),
        compiler_params=_params("arbitrary", "arbitrary"), name="branch_merge",
    )(o_sb, o_da, w_sb, w_da, gate, gate, *side)


def _ln_rows(res_ref, y_ref, g_ref, b_ref, alpha):
    z = alpha * res_ref[...] + y_ref[...]
    mu = jnp.mean(z, axis=-1, keepdims=True)
    zc = z - mu
    var = jnp.mean(zc * zc, axis=-1, keepdims=True)
    return zc * lax.rsqrt(var + LN_EPS) * g_ref[...] + b_ref[...]


def _ln_peer_kernel(res_ref, y_ref, g_ref, b_ref, wq_ref, o_ref, low_t_ref, hq_ref, *, alpha):
    out = _ln_rows(res_ref, y_ref, g_ref, b_ref, alpha)
    o_ref[...] = out
    low = out.astype(low_t_ref.dtype)
    low_t_ref[...] = low.T
    hq_ref[...] = jnp.dot(low, wq_ref[...], preferred_element_type=jnp.float32).astype(hq_ref.dtype)


def _residual_ln_peer_query(res, y, g, b, alpha, w_q):
    m, d = res.shape
    nq = w_q.shape[1]
    bm = _blk(m, 256)
    row = pl.BlockSpec((bm, d), lambda i: (i, 0))
    vec = pl.BlockSpec((1, d), lambda i: (0, 0))
    return pl.pallas_call(
        functools.partial(_ln_peer_kernel, alpha=alpha),
        out_shape=(jax.ShapeDtypeStruct((m, d), jnp.float32), jax.ShapeDtypeStruct((d, m), jnp.bfloat16),
                   jax.ShapeDtypeStruct((m, nq), jnp.bfloat16)),
        grid=(m // bm,),
        in_specs=[row, row, vec, vec, pl.BlockSpec((d, nq), lambda i: (0, 0), pipeline_mode=pl.Buffered(1))],
        out_specs=(row, pl.BlockSpec((d, bm), lambda i: (0, i)), pl.BlockSpec((bm, nq), lambda i: (i, 0))),
        compiler_params=_params("parallel"), name="layernorm_peer_query",
    )(res, y, g.reshape(1, d), b.reshape(1, d), w_q)


SB_SKIP_LOG2 = -100.0 * math.log2(math.e)


def _sb_kernel(q_ref, k_ref, v_ref, tri_ref, o_ref, c_ref, acc_ref, *, blk, hd, hps):
    qi = pl.program_id(2)
    row = lax.broadcasted_iota(jnp.int32, (blk, blk), 0)
    col = lax.broadcasted_iota(jnp.int32, (blk, blk), 1)
    valid_diag = col < row
    tri = tri_ref[...]

    def visit(j, valid):
        start = pl.multiple_of(j * blk, blk)
        zs, log_fails = [], []
        for g in range(hps):
            q = q_ref[:, g * hd:(g + 1) * hd]
            k = k_ref[pl.ds(start, blk), g * hd:(g + 1) * hd]
            z = lax.dot_general(q, k, (((1,), (1,)), ((), ())), preferred_element_type=jnp.float32)
            log_fail = -(jnp.maximum(z, 0.0) + jnp.log2(1.0 + jnp.exp2(-jnp.abs(z))))
            if valid is not None:
                log_fail = jnp.where(valid, log_fail, 0.0)
            zs.append(z)
            log_fails.append(log_fail)
        sums_all = jnp.dot(jnp.concatenate([lf.astype(jnp.bfloat16) for lf in log_fails], axis=0), tri,
                           preferred_element_type=jnp.float32)
        for g in range(hps):
            v = v_ref[pl.ds(start, blk), g * hd:(g + 1) * hd]
            sums = sums_all[g * blk:(g + 1) * blk]
            c = c_ref[g]
            w = jnp.exp2(zs[g] + log_fails[g] + sums[:, :blk] + c)
            if valid is not None:
                w = jnp.where(valid, w, 0.0)
            acc_ref[g] += jnp.dot(w.astype(v.dtype), v, preferred_element_type=jnp.float32)
            c_ref[g] = c + sums[:, blk:]

    def bound():
        return jnp.max(c_ref[:, :, :hd])

    c_ref[...] = jnp.zeros_like(c_ref)
    acc_ref[...] = jnp.zeros_like(acc_ref)
    visit(qi, valid_diag)

    def cond(state):
        j, worst = state
        return jnp.logical_and(j >= 0, worst > SB_SKIP_LOG2)

    def body(state):
        j, _ = state
        visit(j, None)
        return j - 1, bound()

    lax.while_loop(cond, body, (qi - 1, bound()))
    for g in range(hps):
        o_ref[:, g * hd:(g + 1) * hd] = acc_ref[g].astype(o_ref.dtype)


def _sb_attention(qkv, batch, seq, heads, hd):
    blk = _blk(seq, 256)
    nq = seq // blk
    hps = max(h for h in (4, 2, 1) if heads % h == 0)
    hg, w = heads // hps, hps * hd
    tri = np.concatenate([np.tril(np.ones((blk, blk), np.float32), -1), np.ones((blk, blk), np.float32)], axis=1)
    kern = functools.partial(_sb_kernel, blk=blk, hd=hd, hps=hps)
    return pl.pallas_call(
        kern, out_shape=jax.ShapeDtypeStruct((batch * seq, heads * hd), jnp.bfloat16), grid=(batch, hg, nq),
        in_specs=[pl.BlockSpec((blk, w), lambda b, h, i: (b * nq + i, h)),
                  pl.BlockSpec((seq, w), lambda b, h, i: (b, hg + h)),
                  pl.BlockSpec((seq, w), lambda b, h, i: (b, 2 * hg + h)),
                  pl.BlockSpec((blk, 2 * blk), lambda b, h, i: (0, 0))],
        out_specs=pl.BlockSpec((blk, w), lambda b, h, i: (b * nq + i, h)),
        scratch_shapes=[pltpu.VMEM((hps, blk, blk), jnp.float32), pltpu.VMEM((hps, blk, hd), jnp.float32)],
        compiler_params=_params("parallel", "parallel", "parallel"), name="stick_breaking_attention",
    )(qkv, qkv, qkv, jnp.asarray(tri, jnp.bfloat16))


def _da_kernel(lq1_ref, lk1_ref, lq2_ref, lk2_ref, g_ref, q1_ref, q2_ref, k1_ref, k2_ref, v_ref, o_ref,
               m_ref, l_ref, acc_ref, *, blk, hd, lam_init):
    qi = pl.program_id(2)
    row = lax.broadcasted_iota(jnp.int32, (blk, blk), 0)
    col = lax.broadcasted_iota(jnp.int32, (blk, blk), 1)
    causal_diag = col <= row
    qs = (q1_ref[...], q2_ref[...])
    ks = (k1_ref, k2_ref)

    m_ref[...] = jnp.full_like(m_ref, NEG_INF)
    l_ref[...] = jnp.zeros_like(l_ref)
    acc_ref[...] = jnp.zeros_like(acc_ref)

    def visit(j, mask):
        start = pl.multiple_of(j * blk, blk)
        v = v_ref[pl.ds(start, blk), :]
        probs, corrs = [], []
        for s in range(2):
            k = ks[s][pl.ds(start, blk), :]
            sc = lax.dot_general(qs[s], k, (((1,), (1,)), ((), ())), preferred_element_type=jnp.float32)
            if mask is not None:
                sc = jnp.where(mask, sc, NEG_INF)
            m_old = m_ref[s]
            m_new = jnp.maximum(m_old, jnp.max(sc, axis=-1, keepdims=True))
            p = jnp.exp2(sc - jnp.concatenate([m_new] * (blk // hd), axis=1))
            corr = jnp.exp2(m_old - m_new)
            l_ref[s] = corr * l_ref[s] + jnp.sum(p, axis=-1, keepdims=True)
            m_ref[s] = m_new
            probs.append(p.astype(v.dtype))
            corrs.append(jnp.concatenate([corr, corr], axis=1))
        pv = jnp.dot(jnp.concatenate(probs, axis=0), v, preferred_element_type=jnp.float32)
        for s in range(2):
            acc_ref[s] = corrs[s] * acc_ref[s] + pv[s * blk:(s + 1) * blk]

    def body(j, carry):
        visit(j, None)
        return carry

    lax.fori_loop(0, qi, body, 0)
    visit(qi, causal_diag)

    lam = (jnp.exp(jnp.sum(lq1_ref[...] * lk1_ref[...], axis=-1, keepdims=True))
           - jnp.exp(jnp.sum(lq2_ref[...] * lk2_ref[...], axis=-1, keepdims=True)) + lam_init)
    l1 = jnp.concatenate([l_ref[0], l_ref[0]], axis=1)
    l2 = jnp.concatenate([l_ref[1], l_ref[1]], axis=1)
    o = acc_ref[0] / l1 - lam * (acc_ref[1] / l2)
    o = o * lax.rsqrt(jnp.mean(o * o, axis=-1, keepdims=True) + LN_EPS) * g_ref[...] * (1.0 - lam_init)
    o_ref[...] = o.astype(o_ref.dtype)


def _da_attention(qkv, lambdas, subln_g, batch, seq, heads, hd, col0, lam_init):
    blk = _blk(seq, 512)
    nq = seq // blk
    qb, kb, vb = col0 // hd, col0 // hd + 2 * heads, col0 // (2 * hd) + 2 * heads
    vec = pl.BlockSpec((1, hd), lambda b, h, i: (0, 0))
    kern = functools.partial(_da_kernel, blk=blk, hd=hd, lam_init=lam_init)
    return pl.pallas_call(
        kern, out_shape=jax.ShapeDtypeStruct((batch * seq, heads * 2 * hd), jnp.bfloat16), grid=(batch, heads, nq),
        in_specs=[vec, vec, vec, vec, pl.BlockSpec((1, 2 * hd), lambda b, h, i: (0, 0)),
                  pl.BlockSpec((blk, hd), lambda b, h, i: (b * nq + i, qb + 2 * h)),
                  pl.BlockSpec((blk, hd), lambda b, h, i: (b * nq + i, qb + 2 * h + 1)),
                  pl.BlockSpec((seq, hd), lambda b, h, i: (b, kb + 2 * h)),
                  pl.BlockSpec((seq, hd), lambda b, h, i: (b, kb + 2 * h + 1)),
                  pl.BlockSpec((seq, 2 * hd), lambda b, h, i: (b, vb + h))],
        out_specs=pl.BlockSpec((blk, 2 * hd), lambda b, h, i: (b * nq + i, h)),
        scratch_shapes=[pltpu.VMEM((2, blk, hd), jnp.float32), pltpu.VMEM((2, blk, hd), jnp.float32),
                        pltpu.VMEM((2, blk, 2 * hd), jnp.float32)],
        compiler_params=_params("parallel", "parallel", "parallel"), name="differential_attention",
    )(*[l.reshape(1, hd) for l in lambdas], subln_g.reshape(1, 2 * hd), qkv, qkv, qkv, qkv, qkv)


def _extract_top(x, n):
    rows = lax.broadcasted_iota(jnp.int32, x.shape, 0).astype(jnp.float32)
    tops = []
    for _ in range(n):
        m = jnp.max(x, axis=0, keepdims=True)
        tops.append(m)
        first = jnp.min(jnp.where(x == m, rows, float(x.shape[0])), axis=0, keepdims=True)
        x = jnp.where(rows == first, NEG_INF, x)
    return tops


def _sorting_network(n):
    pairs, p = [], 1
    while p < n:
        k = p
        while k >= 1:
            for j in range(k % p, n - k, 2 * k):
                for i in range(min(k, n - j - k)):
                    if (i + j) // (2 * p) == (i + j + k) // (2 * p):
                        pairs.append((i + j, i + j + k))
            k //= 2
        p *= 2
    return pairs


def _extract_top_sorted(x, n):
    groups = [x[g * SUBLANES:(g + 1) * SUBLANES] for g in range(x.shape[0] // SUBLANES)]
    for a, b in _sorting_network(len(groups)):
        groups[a], groups[b] = jnp.maximum(groups[a], groups[b]), jnp.minimum(groups[a], groups[b])
    lane_list = lax.broadcasted_iota(jnp.int32, groups[0].shape, 0).astype(jnp.float32)
    tops = []
    for t in range(n):
        head = groups[0]
        m = jnp.max(head, axis=0, keepdims=True)
        tops.append(m)
        if t == n - 1:
            break
        first = jnp.min(jnp.where(head == m, lane_list, float(SUBLANES)), axis=0, keepdims=True)
        popped = lane_list == first
        live = min(len(groups), n - t)
        for g in range(live):
            below = groups[g + 1] if g + 1 < len(groups) else NEG_INF
            groups[g] = jnp.where(popped, below, groups[g])
    return tops


def _pair_candidates(top_a, top_b, combine):
    k = len(top_a)
    sub = SUBLANES
    b_all = jnp.concatenate(top_b, axis=0)
    pieces, masks, singles = [], [], []
    for p in range(k):
        n_q = k // (p + 1)
        if n_q == 1:
            singles.append(top_a[p])
            continue
        rows = -(-n_q // sub) * sub
        pieces.append(combine(top_a[p], b_all[0:rows]))
        masks.append(lax.broadcasted_iota(jnp.int32, (rows, 1), 0) < n_q)
    if singles:
        pieces.append(combine(jnp.concatenate(singles, axis=0), top_b[0]))
        masks.append(None)
    return pieces, masks


def _peer_topk_kernel(q_ref, keys_ref, s_ref, e_ref, tau_ref, *, n_keys, topk):
    half = q_ref.shape[1] // 2
    scores, tops = [], []
    for c in range(2):
        s = lax.dot_general(keys_ref[0, c], q_ref[:, c * half:(c + 1) * half], (((1,), (1,)), ((), ())),
                            preferred_element_type=jnp.float32)
        scores.append(s)
        n_groups = n_keys // SUBLANES
        pow2 = n_keys % SUBLANES == 0 and n_groups & (n_groups - 1) == 0
        tops.append(_extract_top_sorted(s, topk) if pow2 else _extract_top(s, topk))
    exps = [[jnp.exp(t - top[0]) for t in top] for top in tops]
    sums, masks = _pair_candidates(tops[0], tops[1], jnp.add)
    prods, _ = _pair_candidates(exps[0], exps[1], jnp.multiply)
    cand = jnp.concatenate([s if m is None else jnp.where(m, s, NEG_INF) for s, m in zip(sums, masks)], axis=0)
    cand_e = jnp.concatenate(prods, axis=0)
    groups = -(-cand.shape[0] // SUBLANES)
    pad_rows = SUBLANES * (1 << (groups - 1).bit_length()) - cand.shape[0]
    padded = jnp.concatenate([cand, jnp.full((pad_rows, cand.shape[1]), NEG_INF, cand.dtype)], axis=0)
    tau = _extract_top_sorted(padded, topk)[-1]
    z = jnp.sum(jnp.where(cand >= tau, cand_e, 0.0), axis=0, keepdims=True)
    s_ref[...] = jnp.concatenate(scores, axis=0)
    e_ref[...] = jnp.concatenate([jnp.exp(scores[0] - tops[0][0]) / z, jnp.exp(scores[1] - tops[1][0])], axis=0)
    tau_ref[0] = tau


def _peer_topk(hq, keys):
    t = hq.shape[0]
    heads, _, n_keys, half = keys.shape
    tn = _blk(t, 512)
    kern = functools.partial(_peer_topk_kernel, n_keys=n_keys, topk=PEER_TOPK)
    st = jax.ShapeDtypeStruct((heads * 2 * n_keys, t), jnp.float32)
    blk = pl.BlockSpec((2 * n_keys, tn), lambda i, h: (h, i))
    return pl.pallas_call(
        kern, out_shape=(st, st, jax.ShapeDtypeStruct((heads, 1, t), jnp.float32)), grid=(t // tn, heads),
        in_specs=[pl.BlockSpec((tn, 2 * half), lambda i, h: (i, h)),
                  pl.BlockSpec((1, 2, n_keys, half), lambda i, h: (h, 0, 0, 0))],
        out_specs=(blk, blk, pl.BlockSpec((1, 1, tn), lambda i, h: (h, 0, i))),
        compiler_params=_params("parallel", "parallel"), name="peer_topk",
    )(hq, keys)


def _peer_dense_kernel(h_ref, u_ref, v_ref, s_ref, e_ref, tau_ref, res_ref, g_ref, b_ref, o_ref, ga_ref, *,
                       heads, n_keys, n_blocks, alpha):
    j = pl.program_id(1)

    @pl.when(j == 0)
    def _():
        ga_ref[1] = jnp.zeros(ga_ref.shape[1:], ga_ref.dtype)
        o_ref[...] = jnp.zeros_like(o_ref)

    for parity in range(2):
        pl.when(j % 2 == parity)(functools.partial(
            _peer_dense_step, h_ref, u_ref, v_ref, s_ref, e_ref, tau_ref, o_ref, ga_ref,
            jnp.minimum(j, n_blocks - 1), parity, heads, n_keys))

    @pl.when(j == n_blocks)
    def _():
        rows = min(o_ref.shape[0], LANES)
        for r in range(o_ref.shape[0] // rows):
            sl = slice(r * rows, (r + 1) * rows)
            o_ref[sl] = _ln_rows(res_ref.at[sl], o_ref.at[sl], g_ref, b_ref, alpha)


def _peer_dense_step(h_ref, u_ref, v_ref, s_ref, e_ref, tau_ref, o_ref, ga_ref, jb, slot_now, heads, n_keys):
    tm, te = ga_ref.shape[1:]
    d = o_ref.shape[1]
    n_i = te // n_keys
    mxu = MXU_TILE
    n_tok_tiles = tm // mxu
    rt = min(te, PEER_KEY_ROWS)

    def link_from(x):
        bits = pltpu.bitcast(x[0:8, 0:LANES], jnp.uint32)
        zero = lax.shift_right_logical(lax.shift_right_logical(bits, jnp.uint32(16)), jnp.uint32(16))
        return pltpu.bitcast(zero, jnp.float32)

    pre_tiles = {}

    k_parts = PEER_KEY_SPLIT
    kw = d // k_parts
    done_tiles = set()

    def pre_tile(p, kk):
        mi, ni = divmod(p, n_tok_tiles)
        part = jnp.dot(u_ref[mi * rt:(mi + 1) * rt, kk * kw:(kk + 1) * kw],
                       h_ref[kk * kw:(kk + 1) * kw, ni * mxu:(ni + 1) * mxu], preferred_element_type=jnp.float32)
        pre_tiles[mi, ni] = part if kk == 0 else pre_tiles[mi, ni] + part
        if kk == k_parts - 1:
            done_tiles.add((mi, ni))
        return link_from(part)

    def value_chunk(c):
        cols = slice(c * mxu, (c + 1) * mxu)
        acc = o_ref[:, cols] + jnp.dot(ga_ref[1 - slot_now], v_ref[:, cols], preferred_element_type=jnp.float32)
        o_ref[:, cols] = acc
        return link_from(acc)

    n_pre = (te // rt) * n_tok_tiles
    chunks = [functools.partial(pre_tile, p, kk) for p in range(n_pre) for kk in range(k_parts)]
    chunks += [functools.partial(value_chunk, c) for c in range(d // mxu)]
    results = []

    def finalize(il):
        mi, off = divmod(il * n_keys, rt)
        while any((mi, ni) not in done_tiles for ni in range(n_tok_tiles)):
            results.append(chunks.pop(0)())
        p = jnp.concatenate([pre_tiles[mi, ni][off:off + n_keys, :] for ni in range(n_tok_tiles)], axis=1)
        act = 0.5 * p * (1.0 + lax.erf(p * (2.0 ** -0.5)))
        ga_ref[slot_now, :, il * n_keys:(il + 1) * n_keys] = (gates.pop(il) * act).astype(ga_ref.dtype).T

    pieces = [(il, h) for il in range(n_i) for h in range(heads)]
    costs = [rt * kw * mxu] * (n_pre * k_parts) + [tm * te * mxu] * (d // mxu)
    ends = [sum(costs[:c + 1]) for c in range(len(costs))]
    slot = ends[-1] / len(pieces)
    linked = 0
    gates = {}
    link = link_from(s_ref[0:8, 0:LANES])
    for k, (il, h) in enumerate(pieces):
        while len(results) < len(costs) and ends[len(results)] - costs[len(results)] < (k + 1) * slot:
            results.append(chunks.pop(0)())
        while linked < len(results) and ends[linked] <= k * slot:
            link, linked = link + results[linked], linked + 1
        for _ in range(PEER_LINK_ROLLS):
            link = pltpu.roll(link, 1, 1)
        base = h * 2 * n_keys
        a_row = s_ref[pl.ds(base + jb * n_i + il, 1), :] + jnp.concatenate([link[0:1, :]] * (tm // LANES), axis=1)
        ea_row = e_ref[pl.ds(base + jb * n_i + il, 1), :]
        b_tile = s_ref[base + n_keys:base + 2 * n_keys, :]
        eb_tile = e_ref[base + n_keys:base + 2 * n_keys, :]
        term = jnp.where(a_row + b_tile >= tau_ref[h], ea_row * eb_tile, 0.0)
        gates[il] = term if h == 0 else gates[il] + term
        if h == heads - 1:
            finalize(il)
    while chunks:
        chunks.pop(0)()


def _peer_dense_ln(h, h_bf_t, u_bf, v_bf, scores_t, exps_t, tau, g, b, alpha, heads, n_keys):
    d, t = h_bf_t.shape
    e = u_bf.shape[0]
    tm, te = _blk(t, PEER_TOKEN_BLOCK), _blk(e, PEER_EXPERT_BLOCK)
    nb = e // te
    kern = functools.partial(_peer_dense_kernel, heads=heads, n_keys=n_keys, n_blocks=nb, alpha=alpha)
    rows = scores_t.shape[0]
    once = pl.Buffered(1)
    vec = pl.BlockSpec((1, d), lambda i, j: (0, 0))
    return pl.pallas_call(
        kern, out_shape=jax.ShapeDtypeStruct((t, d), jnp.float32), grid=(t // tm, nb + 1),
        in_specs=[pl.BlockSpec((d, tm), lambda i, j: (0, i), pipeline_mode=once),
                  pl.BlockSpec((te, d), lambda i, j: (jnp.minimum(j, nb - 1), 0)),
                  pl.BlockSpec((te, d), lambda i, j: (jnp.maximum(j - 1, 0), 0)),
                  pl.BlockSpec((rows, tm), lambda i, j: (0, i), pipeline_mode=once),
                  pl.BlockSpec((rows, tm), lambda i, j: (0, i), pipeline_mode=once),
                  pl.BlockSpec((heads, 1, tm), lambda i, j: (0, 0, i)),
                  pl.BlockSpec((tm, d), lambda i, j: (i, 0), pipeline_mode=once), vec, vec],
        out_specs=pl.BlockSpec((tm, d), lambda i, j: (i, 0), pipeline_mode=once),
        scratch_shapes=[pltpu.VMEM((2, tm, te), jnp.bfloat16)],
        compiler_params=pltpu.CompilerParams(dimension_semantics=("parallel", "arbitrary"),
                                             vmem_limit_bytes=PEER_VMEM_LIMIT_BYTES), name="peer_dense",
    )(h_bf_t, u_bf, v_bf, scores_t, exps_t, tau, h, g.reshape(1, d), b.reshape(1, d))


def kernel(x, positions, w_in, b_gate, lambda_q1, lambda_k1, lambda_q2, lambda_k2, subln_g, w_sb_branch,
           w_da_branch, w_out, ln1_g, ln1_b, peer_w_q, peer_sub_keys, peer_u, peer_v, ln2_g, ln2_b):
    batch, seq, d = x.shape
    depth = w_in.shape[0]
    hd = lambda_q1.shape[-1]
    sb_w, da_w = w_sb_branch.shape[1], w_da_branch.shape[1]
    sb_heads, da_heads = sb_w // hd, da_w // (2 * hd)
    n_qkv = 3 * sb_w + 3 * da_w
    alpha = (2 * depth) ** 0.25
    bf = jnp.bfloat16
    t = batch * seq

    cos, sin = _rope_tables(positions, hd)
    col_scale = np.ones((1, n_qkv), np.float32)
    col_scale[:, :sb_w] = hd ** -0.5 * math.log2(math.e)
    col_scale[:, 3 * sb_w:3 * sb_w + da_w] = hd ** -0.5 * math.log2(math.e)
    col_scale = jnp.asarray(col_scale)
    h = x.reshape(t, d)
    h_bf = _to_bf16(h)
    for l in range(depth):
        lam_init = 0.8 - 0.6 * math.exp(-0.3 * l)
        w_in_bf = _to_bf16(w_in[l])
        qkv, w_sb_bf, w_da_bf, w_out_bf, w_q_bf = _qkv_proj(
            h_bf, w_in_bf, col_scale, cos, sin, n_qkv, (3 * sb_w, 3 * sb_w + 2 * da_w), hd,
            (w_sb_branch[l], w_da_branch[l], w_out[l], peer_w_q[l]))
        gate, u_bf = _gate_proj(h_bf, w_in_bf, b_gate[l].reshape(1, -1), n_qkv, 2 * d, (peer_u[l],))
        o_sb = _sb_attention(qkv, batch, seq, sb_heads, hd)
        o_da = _da_attention(qkv, (lambda_q1[l], lambda_k1[l], lambda_q2[l], lambda_k2[l]), subln_g[l],
                             batch, seq, da_heads, hd, 3 * sb_w, lam_init)
        merged, v_bf = _merge(o_sb, o_da, w_sb_bf, w_da_bf, gate, (peer_v[l],))
        mix = _matmul(merged, w_out_bf, jnp.float32, "out_proj")
        h, h_bf_t, hq = _residual_ln_peer_query(h, mix, ln1_g[l], ln1_b[l], alpha, w_q_bf)

        heads, _, n_keys, half = peer_sub_keys[l].shape
        scores_t, exps_t, tau = _peer_topk(hq, peer_sub_keys[l].astype(bf))
        h = _peer_dense_ln(h, h_bf_t, u_bf, v_bf, scores_t, exps_t, tau, ln2_g[l], ln2_b[l], alpha, heads, n_keys)
        if l + 1 < depth:
            h_bf = _to_bf16(h)
    return h.reshape(batch, seq, d)
```

```python
import functools
import math

import numpy as np
import jax
import jax.numpy as jnp
from jax import lax
from jax.experimental import pallas as pl
from jax.experimental.pallas import tpu as pltpu

LN_EPS = 1e-5
ROPE_THETA = 10000.0
PEER_TOPK = 16
V7X_VMEM_LIMIT_BYTES = 56 * 1024 * 1024
PEER_VMEM_LIMIT_BYTES = 62 * 1024 * 1024
LANES = 128
SUBLANES = 8
MXU_TILE = 256
PEER_TOKEN_BLOCK = 512
PEER_EXPERT_BLOCK = 512
PEER_KEY_ROWS = 256
PEER_KEY_SPLIT = 4
PEER_LINK_ROLLS = 1
NEG_INF = float("-inf")


def _params(*sem):
    return pltpu.CompilerParams(dimension_semantics=sem, vmem_limit_bytes=V7X_VMEM_LIMIT_BYTES)


def _blk(n, pref):
    b = min(n, pref)
    while n % b:
        b //= 2
    return b


def _cast_kernel(x_ref, o_ref):
    o_ref[...] = x_ref[...].astype(o_ref.dtype)


def _to_bf16(a):
    m, n = a.shape
    rows = max(2 * SUBLANES, (8 * 1024 * 1024) // (4 * n))
    bm = _blk(m, 1 << (rows.bit_length() - 1))
    spec = pl.BlockSpec((bm, n), lambda i: (i, 0))
    return pl.pallas_call(
        _cast_kernel, out_shape=jax.ShapeDtypeStruct((m, n), jnp.bfloat16), grid=(m // bm,),
        in_specs=[spec], out_specs=spec, compiler_params=_params("parallel"), name="to_bf16",
    )(a)


def _side_cast_plan(arrays, grid):
    steps, nj = grid[0] * grid[1], grid[1]
    specs, shapes, blocks = [], [], []
    for a in arrays:
        r, c = a.shape
        nb = 1
        while nb * 2 <= steps and r % (nb * 2) == 0 and (r // (nb * 2)) % (2 * SUBLANES) == 0:
            nb *= 2
        specs.append(pl.BlockSpec((r // nb, c), lambda i, j, nb=nb: (jnp.minimum(i * nj + j, nb - 1), 0)))
        shapes.append(jax.ShapeDtypeStruct((r, c), jnp.bfloat16))
        blocks.append(nb)
    return specs, shapes, tuple(blocks)


def _side_casts(src_refs, dst_refs, side_blocks):
    step = pl.program_id(0) * pl.num_programs(1) + pl.program_id(1)
    for src, dst, nb in zip(src_refs, dst_refs, side_blocks):
        @pl.when(step < nb)
        def _(src=src, dst=dst):
            dst[...] = src[...].astype(dst.dtype)


def _rope_kernel(pos_ref, freq_ref, cos_ref, sin_ref):
    ang = pos_ref[...].astype(jnp.float32) * freq_ref[...]
    hd = freq_ref.shape[1]
    lane = lax.broadcasted_iota(jnp.int32, ang.shape, 1)
    cos_ref[...] = jnp.cos(ang)
    s = jnp.sin(ang)
    sin_ref[...] = jnp.where(lane < hd // 2, -s, s)


def _rope_tables(positions, hd):
    t = positions.size
    bm = _blk(t, 2048)
    inv = ROPE_THETA ** (-np.arange(0, hd, 2, dtype=np.float32) / np.float32(hd))
    freq = jnp.asarray(np.concatenate([inv, inv]).astype(np.float32).reshape(1, hd))
    out = jax.ShapeDtypeStruct((t, hd), jnp.float32)
    return pl.pallas_call(
        _rope_kernel, out_shape=(out, out), grid=(t // bm,),
        in_specs=[pl.BlockSpec((bm, 1), lambda i: (i, 0)), pl.BlockSpec((1, hd), lambda i: (0, 0))],
        out_specs=(pl.BlockSpec((bm, hd), lambda i: (i, 0)), pl.BlockSpec((bm, hd), lambda i: (i, 0))),
        compiler_params=_params("parallel"), name="rope_tables",
    )(positions.reshape(t, 1), freq)


def _qkv_kernel(x_ref, w_ref, cs_ref, cos_ref, sin_ref, *refs, rope_lo, rope_hi, hd, side_blocks):
    n = len(side_blocks)
    o_ref = refs[n]
    _side_casts(refs[:n], refs[n + 1:], side_blocks)
    acc = jnp.dot(x_ref[...], w_ref[...], preferred_element_type=jnp.float32) * cs_ref[...]
    j = pl.program_id(1)
    is_rope = jnp.logical_and(j >= rope_lo, j < rope_hi)

    @pl.when(is_rope)
    def _():
        c = cos_ref[...]
        s = sin_ref[...]
        for t in range(acc.shape[1] // hd):
            a = acc[:, t * hd:(t + 1) * hd]
            o_ref[:, t * hd:(t + 1) * hd] = (a * c + pltpu.roll(a, hd // 2, 1) * s).astype(o_ref.dtype)

    @pl.when(jnp.logical_not(is_rope))
    def _():
        o_ref[...] = acc.astype(o_ref.dtype)


def _qkv_proj(x, w, col_scale, cos, sin, n_out, rope_cols, hd, side):
    m, k = x.shape
    bm, bn = _blk(m, 1024), _blk(math.gcd(n_out, rope_cols[0], rope_cols[1]), 1024)
    grid = (m // bm, n_out // bn)
    side_specs, side_shapes, side_blocks = _side_cast_plan(side, grid)
    kern = functools.partial(_qkv_kernel, rope_lo=rope_cols[0] // bn, rope_hi=rope_cols[1] // bn, hd=hd,
                             side_blocks=side_blocks)
    return pl.pallas_call(
        kern, out_shape=(jax.ShapeDtypeStruct((m, n_out), jnp.bfloat16), *side_shapes), grid=grid,
        in_specs=[pl.BlockSpec((bm, k), lambda i, j: (i, 0)), pl.BlockSpec((k, bn), lambda i, j: (0, j)),
                  pl.BlockSpec((1, bn), lambda i, j: (0, j)),
                  pl.BlockSpec((bm, hd), lambda i, j: (i, 0)), pl.BlockSpec((bm, hd), lambda i, j: (i, 0)),
                  *side_specs],
        out_specs=(pl.BlockSpec((bm, bn), lambda i, j: (i, j)), *side_specs),
        compiler_params=_params("arbitrary", "arbitrary"), name="qkv_proj",
    )(x, w, col_scale, cos, sin, *side)


def _gate_kernel(x_ref, w_ref, b_ref, *refs, side_blocks):
    n = len(side_blocks)
    o_ref = refs[n]
    acc = jnp.dot(x_ref[...], w_ref[...], preferred_element_type=jnp.float32)
    o_ref[...] = jax.nn.sigmoid(acc + b_ref[...]).astype(o_ref.dtype)
    _side_casts(refs[:n], refs[n + 1:], side_blocks)


def _gate_proj(x, w, bias, col0, n_out, side):
    m, k = x.shape
    bm, bn = _blk(m, 1024), _blk(math.gcd(n_out, col0), 1024)
    off = col0 // bn
    grid = (m // bm, n_out // bn)
    side_specs, side_shapes, side_blocks = _side_cast_plan(side, grid)
    return pl.pallas_call(
        functools.partial(_gate_kernel, side_blocks=side_blocks),
        out_shape=(jax.ShapeDtypeStruct((m, n_out), jnp.float32), *side_shapes), grid=grid,
        in_specs=[pl.BlockSpec((bm, k), lambda i, j: (i, 0)), pl.BlockSpec((k, bn), lambda i, j: (0, j + off)),
                  pl.BlockSpec((1, bn), lambda i, j: (0, j)), *side_specs],
        out_specs=(pl.BlockSpec((bm, bn), lambda i, j: (i, j)), *side_specs),
        compiler_params=_params("arbitrary", "arbitrary"), name="gate_proj",
    )(x, w, bias, *side)


def _mm_kernel(a_ref, b_ref, o_ref):
    o_ref[...] = jnp.dot(a_ref[...], b_ref[...], preferred_element_type=jnp.float32).astype(o_ref.dtype)


def _matmul(a, b, out_dtype, name):
    m, k = a.shape
    n = b.shape[1]
    bm, bn = _blk(m, 1024), _blk(n, 1024)
    return pl.pallas_call(
        _mm_kernel, out_shape=jax.ShapeDtypeStruct((m, n), out_dtype), grid=(m // bm, n // bn),
        in_specs=[pl.BlockSpec((bm, k), lambda i, j: (i, 0)), pl.BlockSpec((k, bn), lambda i, j: (0, j))],
        out_specs=pl.BlockSpec((bm, bn), lambda i, j: (i, j)),
        compiler_params=_params("parallel", "parallel"), name=name,
    )(a, b)


def _merge_kernel(osb_ref, oda_ref, wsb_ref, wda_ref, gsb_ref, gda_ref, *refs, side_blocks):
    n = len(side_blocks)
    o_ref = refs[n]
    ysb = jnp.dot(osb_ref[...], wsb_ref[...], preferred_element_type=jnp.float32)
    yda = jnp.dot(oda_ref[...], wda_ref[...], preferred_element_type=jnp.float32)
    o_ref[...] = (gsb_ref[...] * ysb + gda_ref[...] * yda).astype(o_ref.dtype)
    _side_casts(refs[:n], refs[n + 1:], side_blocks)


def _merge(o_sb, o_da, w_sb, w_da, gate, side):
    m = o_sb.shape[0]
    d = w_sb.shape[1]
    bm, bn = _blk(m, 1024), _blk(d, 512)
    goff = d // bn
    grid = (m // bm, d // bn)
    side_specs, side_shapes, side_blocks = _side_cast_plan(side, grid)
    return pl.pallas_call(
        functools.partial(_merge_kernel, side_blocks=side_blocks),
        out_shape=(jax.ShapeDtypeStruct((m, d), jnp.bfloat16), *side_shapes), grid=grid,
        in_specs=[pl.BlockSpec((bm, o_sb.shape[1]), lambda i, j: (i, 0)),
                  pl.BlockSpec((bm, o_da.shape[1]), lambda i, j: (i, 0)),
                  pl.BlockSpec((w_sb.shape[0], bn), lambda i, j: (0, j)),
                  pl.BlockSpec((w_da.shape[0], bn), lambda i, j: (0, j)),
                  pl.BlockSpec((bm, bn), lambda i, j: (i, j)),
                  pl.BlockSpec((bm, bn), lambda i, j: (i, j + goff)), *side_specs],
        out_specs=(pl.BlockSpec((bm, bn), lambda i, j: (i, j)), *side_specs),
        compiler_params=_params("arbitrary", "arbitrary"), name="branch_merge",
    )(o_sb, o_da, w_sb, w_da, gate, gate, *side)


def _ln_rows(res_ref, y_ref, g_ref, b_ref, alpha):
    z = alpha * res_ref[...] + y_ref[...]
    mu = jnp.mean(z, axis=-1, keepdims=True)
    zc = z - mu
    var = jnp.mean(zc * zc, axis=-1, keepdims=True)
    return zc * lax.rsqrt(var + LN_EPS) * g_ref[...] + b_ref[...]


def _ln_peer_kernel(res_ref, y_ref, g_ref, b_ref, wq_ref, o_ref, low_t_ref, hq_ref, *, alpha):
    out = _ln_rows(res_ref, y_ref, g_ref, b_ref, alpha)
    o_ref[...] = out
    low = out.astype(low_t_ref.dtype)
    low_t_ref[...] = low.T
    hq_ref[...] = jnp.dot(low, wq_ref[...], preferred_element_type=jnp.float32).astype(hq_ref.dtype)


def _residual_ln_peer_query(res, y, g, b, alpha, w_q):
    m, d = res.shape
    nq = w_q.shape[1]
    bm = _blk(m, 256)
    row = pl.BlockSpec((bm, d), lambda i: (i, 0))
    vec = pl.BlockSpec((1, d), lambda i: (0, 0))
    return pl.pallas_call(
        functools.partial(_ln_peer_kernel, alpha=alpha),
        out_shape=(jax.ShapeDtypeStruct((m, d), jnp.float32), jax.ShapeDtypeStruct((d, m), jnp.bfloat16),
                   jax.ShapeDtypeStruct((m, nq), jnp.bfloat16)),
        grid=(m // bm,),
        in_specs=[row, row, vec, vec, pl.BlockSpec((d, nq), lambda i: (0, 0), pipeline_mode=pl.Buffered(1))],
        out_specs=(row, pl.BlockSpec((d, bm), lambda i: (0, i)), pl.BlockSpec((bm, nq), lambda i: (i, 0))),
        compiler_params=_params("parallel"), name="layernorm_peer_query",
    )(res, y, g.reshape(1, d), b.reshape(1, d), w_q)


SB_SKIP_LOG2 = -100.0 * math.log2(math.e)


def _sb_kernel(q_ref, k_ref, v_ref, tri_ref, o_ref, c_ref, acc_ref, *, blk, hd, hps):
    qi = pl.program_id(2)
    row = lax.broadcasted_iota(jnp.int32, (blk, blk), 0)
    col = lax.broadcasted_iota(jnp.int32, (blk, blk), 1)
    valid_diag = col < row
    tri = tri_ref[...]

    def visit(j, valid):
        start = pl.multiple_of(j * blk, blk)
        zs, log_fails = [], []
        for g in range(hps):
            q = q_ref[:, g * hd:(g + 1) * hd]
            k = k_ref[pl.ds(start, blk), g * hd:(g + 1) * hd]
            z = lax.dot_general(q, k, (((1,), (1,)), ((), ())), preferred_element_type=jnp.float32)
            log_fail = -(jnp.maximum(z, 0.0) + jnp.log2(1.0 + jnp.exp2(-jnp.abs(z))))
            if valid is not None:
                log_fail = jnp.where(valid, log_fail, 0.0)
            zs.append(z)
            log_fails.append(log_fail)
        sums_all = jnp.dot(jnp.concatenate([lf.astype(jnp.bfloat16) for lf in log_fails], axis=0), tri,
                           preferred_element_type=jnp.float32)
        for g in range(hps):
            v = v_ref[pl.ds(start, blk), g * hd:(g + 1) * hd]
            sums = sums_all[g * blk:(g + 1) * blk]
            c = c_ref[g]
            w = jnp.exp2(zs[g] + log_fails[g] + sums[:, :blk] + c)
            if valid is not None:
                w = jnp.where(valid, w, 0.0)
            acc_ref[g] += jnp.dot(w.astype(v.dtype), v, preferred_element_type=jnp.float32)
            c_ref[g] = c + sums[:, blk:]

    def bound():
        return jnp.max(c_ref[:, :, :hd])

    c_ref[...] = jnp.zeros_like(c_ref)
    acc_ref[...] = jnp.zeros_like(acc_ref)
    visit(qi, valid_diag)

    def cond(state):
        j, worst = state
        return jnp.logical_and(j >= 0, worst > SB_SKIP_LOG2)

    def body(state):
        j, _ = state
        visit(j, None)
        return j - 1, bound()

    lax.while_loop(cond, body, (qi - 1, bound()))
    for g in range(hps):
        o_ref[:, g * hd:(g + 1) * hd] = acc_ref[g].astype(o_ref.dtype)


def _sb_attention(qkv, batch, seq, heads, hd):
    blk = _blk(seq, 256)
    nq = seq // blk
    hps = max(h for h in (4, 2, 1) if heads % h == 0)
    hg, w = heads // hps, hps * hd
    tri = np.concatenate([np.tril(np.ones((blk, blk), np.float32), -1), np.ones((blk, blk), np.float32)], axis=1)
    kern = functools.partial(_sb_kernel, blk=blk, hd=hd, hps=hps)
    return pl.pallas_call(
        kern, out_shape=jax.ShapeDtypeStruct((batch * seq, heads * hd), jnp.bfloat16), grid=(batch, hg, nq),
        in_specs=[pl.BlockSpec((blk, w), lambda b, h, i: (b * nq + i, h)),
                  pl.BlockSpec((seq, w), lambda b, h, i: (b, hg + h)),
                  pl.BlockSpec((seq, w), lambda b, h, i: (b, 2 * hg + h)),
                  pl.BlockSpec((blk, 2 * blk), lambda b, h, i: (0, 0))],
        out_specs=pl.BlockSpec((blk, w), lambda b, h, i: (b * nq + i, h)),
        scratch_shapes=[pltpu.VMEM((hps, blk, blk), jnp.float32), pltpu.VMEM((hps, blk, hd), jnp.float32)],
        compiler_params=_params("parallel", "parallel", "parallel"), name="stick_breaking_attention",
    )(qkv, qkv, qkv, jnp.asarray(tri, jnp.bfloat16))


def _da_kernel(lq1_ref, lk1_ref, lq2_ref, lk2_ref, g_ref, q1_ref, q2_ref, k1_ref, k2_ref, v_ref, o_ref,
               m_ref, l_ref, acc_ref, *, blk, hd, lam_init):
    qi = pl.program_id(2)
    row = lax.broadcasted_iota(jnp.int32, (blk, blk), 0)
    col = lax.broadcasted_iota(jnp.int32, (blk, blk), 1)
    causal_diag = col <= row
    qs = (q1_ref[...], q2_ref[...])
    ks = (k1_ref, k2_ref)

    m_ref[...] = jnp.full_like(m_ref, NEG_INF)
    l_ref[...] = jnp.zeros_like(l_ref)
    acc_ref[...] = jnp.zeros_like(acc_ref)

    def visit(j, mask):
        start = pl.multiple_of(j * blk, blk)
        v = v_ref[pl.ds(start, blk), :]
        probs, corrs = [], []
        for s in range(2):
            k = ks[s][pl.ds(start, blk), :]
            sc = lax.dot_general(qs[s], k, (((1,), (1,)), ((), ())), preferred_element_type=jnp.float32)
            if mask is not None:
                sc = jnp.where(mask, sc, NEG_INF)
            m_old = m_ref[s]
            m_new = jnp.maximum(m_old, jnp.max(sc, axis=-1, keepdims=True))
            p = jnp.exp2(sc - jnp.concatenate([m_new] * (blk // hd), axis=1))
            corr = jnp.exp2(m_old - m_new)
            l_ref[s] = corr * l_ref[s] + jnp.sum(p, axis=-1, keepdims=True)
            m_ref[s] = m_new
            probs.append(p.astype(v.dtype))
            corrs.append(jnp.concatenate([corr, corr], axis=1))
        pv = jnp.dot(jnp.concatenate(probs, axis=0), v, preferred_element_type=jnp.float32)
        for s in range(2):
            acc_ref[s] = corrs[s] * acc_ref[s] + pv[s * blk:(s + 1) * blk]

    def body(j, carry):
        visit(j, None)
        return carry

    lax.fori_loop(0, qi, body, 0)
    visit(qi, causal_diag)

    lam = (jnp.exp(jnp.sum(lq1_ref[...] * lk1_ref[...], axis=-1, keepdims=True))
           - jnp.exp(jnp.sum(lq2_ref[...] * lk2_ref[...], axis=-1, keepdims=True)) + lam_init)
    l1 = jnp.concatenate([l_ref[0], l_ref[0]], axis=1)
    l2 = jnp.concatenate([l_ref[1], l_ref[1]], axis=1)
    o = acc_ref[0] / l1 - lam * (acc_ref[1] / l2)
    o = o * lax.rsqrt(jnp.mean(o * o, axis=-1, keepdims=True) + LN_EPS) * g_ref[...] * (1.0 - lam_init)
    o_ref[...] = o.astype(o_ref.dtype)


def _da_attention(qkv, lambdas, subln_g, batch, seq, heads, hd, col0, lam_init):
    blk = _blk(seq, 512)
    nq = seq // blk
    qb, kb, vb = col0 // hd, col0 // hd + 2 * heads, col0 // (2 * hd) + 2 * heads
    vec = pl.BlockSpec((1, hd), lambda b, h, i: (0, 0))
    kern = functools.partial(_da_kernel, blk=blk, hd=hd, lam_init=lam_init)
    return pl.pallas_call(
        kern, out_shape=jax.ShapeDtypeStruct((batch * seq, heads * 2 * hd), jnp.bfloat16), grid=(batch, heads, nq),
        in_specs=[vec, vec, vec, vec, pl.BlockSpec((1, 2 * hd), lambda b, h, i: (0, 0)),
                  pl.BlockSpec((blk, hd), lambda b, h, i: (b * nq + i, qb + 2 * h)),
                  pl.BlockSpec((blk, hd), lambda b, h, i: (b * nq + i, qb + 2 * h + 1)),
                  pl.BlockSpec((seq, hd), lambda b, h, i: (b, kb + 2 * h)),
                  pl.BlockSpec((seq, hd), lambda b, h, i: (b, kb + 2 * h + 1)),
                  pl.BlockSpec((seq, 2 * hd), lambda b, h, i: (b, vb + h))],
        out_specs=pl.BlockSpec((blk, 2 * hd), lambda b, h, i: (b * nq + i, h)),
        scratch_shapes=[pltpu.VMEM((2, blk, hd), jnp.float32), pltpu.VMEM((2, blk, hd), jnp.float32),
                        pltpu.VMEM((2, blk, 2 * hd), jnp.float32)],
        compiler_params=_params("parallel", "parallel", "parallel"), name="differential_attention",
    )(*[l.reshape(1, hd) for l in lambdas], subln_g.reshape(1, 2 * hd), qkv, qkv, qkv, qkv, qkv)


def _extract_top(x, n):
    rows = lax.broadcasted_iota(jnp.int32, x.shape, 0).astype(jnp.float32)
    tops = []
    for _ in range(n):
        m = jnp.max(x, axis=0, keepdims=True)
        tops.append(m)
        first = jnp.min(jnp.where(x == m, rows, float(x.shape[0])), axis=0, keepdims=True)
        x = jnp.where(rows == first, NEG_INF, x)
    return tops


def _sorting_network(n):
    pairs, p = [], 1
    while p < n:
        k = p
        while k >= 1:
            for j in range(k % p, n - k, 2 * k):
                for i in range(min(k, n - j - k)):
                    if (i + j) // (2 * p) == (i + j + k) // (2 * p):
                        pairs.append((i + j, i + j + k))
            k //= 2
        p *= 2
    return pairs


def _extract_top_sorted(x, n):
    groups = [x[g * SUBLANES:(g + 1) * SUBLANES] for g in range(x.shape[0] // SUBLANES)]
    for a, b in _sorting_network(len(groups)):
        groups[a], groups[b] = jnp.maximum(groups[a], groups[b]), jnp.minimum(groups[a], groups[b])
    lane_list = lax.broadcasted_iota(jnp.int32, groups[0].shape, 0).astype(jnp.float32)
    tops = []
    for t in range(n):
        head = groups[0]
        m = jnp.max(head, axis=0, keepdims=True)
        tops.append(m)
        if t == n - 1:
            break
        first = jnp.min(jnp.where(head == m, lane_list, float(SUBLANES)), axis=0, keepdims=True)
        popped = lane_list == first
        live = min(len(groups), n - t)
        for g in range(live):
            below = groups[g + 1] if g + 1 < len(groups) else NEG_INF
            groups[g] = jnp.where(popped, below, groups[g])
    return tops


def _pair_candidates(top_a, top_b, combine):
    k = len(top_a)
    sub = SUBLANES
    b_all = jnp.concatenate(top_b, axis=0)
    pieces, masks, singles = [], [], []
    for p in range(k):
        n_q = k // (p + 1)
        if n_q == 1:
            singles.append(top_a[p])
            continue
        rows = -(-n_q // sub) * sub
        pieces.append(combine(top_a[p], b_all[0:rows]))
        masks.append(lax.broadcasted_iota(jnp.int32, (rows, 1), 0) < n_q)
    if singles:
        pieces.append(combine(jnp.concatenate(singles, axis=0), top_b[0]))
        masks.append(None)
    return pieces, masks


def _peer_topk_kernel(q_ref, keys_ref, s_ref, e_ref, tau_ref, *, n_keys, topk):
    half = q_ref.shape[1] // 2
    scores, tops = [], []
    for c in range(2):
        s = lax.dot_general(keys_ref[0, c], q_ref[:, c * half:(c + 1) * half], (((1,), (1,)), ((), ())),
                            preferred_element_type=jnp.float32)
        scores.append(s)
        n_groups = n_keys // SUBLANES
        pow2 = n_keys % SUBLANES == 0 and n_groups & (n_groups - 1) == 0
        tops.append(_extract_top_sorted(s, topk) if pow2 else _extract_top(s, topk))
    exps = [[jnp.exp(t - top[0]) for t in top] for top in tops]
    sums, masks = _pair_candidates(tops[0], tops[1], jnp.add)
    prods, _ = _pair_candidates(exps[0], exps[1], jnp.multiply)
    cand = jnp.concatenate([s if m is None else jnp.where(m, s, NEG_INF) for s, m in zip(sums, masks)], axis=0)
    cand_e = jnp.concatenate(prods, axis=0)
    groups = -(-cand.shape[0] // SUBLANES)
    pad_rows = SUBLANES * (1 << (groups - 1).bit_length()) - cand.shape[0]
    padded = jnp.concatenate([cand, jnp.full((pad_rows, cand.shape[1]), NEG_INF, cand.dtype)], axis=0)
    tau = _extract_top_sorted(padded, topk)[-1]
    z = jnp.sum(jnp.where(cand >= tau, cand_e, 0.0), axis=0, keepdims=True)
    s_ref[...] = jnp.concatenate(scores, axis=0)
    e_ref[...] = jnp.concatenate([jnp.exp(scores[0] - tops[0][0]) / z, jnp.exp(scores[1] - tops[1][0])], axis=0)
    tau_ref[0] = tau


def _peer_topk(hq, keys):
    t = hq.shape[0]
    heads, _, n_keys, half = keys.shape
    tn = _blk(t, 512)
    kern = functools.partial(_peer_topk_kernel, n_keys=n_keys, topk=PEER_TOPK)
    st = jax.ShapeDtypeStruct((heads * 2 * n_keys, t), jnp.float32)
    blk = pl.BlockSpec((2 * n_keys, tn), lambda i, h: (h, i))
    return pl.pallas_call(
        kern, out_shape=(st, st, jax.ShapeDtypeStruct((heads, 1, t), jnp.float32)), grid=(t // tn, heads),
        in_specs=[pl.BlockSpec((tn, 2 * half), lambda i, h: (i, h)),
                  pl.BlockSpec((1, 2, n_keys, half), lambda i, h: (h, 0, 0, 0))],
        out_specs=(blk, blk, pl.BlockSpec((1, 1, tn), lambda i, h: (h, 0, i))),
        compiler_params=_params("parallel", "parallel"), name="peer_topk",
    )(hq, keys)


def _peer_dense_kernel(h_ref, u_ref, v_ref, s_ref, e_ref, tau_ref, res_ref, g_ref, b_ref, o_ref, ga_ref, *,
                       heads, n_keys, n_blocks, alpha):
    j = pl.program_id(1)

    @pl.when(j == 0)
    def _():
        ga_ref[1] = jnp.zeros(ga_ref.shape[1:], ga_ref.dtype)
        o_ref[...] = jnp.zeros_like(o_ref)

    for parity in range(2):
        pl.when(j % 2 == parity)(functools.partial(
            _peer_dense_step, h_ref, u_ref, v_ref, s_ref, e_ref, tau_ref, o_ref, ga_ref,
            jnp.minimum(j, n_blocks - 1), parity, heads, n_keys))

    @pl.when(j == n_blocks)
    def _():
        rows = min(o_ref.shape[0], LANES)
        for r in range(o_ref.shape[0] // rows):
            sl = slice(r * rows, (r + 1) * rows)
            o_ref[sl] = _ln_rows(res_ref.at[sl], o_ref.at[sl], g_ref, b_ref, alpha)


def _peer_dense_step(h_ref, u_ref, v_ref, s_ref, e_ref, tau_ref, o_ref, ga_ref, jb, slot_now, heads, n_keys):
    tm, te = ga_ref.shape[1:]
    d = o_ref.shape[1]
    n_i = te // n_keys
    mxu = MXU_TILE
    n_tok_tiles = tm // mxu
    rt = min(te, PEER_KEY_ROWS)

    def link_from(x):
        bits = pltpu.bitcast(x[0:8, 0:LANES], jnp.uint32)
        zero = lax.shift_right_logical(lax.shift_right_logical(bits, jnp.uint32(16)), jnp.uint32(16))
        return pltpu.bitcast(zero, jnp.float32)

    pre_tiles = {}

    k_parts = PEER_KEY_SPLIT
    kw = d // k_parts
    done_tiles = set()

    def pre_tile(p, kk):
        mi, ni = divmod(p, n_tok_tiles)
        part = jnp.dot(u_ref[mi * rt:(mi + 1) * rt, kk * kw:(kk + 1) * kw],
                       h_ref[kk * kw:(kk + 1) * kw, ni * mxu:(ni + 1) * mxu], preferred_element_type=jnp.float32)
        pre_tiles[mi, ni] = part if kk == 0 else pre_tiles[mi, ni] + part
        if kk == k_parts - 1:
            done_tiles.add((mi, ni))
        return link_from(part)

    def value_chunk(c):
        cols = slice(c * mxu, (c + 1) * mxu)
        acc = o_ref[:, cols] + jnp.dot(ga_ref[1 - slot_now], v_ref[:, cols], preferred_element_type=jnp.float32)
        o_ref[:, cols] = acc
        return link_from(acc)

    n_pre = (te // rt) * n_tok_tiles
    chunks = [functools.partial(pre_tile, p, kk) for p in range(n_pre) for kk in range(k_parts)]
    chunks += [functools.partial(value_chunk, c) for c in range(d // mxu)]
    results = []

    def finalize(il):
        mi, off = divmod(il * n_keys, rt)
        while any((mi, ni) not in done_tiles for ni in range(n_tok_tiles)):
            results.append(chunks.pop(0)())
        p = jnp.concatenate([pre_tiles[mi, ni][off:off + n_keys, :] for ni in range(n_tok_tiles)], axis=1)
        act = 0.5 * p * (1.0 + lax.erf(p * (2.0 ** -0.5)))
        ga_ref[slot_now, :, il * n_keys:(il + 1) * n_keys] = (gates.pop(il) * act).astype(ga_ref.dtype).T

    pieces = [(il, h) for il in range(n_i) for h in range(heads)]
    costs = [rt * kw * mxu] * (n_pre * k_parts) + [tm * te * mxu] * (d // mxu)
    ends = [sum(costs[:c + 1]) for c in range(len(costs))]
    slot = ends[-1] / len(pieces)
    linked = 0
    gates = {}
    link = link_from(s_ref[0:8, 0:LANES])
    for k, (il, h) in enumerate(pieces):
        while len(results) < len(costs) and ends[len(results)] - costs[len(results)] < (k + 1) * slot:
            results.append(chunks.pop(0)())
        while linked < len(results) and ends[linked] <= k * slot:
            link, linked = link + results[linked], linked + 1
        for _ in range(PEER_LINK_ROLLS):
            link = pltpu.roll(link, 1, 1)
        base = h * 2 * n_keys
        a_row = s_ref[pl.ds(base + jb * n_i + il, 1), :] + jnp.concatenate([link[0:1, :]] * (tm // LANES), axis=1)
        ea_row = e_ref[pl.ds(base + jb * n_i + il, 1), :]
        b_tile = s_ref[base + n_keys:base + 2 * n_keys, :]
        eb_tile = e_ref[base + n_keys:base + 2 * n_keys, :]
        term = jnp.where(a_row + b_tile >= tau_ref[h], ea_row * eb_tile, 0.0)
        gates[il] = term if h == 0 else gates[il] + term
        if h == heads - 1:
            finalize(il)
    while chunks:
        chunks.pop(0)()


def _peer_dense_ln(h, h_bf_t, u_bf, v_bf, scores_t, exps_t, tau, g, b, alpha, heads, n_keys):
    d, t = h_bf_t.shape
    e = u_bf.shape[0]
    tm, te = _blk(t, PEER_TOKEN_BLOCK), _blk(e, PEER_EXPERT_BLOCK)
    nb = e // te
    kern = functools.partial(_peer_dense_kernel, heads=heads, n_keys=n_keys, n_blocks=nb, alpha=alpha)
    rows = scores_t.shape[0]
    once = pl.Buffered(1)
    vec = pl.BlockSpec((1, d), lambda i, j: (0, 0))
    return pl.pallas_call(
        kern, out_shape=jax.ShapeDtypeStruct((t, d), jnp.float32), grid=(t // tm, nb + 1),
        in_specs=[pl.BlockSpec((d, tm), lambda i, j: (0, i), pipeline_mode=once),
                  pl.BlockSpec((te, d), lambda i, j: (jnp.minimum(j, nb - 1), 0)),
                  pl.BlockSpec((te, d), lambda i, j: (jnp.maximum(j - 1, 0), 0)),
                  pl.BlockSpec((rows, tm), lambda i, j: (0, i), pipeline_mode=once),
                  pl.BlockSpec((rows, tm), lambda i, j: (0, i), pipeline_mode=once),
                  pl.BlockSpec((heads, 1, tm), lambda i, j: (0, 0, i)),
                  pl.BlockSpec((tm, d), lambda i, j: (i, 0), pipeline_mode=once), vec, vec],
        out_specs=pl.BlockSpec((tm, d), lambda i, j: (i, 0)),
        scratch_shapes=[pltpu.VMEM((2, tm, te), jnp.bfloat16)],
        compiler_params=pltpu.CompilerParams(dimension_semantics=("parallel", "arbitrary"),
                                             vmem_limit_bytes=PEER_VMEM_LIMIT_BYTES), name="peer_dense",
    )(h_bf_t, u_bf, v_bf, scores_t, exps_t, tau, h, g.reshape(1, d), b.reshape(1, d))


def kernel(x, positions, w_in, b_gate, lambda_q1, lambda_k1, lambda_q2, lambda_k2, subln_g, w_sb_branch,
           w_da_branch, w_out, ln1_g, ln1_b, peer_w_q, peer_sub_keys, peer_u, peer_v, ln2_g, ln2_b):
    batch, seq, d = x.shape
    depth = w_in.shape[0]
    hd = lambda_q1.shape[-1]
    sb_w, da_w = w_sb_branch.shape[1], w_da_branch.shape[1]
    sb_heads, da_heads = sb_w // hd, da_w // (2 * hd)
    n_qkv = 3 * sb_w + 3 * da_w
    alpha = (2 * depth) ** 0.25
    bf = jnp.bfloat16
    t = batch * seq

    cos, sin = _rope_tables(positions, hd)
    col_scale = np.ones((1, n_qkv), np.float32)
    col_scale[:, :sb_w] = hd ** -0.5 * math.log2(math.e)
    col_scale[:, 3 * sb_w:3 * sb_w + da_w] = hd ** -0.5 * math.log2(math.e)
    col_scale = jnp.asarray(col_scale)
    h = x.reshape(t, d)
    h_bf = _to_bf16(h)
    for l in range(depth):
        lam_init = 0.8 - 0.6 * math.exp(-0.3 * l)
        w_in_bf = _to_bf16(w_in[l])
        qkv, w_sb_bf, w_da_bf, w_out_bf, w_q_bf = _qkv_proj(
            h_bf, w_in_bf, col_scale, cos, sin, n_qkv, (3 * sb_w, 3 * sb_w + 2 * da_w), hd,
            (w_sb_branch[l], w_da_branch[l], w_out[l], peer_w_q[l]))
        gate, u_bf = _gate_proj(h_bf, w_in_bf, b_gate[l].reshape(1, -1), n_qkv, 2 * d, (peer_u[l],))
        o_sb = _sb_attention(qkv, batch, seq, sb_heads, hd)
        o_da = _da_attention(qkv, (lambda_q1[l], lambda_k1[l], lambda_q2[l], lambda_k2[l]), subln_g[l],
                             batch, seq, da_heads, hd, 3 * sb_w, lam_init)
        merged, v_bf = _merge(o_sb, o_da, w_sb_bf, w_da_bf, gate, (peer_v[l],))
        mix = _matmul(merged, w_out_bf, jnp.float32, "out_proj")
        h, h_bf_t, hq = _residual_ln_peer_query(h, mix, ln1_g[l], ln1_b[l], alpha, w_q_bf)

        heads, _, n_keys, half = peer_sub_keys[l].shape
        scores_t, exps_t, tau = _peer_topk(hq, peer_sub_keys[l].astype(bf))
        h = _peer_dense_ln(h, h_bf_t, u_bf, v_bf, scores_t, exps_t, tau, ln2_g[l], ln2_b[l], alpha, heads, n_keys)
        if l + 1 < depth:
            h_bf = _to_bf16(h)
    return h.reshape(batch, seq, d)
```

```python
import functools
import math

import numpy as np
import jax
import jax.numpy as jnp
from jax import lax
from jax.experimental import pallas as pl
from jax.experimental.pallas import tpu as pltpu

LN_EPS = 1e-5
ROPE_THETA = 10000.0
PEER_TOPK = 16
V7X_VMEM_LIMIT_BYTES = 56 * 1024 * 1024
PEER_VMEM_LIMIT_BYTES = 60 * 1024 * 1024
LANES = 128
SUBLANES = 8
MXU_TILE = 256
PEER_TOKEN_BLOCK = 512
PEER_EXPERT_BLOCK = 512
PEER_KEY_ROWS = 256
PEER_KEY_SPLIT = 4
PEER_LINK_ROLLS = 0
NEG_INF = float("-inf")


def _params(*sem):
    return pltpu.CompilerParams(dimension_semantics=sem, vmem_limit_bytes=V7X_VMEM_LIMIT_BYTES)


def _blk(n, pref):
    b = min(n, pref)
    while n % b:
        b //= 2
    return b


def _cast_kernel(x_ref, o_ref):
    o_ref[...] = x_ref[...].astype(o_ref.dtype)


def _to_bf16(a):
    m, n = a.shape
    rows = max(2 * SUBLANES, (8 * 1024 * 1024) // (4 * n))
    bm = _blk(m, 1 << (rows.bit_length() - 1))
    spec = pl.BlockSpec((bm, n), lambda i: (i, 0))
    return pl.pallas_call(
        _cast_kernel, out_shape=jax.ShapeDtypeStruct((m, n), jnp.bfloat16), grid=(m // bm,),
        in_specs=[spec], out_specs=spec, compiler_params=_params("parallel"), name="to_bf16",
    )(a)


def _side_cast_plan(arrays, grid):
    steps, nj = grid[0] * grid[1], grid[1]
    specs, shapes, blocks = [], [], []
    for a in arrays:
        r, c = a.shape
        nb = 1
        while nb * 2 <= steps and r % (nb * 2) == 0 and (r // (nb * 2)) % (2 * SUBLANES) == 0:
            nb *= 2
        specs.append(pl.BlockSpec((r // nb, c), lambda i, j, nb=nb: (jnp.minimum(i * nj + j, nb - 1), 0)))
        shapes.append(jax.ShapeDtypeStruct((r, c), jnp.bfloat16))
        blocks.append(nb)
    return specs, shapes, tuple(blocks)


def _side_casts(src_refs, dst_refs, side_blocks):
    step = pl.program_id(0) * pl.num_programs(1) + pl.program_id(1)
    for src, dst, nb in zip(src_refs, dst_refs, side_blocks):
        @pl.when(step < nb)
        def _(src=src, dst=dst):
            dst[...] = src[...].astype(dst.dtype)


def _rope_kernel(pos_ref, freq_ref, cos_ref, sin_ref):
    ang = pos_ref[...].astype(jnp.float32) * freq_ref[...]
    hd = freq_ref.shape[1]
    lane = lax.broadcasted_iota(jnp.int32, ang.shape, 1)
    cos_ref[...] = jnp.cos(ang)
    s = jnp.sin(ang)
    sin_ref[...] = jnp.where(lane < hd // 2, -s, s)


def _rope_tables(positions, hd):
    t = positions.size
    bm = _blk(t, 2048)
    inv = ROPE_THETA ** (-np.arange(0, hd, 2, dtype=np.float32) / np.float32(hd))
    freq = jnp.asarray(np.concatenate([inv, inv]).astype(np.float32).reshape(1, hd))
    out = jax.ShapeDtypeStruct((t, hd), jnp.float32)
    return pl.pallas_call(
        _rope_kernel, out_shape=(out, out), grid=(t // bm,),
        in_specs=[pl.BlockSpec((bm, 1), lambda i: (i, 0)), pl.BlockSpec((1, hd), lambda i: (0, 0))],
        out_specs=(pl.BlockSpec((bm, hd), lambda i: (i, 0)), pl.BlockSpec((bm, hd), lambda i: (i, 0))),
        compiler_params=_params("parallel"), name="rope_tables",
    )(positions.reshape(t, 1), freq)


def _qkv_kernel(x_ref, w_ref, cs_ref, cos_ref, sin_ref, *refs, rope_lo, rope_hi, hd, side_blocks):
    n = len(side_blocks)
    o_ref = refs[n]
    _side_casts(refs[:n], refs[n + 1:], side_blocks)
    acc = jnp.dot(x_ref[...], w_ref[...], preferred_element_type=jnp.float32) * cs_ref[...]
    j = pl.program_id(1)
    is_rope = jnp.logical_and(j >= rope_lo, j < rope_hi)

    @pl.when(is_rope)
    def _():
        c = cos_ref[...]
        s = sin_ref[...]
        for t in range(acc.shape[1] // hd):
            a = acc[:, t * hd:(t + 1) * hd]
            o_ref[:, t * hd:(t + 1) * hd] = (a * c + pltpu.roll(a, hd // 2, 1) * s).astype(o_ref.dtype)

    @pl.when(jnp.logical_not(is_rope))
    def _():
        o_ref[...] = acc.astype(o_ref.dtype)


def _qkv_proj(x, w, col_scale, cos, sin, n_out, rope_cols, hd, side):
    m, k = x.shape
    bm, bn = _blk(m, 1024), _blk(math.gcd(n_out, rope_cols[0], rope_cols[1]), 1024)
    grid = (m // bm, n_out // bn)
    side_specs, side_shapes, side_blocks = _side_cast_plan(side, grid)
    kern = functools.partial(_qkv_kernel, rope_lo=rope_cols[0] // bn, rope_hi=rope_cols[1] // bn, hd=hd,
                             side_blocks=side_blocks)
    return pl.pallas_call(
        kern, out_shape=(jax.ShapeDtypeStruct((m, n_out), jnp.bfloat16), *side_shapes), grid=grid,
        in_specs=[pl.BlockSpec((bm, k), lambda i, j: (i, 0)), pl.BlockSpec((k, bn), lambda i, j: (0, j)),
                  pl.BlockSpec((1, bn), lambda i, j: (0, j)),
                  pl.BlockSpec((bm, hd), lambda i, j: (i, 0)), pl.BlockSpec((bm, hd), lambda i, j: (i, 0)),
                  *side_specs],
        out_specs=(pl.BlockSpec((bm, bn), lambda i, j: (i, j)), *side_specs),
        compiler_params=_params("arbitrary", "arbitrary"), name="qkv_proj",
    )(x, w, col_scale, cos, sin, *side)


def _gate_kernel(x_ref, w_ref, b_ref, *refs, side_blocks):
    n = len(side_blocks)
    o_ref = refs[n]
    acc = jnp.dot(x_ref[...], w_ref[...], preferred_element_type=jnp.float32)
    o_ref[...] = jax.nn.sigmoid(acc + b_ref[...]).astype(o_ref.dtype)
    _side_casts(refs[:n], refs[n + 1:], side_blocks)


def _gate_proj(x, w, bias, col0, n_out, side):
    m, k = x.shape
    bm, bn = _blk(m, 1024), _blk(math.gcd(n_out, col0), 1024)
    off = col0 // bn
    grid = (m // bm, n_out // bn)
    side_specs, side_shapes, side_blocks = _side_cast_plan(side, grid)
    return pl.pallas_call(
        functools.partial(_gate_kernel, side_blocks=side_blocks),
        out_shape=(jax.ShapeDtypeStruct((m, n_out), jnp.float32), *side_shapes), grid=grid,
        in_specs=[pl.BlockSpec((bm, k), lambda i, j: (i, 0)), pl.BlockSpec((k, bn), lambda i, j: (0, j + off)),
                  pl.BlockSpec((1, bn), lambda i, j: (0, j)), *side_specs],
        out_specs=(pl.BlockSpec((bm, bn), lambda i, j: (i, j)), *side_specs),
        compiler_params=_params("arbitrary", "arbitrary"), name="gate_proj",
    )(x, w, bias, *side)


def _mm_kernel(a_ref, b_ref, o_ref):
    o_ref[...] = jnp.dot(a_ref[...], b_ref[...], preferred_element_type=jnp.float32).astype(o_ref.dtype)


def _matmul(a, b, out_dtype, name):
    m, k = a.shape
    n = b.shape[1]
    bm, bn = _blk(m, 1024), _blk(n, 1024)
    return pl.pallas_call(
        _mm_kernel, out_shape=jax.ShapeDtypeStruct((m, n), out_dtype), grid=(m // bm, n // bn),
        in_specs=[pl.BlockSpec((bm, k), lambda i, j: (i, 0)), pl.BlockSpec((k, bn), lambda i, j: (0, j))],
        out_specs=pl.BlockSpec((bm, bn), lambda i, j: (i, j)),
        compiler_params=_params("parallel", "parallel"), name=name,
    )(a, b)


def _merge_kernel(osb_ref, oda_ref, wsb_ref, wda_ref, gsb_ref, gda_ref, *refs, side_blocks):
    n = len(side_blocks)
    o_ref = refs[n]
    ysb = jnp.dot(osb_ref[...], wsb_ref[...], preferred_element_type=jnp.float32)
    yda = jnp.dot(oda_ref[...], wda_ref[...], preferred_element_type=jnp.float32)
    o_ref[...] = (gsb_ref[...] * ysb + gda_ref[...] * yda).astype(o_ref.dtype)
    _side_casts(refs[:n], refs[n + 1:], side_blocks)


def _merge(o_sb, o_da, w_sb, w_da, gate, side):
    m = o_sb.shape[0]
    d = w_sb.shape[1]
    bm, bn = _blk(m, 1024), _blk(d, 512)
    goff = d // bn
    grid = (m // bm, d // bn)
    side_specs, side_shapes, side_blocks = _side_cast_plan(side, grid)
    return pl.pallas_call(
        functools.partial(_merge_kernel, side_blocks=side_blocks),
        out_shape=(jax.ShapeDtypeStruct((m, d), jnp.bfloat16), *side_shapes), grid=grid,
        in_specs=[pl.BlockSpec((bm, o_sb.shape[1]), lambda i, j: (i, 0)),
                  pl.BlockSpec((bm, o_da.shape[1]), lambda i, j: (i, 0)),
                  pl.BlockSpec((w_sb.shape[0], bn), lambda i, j: (0, j)),
                  pl.BlockSpec((w_da.shape[0], bn), lambda i, j: (0, j)),
                  pl.BlockSpec((bm, bn), lambda i, j: (i, j)),
                  pl.BlockSpec((bm, bn), lambda i, j: (i, j + goff)), *side_specs],
        out_specs=(pl.BlockSpec((bm, bn), lambda i, j: (i, j)), *side_specs),
        compiler_params=_params("arbitrary", "arbitrary"), name="branch_merge",
    )(o_sb, o_da, w_sb, w_da, gate, gate, *side)


def _ln_rows(res_ref, y_ref, g_ref, b_ref, alpha):
    z = alpha * res_ref[...] + y_ref[...]
    mu = jnp.mean(z, axis=-1, keepdims=True)
    zc = z - mu
    var = jnp.mean(zc * zc, axis=-1, keepdims=True)
    return zc * lax.rsqrt(var + LN_EPS) * g_ref[...] + b_ref[...]


def _ln_peer_kernel(res_ref, y_ref, g_ref, b_ref, wq_ref, o_ref, low_t_ref, hq_ref, *, alpha):
    out = _ln_rows(res_ref, y_ref, g_ref, b_ref, alpha)
    o_ref[...] = out
    low = out.astype(low_t_ref.dtype)
    low_t_ref[...] = low.T
    hq_ref[...] = jnp.dot(low, wq_ref[...], preferred_element_type=jnp.float32).astype(hq_ref.dtype)


def _residual_ln_peer_query(res, y, g, b, alpha, w_q):
    m, d = res.shape
    nq = w_q.shape[1]
    bm = _blk(m, 256)
    row = pl.BlockSpec((bm, d), lambda i: (i, 0))
    vec = pl.BlockSpec((1, d), lambda i: (0, 0))
    return pl.pallas_call(
        functools.partial(_ln_peer_kernel, alpha=alpha),
        out_shape=(jax.ShapeDtypeStruct((m, d), jnp.float32), jax.ShapeDtypeStruct((d, m), jnp.bfloat16),
                   jax.ShapeDtypeStruct((m, nq), jnp.bfloat16)),
        grid=(m // bm,),
        in_specs=[row, row, vec, vec, pl.BlockSpec((d, nq), lambda i: (0, 0), pipeline_mode=pl.Buffered(1))],
        out_specs=(row, pl.BlockSpec((d, bm), lambda i: (0, i)), pl.BlockSpec((bm, nq), lambda i: (i, 0))),
        compiler_params=_params("parallel"), name="layernorm_peer_query",
    )(res, y, g.reshape(1, d), b.reshape(1, d), w_q)


SB_SKIP_LOG2 = -100.0 * math.log2(math.e)


def _sb_kernel(q_ref, k_ref, v_ref, tri_ref, o_ref, c_ref, acc_ref, *, blk, hd, hps):
    qi = pl.program_id(2)
    row = lax.broadcasted_iota(jnp.int32, (blk, blk), 0)
    col = lax.broadcasted_iota(jnp.int32, (blk, blk), 1)
    valid_diag = col < row
    tri = tri_ref[...]

    def visit(j, valid):
        start = pl.multiple_of(j * blk, blk)
        zs, log_fails = [], []
        for g in range(hps):
            q = q_ref[:, g * hd:(g + 1) * hd]
            k = k_ref[pl.ds(start, blk), g * hd:(g + 1) * hd]
            z = lax.dot_general(q, k, (((1,), (1,)), ((), ())), preferred_element_type=jnp.float32)
            log_fail = -(jnp.maximum(z, 0.0) + jnp.log2(1.0 + jnp.exp2(-jnp.abs(z))))
            if valid is not None:
                log_fail = jnp.where(valid, log_fail, 0.0)
            zs.append(z)
            log_fails.append(log_fail)
        sums_all = jnp.dot(jnp.concatenate([lf.astype(jnp.bfloat16) for lf in log_fails], axis=0), tri,
                           preferred_element_type=jnp.float32)
        for g in range(hps):
            v = v_ref[pl.ds(start, blk), g * hd:(g + 1) * hd]
            sums = sums_all[g * blk:(g + 1) * blk]
            c = c_ref[g]
            w = jnp.exp2(zs[g] + log_fails[g] + sums[:, :blk] + c)
            if valid is not None:
                w = jnp.where(valid, w, 0.0)
            acc_ref[g] += jnp.dot(w.astype(v.dtype), v, preferred_element_type=jnp.float32)
            c_ref[g] = c + sums[:, blk:]

    def bound():
        return jnp.max(c_ref[:, :, :hd])

    c_ref[...] = jnp.zeros_like(c_ref)
    acc_ref[...] = jnp.zeros_like(acc_ref)
    visit(qi, valid_diag)

    def cond(state):
        j, worst = state
        return jnp.logical_and(j >= 0, worst > SB_SKIP_LOG2)

    def body(state):
        j, _ = state
        visit(j, None)
        return j - 1, bound()

    lax.while_loop(cond, body, (qi - 1, bound()))
    for g in range(hps):
        o_ref[:, g * hd:(g + 1) * hd] = acc_ref[g].astype(o_ref.dtype)


def _sb_attention(qkv, batch, seq, heads, hd):
    blk = _blk(seq, 256)
    nq = seq // blk
    hps = max(h for h in (4, 2, 1) if heads % h == 0)
    hg, w = heads // hps, hps * hd
    tri = np.concatenate([np.tril(np.ones((blk, blk), np.float32), -1), np.ones((blk, blk), np.float32)], axis=1)
    kern = functools.partial(_sb_kernel, blk=blk, hd=hd, hps=hps)
    return pl.pallas_call(
        kern, out_shape=jax.ShapeDtypeStruct((batch * seq, heads * hd), jnp.bfloat16), grid=(batch, hg, nq),
        in_specs=[pl.BlockSpec((blk, w), lambda b, h, i: (b * nq + i, h)),
                  pl.BlockSpec((seq, w), lambda b, h, i: (b, hg + h)),
                  pl.BlockSpec((seq, w), lambda b, h, i: (b, 2 * hg + h)),
                  pl.BlockSpec((blk, 2 * blk), lambda b, h, i: (0, 0))],
        out_specs=pl.BlockSpec((blk, w), lambda b, h, i: (b * nq + i, h)),
        scratch_shapes=[pltpu.VMEM((hps, blk, blk), jnp.float32), pltpu.VMEM((hps, blk, hd), jnp.float32)],
        compiler_params=_params("parallel", "parallel", "parallel"), name="stick_breaking_attention",
    )(qkv, qkv, qkv, jnp.asarray(tri, jnp.bfloat16))


def _da_kernel(lq1_ref, lk1_ref, lq2_ref, lk2_ref, g_ref, q1_ref, q2_ref, k1_ref, k2_ref, v_ref, o_ref,
               m_ref, l_ref, acc_ref, *, blk, hd, lam_init):
    qi = pl.program_id(2)
    row = lax.broadcasted_iota(jnp.int32, (blk, blk), 0)
    col = lax.broadcasted_iota(jnp.int32, (blk, blk), 1)
    causal_diag = col <= row
    qs = (q1_ref[...], q2_ref[...])
    ks = (k1_ref, k2_ref)

    m_ref[...] = jnp.full_like(m_ref, NEG_INF)
    l_ref[...] = jnp.zeros_like(l_ref)
    acc_ref[...] = jnp.zeros_like(acc_ref)

    def visit(j, mask):
        start = pl.multiple_of(j * blk, blk)
        v = v_ref[pl.ds(start, blk), :]
        probs, corrs = [], []
        for s in range(2):
            k = ks[s][pl.ds(start, blk), :]
            sc = lax.dot_general(qs[s], k, (((1,), (1,)), ((), ())), preferred_element_type=jnp.float32)
            if mask is not None:
                sc = jnp.where(mask, sc, NEG_INF)
            m_old = m_ref[s]
            m_new = jnp.maximum(m_old, jnp.max(sc, axis=-1, keepdims=True))
            p = jnp.exp2(sc - jnp.concatenate([m_new] * (blk // hd), axis=1))
            corr = jnp.exp2(m_old - m_new)
            l_ref[s] = corr * l_ref[s] + jnp.sum(p, axis=-1, keepdims=True)
            m_ref[s] = m_new
            probs.append(p.astype(v.dtype))
            corrs.append(jnp.concatenate([corr, corr], axis=1))
        pv = jnp.dot(jnp.concatenate(probs, axis=0), v, preferred_element_type=jnp.float32)
        for s in range(2):
            acc_ref[s] = corrs[s] * acc_ref[s] + pv[s * blk:(s + 1) * blk]

    def body(j, carry):
        visit(j, None)
        return carry

    lax.fori_loop(0, qi, body, 0)
    visit(qi, causal_diag)

    lam = (jnp.exp(jnp.sum(lq1_ref[...] * lk1_ref[...], axis=-1, keepdims=True))
           - jnp.exp(jnp.sum(lq2_ref[...] * lk2_ref[...], axis=-1, keepdims=True)) + lam_init)
    l1 = jnp.concatenate([l_ref[0], l_ref[0]], axis=1)
    l2 = jnp.concatenate([l_ref[1], l_ref[1]], axis=1)
    o = acc_ref[0] / l1 - lam * (acc_ref[1] / l2)
    o = o * lax.rsqrt(jnp.mean(o * o, axis=-1, keepdims=True) + LN_EPS) * g_ref[...] * (1.0 - lam_init)
    o_ref[...] = o.astype(o_ref.dtype)


def _da_attention(qkv, lambdas, subln_g, batch, seq, heads, hd, col0, lam_init):
    blk = _blk(seq, 512)
    nq = seq // blk
    qb, kb, vb = col0 // hd, col0 // hd + 2 * heads, col0 // (2 * hd) + 2 * heads
    vec = pl.BlockSpec((1, hd), lambda b, h, i: (0, 0))
    kern = functools.partial(_da_kernel, blk=blk, hd=hd, lam_init=lam_init)
    return pl.pallas_call(
        kern, out_shape=jax.ShapeDtypeStruct((batch * seq, heads * 2 * hd), jnp.bfloat16), grid=(batch, heads, nq),
        in_specs=[vec, vec, vec, vec, pl.BlockSpec((1, 2 * hd), lambda b, h, i: (0, 0)),
                  pl.BlockSpec((blk, hd), lambda b, h, i: (b * nq + i, qb + 2 * h)),
                  pl.BlockSpec((blk, hd), lambda b, h, i: (b * nq + i, qb + 2 * h + 1)),
                  pl.BlockSpec((seq, hd), lambda b, h, i: (b, kb + 2 * h)),
                  pl.BlockSpec((seq, hd), lambda b, h, i: (b, kb + 2 * h + 1)),
                  pl.BlockSpec((seq, 2 * hd), lambda b, h, i: (b, vb + h))],
        out_specs=pl.BlockSpec((blk, 2 * hd), lambda b, h, i: (b * nq + i, h)),
        scratch_shapes=[pltpu.VMEM((2, blk, hd), jnp.float32), pltpu.VMEM((2, blk, hd), jnp.float32),
                        pltpu.VMEM((2, blk, 2 * hd), jnp.float32)],
        compiler_params=_params("parallel", "parallel", "parallel"), name="differential_attention",
    )(*[l.reshape(1, hd) for l in lambdas], subln_g.reshape(1, 2 * hd), qkv, qkv, qkv, qkv, qkv)


def _extract_top(x, n):
    rows = lax.broadcasted_iota(jnp.int32, x.shape, 0).astype(jnp.float32)
    tops = []
    for _ in range(n):
        m = jnp.max(x, axis=0, keepdims=True)
        tops.append(m)
        first = jnp.min(jnp.where(x == m, rows, float(x.shape[0])), axis=0, keepdims=True)
        x = jnp.where(rows == first, NEG_INF, x)
    return tops


def _sorting_network(n):
    pairs, p = [], 1
    while p < n:
        k = p
        while k >= 1:
            for j in range(k % p, n - k, 2 * k):
                for i in range(min(k, n - j - k)):
                    if (i + j) // (2 * p) == (i + j + k) // (2 * p):
                        pairs.append((i + j, i + j + k))
            k //= 2
        p *= 2
    return pairs


def _extract_top_sorted(x, n):
    groups = [x[g * SUBLANES:(g + 1) * SUBLANES] for g in range(x.shape[0] // SUBLANES)]
    for a, b in _sorting_network(len(groups)):
        groups[a], groups[b] = jnp.maximum(groups[a], groups[b]), jnp.minimum(groups[a], groups[b])
    lane_list = lax.broadcasted_iota(jnp.int32, groups[0].shape, 0).astype(jnp.float32)
    tops = []
    for t in range(n):
        head = groups[0]
        m = jnp.max(head, axis=0, keepdims=True)
        tops.append(m)
        if t == n - 1:
            break
        first = jnp.min(jnp.where(head == m, lane_list, float(SUBLANES)), axis=0, keepdims=True)
        popped = lane_list == first
        live = min(len(groups), n - t)
        for g in range(live):
            below = groups[g + 1] if g + 1 < len(groups) else NEG_INF
            groups[g] = jnp.where(popped, below, groups[g])
    return tops


def _pair_candidates(top_a, top_b, combine):
    k = len(top_a)
    sub = SUBLANES
    b_all = jnp.concatenate(top_b, axis=0)
    pieces, masks, singles = [], [], []
    for p in range(k):
        n_q = k // (p + 1)
        if n_q == 1:
            singles.append(top_a[p])
            continue
        rows = -(-n_q // sub) * sub
        pieces.append(combine(top_a[p], b_all[0:rows]))
        masks.append(lax.broadcasted_iota(jnp.int32, (rows, 1), 0) < n_q)
    if singles:
        pieces.append(combine(jnp.concatenate(singles, axis=0), top_b[0]))
        masks.append(None)
    return pieces, masks


def _peer_topk_kernel(q_ref, keys_ref, s_ref, e_ref, tau_ref, *, n_keys, topk):
    half = q_ref.shape[1] // 2
    scores, tops = [], []
    for c in range(2):
        s = lax.dot_general(keys_ref[0, c], q_ref[:, c * half:(c + 1) * half], (((1,), (1,)), ((), ())),
                            preferred_element_type=jnp.float32)
        scores.append(s)
        n_groups = n_keys // SUBLANES
        pow2 = n_keys % SUBLANES == 0 and n_groups & (n_groups - 1) == 0
        tops.append(_extract_top_sorted(s, topk) if pow2 else _extract_top(s, topk))
    exps = [[jnp.exp(t - top[0]) for t in top] for top in tops]
    sums, masks = _pair_candidates(tops[0], tops[1], jnp.add)
    prods, _ = _pair_candidates(exps[0], exps[1], jnp.multiply)
    cand = jnp.concatenate([s if m is None else jnp.where(m, s, NEG_INF) for s, m in zip(sums, masks)], axis=0)
    cand_e = jnp.concatenate(prods, axis=0)
    groups = -(-cand.shape[0] // SUBLANES)
    pad_rows = SUBLANES * (1 << (groups - 1).bit_length()) - cand.shape[0]
    padded = jnp.concatenate([cand, jnp.full((pad_rows, cand.shape[1]), NEG_INF, cand.dtype)], axis=0)
    tau = _extract_top_sorted(padded, topk)[-1]
    z = jnp.sum(jnp.where(cand >= tau, cand_e, 0.0), axis=0, keepdims=True)
    s_ref[...] = jnp.concatenate(scores, axis=0)
    e_ref[...] = jnp.concatenate([jnp.exp(scores[0] - tops[0][0]) / z, jnp.exp(scores[1] - tops[1][0])], axis=0)
    tau_ref[0] = tau


def _peer_topk(hq, keys):
    t = hq.shape[0]
    heads, _, n_keys, half = keys.shape
    tn = _blk(t, 512)
    kern = functools.partial(_peer_topk_kernel, n_keys=n_keys, topk=PEER_TOPK)
    st = jax.ShapeDtypeStruct((heads * 2 * n_keys, t), jnp.float32)
    blk = pl.BlockSpec((2 * n_keys, tn), lambda i, h: (h, i))
    return pl.pallas_call(
        kern, out_shape=(st, st, jax.ShapeDtypeStruct((heads, 1, t), jnp.float32)), grid=(t // tn, heads),
        in_specs=[pl.BlockSpec((tn, 2 * half), lambda i, h: (i, h)),
                  pl.BlockSpec((1, 2, n_keys, half), lambda i, h: (h, 0, 0, 0))],
        out_specs=(blk, blk, pl.BlockSpec((1, 1, tn), lambda i, h: (h, 0, i))),
        compiler_params=_params("parallel", "parallel"), name="peer_topk",
    )(hq, keys)


def _peer_dense_kernel(h_ref, u_ref, v_ref, s_ref, e_ref, tau_ref, res_ref, g_ref, b_ref, o_ref, ga_ref, *,
                       heads, n_keys, n_blocks, alpha):
    j = pl.program_id(1)

    @pl.when(j == 0)
    def _():
        ga_ref[1] = jnp.zeros(ga_ref.shape[1:], ga_ref.dtype)
        o_ref[...] = jnp.zeros_like(o_ref)

    for parity in range(2):
        pl.when(j % 2 == parity)(functools.partial(
            _peer_dense_step, h_ref, u_ref, v_ref, s_ref, e_ref, tau_ref, o_ref, ga_ref,
            jnp.minimum(j, n_blocks - 1), parity, heads, n_keys))

    @pl.when(j == n_blocks)
    def _():
        rows = min(o_ref.shape[0], LANES)
        for r in range(o_ref.shape[0] // rows):
            sl = slice(r * rows, (r + 1) * rows)
            o_ref[sl] = _ln_rows(res_ref.at[sl], o_ref.at[sl], g_ref, b_ref, alpha)


def _peer_dense_step(h_ref, u_ref, v_ref, s_ref, e_ref, tau_ref, o_ref, ga_ref, jb, slot_now, heads, n_keys):
    tm, te = ga_ref.shape[1:]
    d = o_ref.shape[1]
    n_i = te // n_keys
    mxu = MXU_TILE
    n_tok_tiles = tm // mxu
    rt = min(te, PEER_KEY_ROWS)

    def link_from(x):
        bits = pltpu.bitcast(x[0:8, 0:LANES], jnp.uint32)
        zero = lax.shift_right_logical(lax.shift_right_logical(bits, jnp.uint32(16)), jnp.uint32(16))
        return pltpu.bitcast(zero, jnp.float32)

    pre_tiles = {}

    k_parts = PEER_KEY_SPLIT
    kw = d // k_parts
    done_tiles = set()

    def pre_tile(p, kk):
        mi, ni = divmod(p, n_tok_tiles)
        part = jnp.dot(u_ref[mi * rt:(mi + 1) * rt, kk * kw:(kk + 1) * kw],
                       h_ref[kk * kw:(kk + 1) * kw, ni * mxu:(ni + 1) * mxu], preferred_element_type=jnp.float32)
        pre_tiles[mi, ni] = part if kk == 0 else pre_tiles[mi, ni] + part
        if kk == k_parts - 1:
            done_tiles.add((mi, ni))
        return link_from(part)

    def value_chunk(c):
        cols = slice(c * mxu, (c + 1) * mxu)
        acc = o_ref[:, cols] + jnp.dot(ga_ref[1 - slot_now], v_ref[:, cols], preferred_element_type=jnp.float32)
        o_ref[:, cols] = acc
        return link_from(acc)

    n_pre = (te // rt) * n_tok_tiles
    chunks = [functools.partial(pre_tile, p, kk) for p in range(n_pre) for kk in range(k_parts)]
    chunks += [functools.partial(value_chunk, c) for c in range(d // mxu)]
    results = []

    def finalize(il):
        mi, off = divmod(il * n_keys, rt)
        while any((mi, ni) not in done_tiles for ni in range(n_tok_tiles)):
            results.append(chunks.pop(0)())
        p = jnp.concatenate([pre_tiles[mi, ni][off:off + n_keys, :] for ni in range(n_tok_tiles)], axis=1)
        act = 0.5 * p * (1.0 + lax.erf(p * (2.0 ** -0.5)))
        ga_ref[slot_now, :, il * n_keys:(il + 1) * n_keys] = (gates.pop(il) * act).astype(ga_ref.dtype).T

    pieces = [(il, h) for il in range(n_i) for h in range(heads)]
    costs = [rt * kw * mxu] * (n_pre * k_parts) + [tm * te * mxu] * (d // mxu)
    ends = [sum(costs[:c + 1]) for c in range(len(costs))]
    slot = ends[-1] / len(pieces)
    linked = 0
    gates = {}
    link = link_from(s_ref[0:8, 0:LANES])
    for k, (il, h) in enumerate(pieces):
        while len(results) < len(costs) and ends[len(results)] - costs[len(results)] < (k + 1) * slot:
            results.append(chunks.pop(0)())
        while linked < len(results) and ends[linked] <= k * slot:
            link, linked = link + results[linked], linked + 1
        for _ in range(PEER_LINK_ROLLS):
            link = pltpu.roll(link, 1, 1)
        base = h * 2 * n_keys
        a_row = s_ref[pl.ds(base + jb * n_i + il, 1), :] + jnp.concatenate([link[0:1, :]] * (tm // LANES), axis=1)
        ea_row = e_ref[pl.ds(base + jb * n_i + il, 1), :]
        b_tile = s_ref[base + n_keys:base + 2 * n_keys, :]
        eb_tile = e_ref[base + n_keys:base + 2 * n_keys, :]
        term = jnp.where(a_row + b_tile >= tau_ref[h], ea_row * eb_tile, 0.0)
        gates[il] = term if h == 0 else gates[il] + term
        if h == heads - 1:
            finalize(il)
    while chunks:
        chunks.pop(0)()


def _peer_dense_ln(h, h_bf_t, u_bf, v_bf, scores_t, exps_t, tau, g, b, alpha, heads, n_keys):
    d, t = h_bf_t.shape
    e = u_bf.shape[0]
    tm, te = _blk(t, PEER_TOKEN_BLOCK), _blk(e, PEER_EXPERT_BLOCK)
    nb = e // te
    kern = functools.partial(_peer_dense_kernel, heads=heads, n_keys=n_keys, n_blocks=nb, alpha=alpha)
    rows = scores_t.shape[0]
    once = pl.Buffered(1)
    vec = pl.BlockSpec((1, d), lambda i, j: (0, 0))
    return pl.pallas_call(
        kern, out_shape=jax.ShapeDtypeStruct((t, d), jnp.float32), grid=(t // tm, nb + 1),
        in_specs=[pl.BlockSpec((d, tm), lambda i, j: (0, i), pipeline_mode=once),
                  pl.BlockSpec((te, d), lambda i, j: (jnp.minimum(j, nb - 1), 0)),
                  pl.BlockSpec((te, d), lambda i, j: (jnp.maximum(j - 1, 0), 0)),
                  pl.BlockSpec((rows, tm), lambda i, j: (0, i), pipeline_mode=once),
                  pl.BlockSpec((rows, tm), lambda i, j: (0, i), pipeline_mode=once),
                  pl.BlockSpec((heads, 1, tm), lambda i, j: (0, 0, i)),
                  pl.BlockSpec((tm, d), lambda i, j: (i, 0), pipeline_mode=once), vec, vec],
        out_specs=pl.BlockSpec((tm, d), lambda i, j: (i, 0), pipeline_mode=once),
        scratch_shapes=[pltpu.VMEM((2, tm, te), jnp.bfloat16)],
        compiler_params=pltpu.CompilerParams(dimension_semantics=("parallel", "arbitrary"),
                                             vmem_limit_bytes=PEER_VMEM_LIMIT_BYTES), name="peer_dense",
    )(h_bf_t, u_bf, v_bf, scores_t, exps_t, tau, h, g.reshape(1, d), b.reshape(1, d))


def kernel(x, positions, w_in, b_gate, lambda_q1, lambda_k1, lambda_q2, lambda_k2, subln_g, w_sb_branch,
           w_da_branch, w_out, ln1_g, ln1_b, peer_w_q, peer_sub_keys, peer_u, peer_v, ln2_g, ln2_b):
    batch, seq, d = x.shape
    depth = w_in.shape[0]
    hd = lambda_q1.shape[-1]
    sb_w, da_w = w_sb_branch.shape[1], w_da_branch.shape[1]
    sb_heads, da_heads = sb_w // hd, da_w // (2 * hd)
    n_qkv = 3 * sb_w + 3 * da_w
    alpha = (2 * depth) ** 0.25
    bf = jnp.bfloat16
    t = batch * seq

    cos, sin = _rope_tables(positions, hd)
    col_scale = np.ones((1, n_qkv), np.float32)
    col_scale[:, :sb_w] = hd ** -0.5 * math.log2(math.e)
    col_scale[:, 3 * sb_w:3 * sb_w + da_w] = hd ** -0.5 * math.log2(math.e)
    col_scale = jnp.asarray(col_scale)
    h = x.reshape(t, d)
    h_bf = _to_bf16(h)
    for l in range(depth):
        lam_init = 0.8 - 0.6 * math.exp(-0.3 * l)
        w_in_bf = _to_bf16(w_in[l])
        qkv, w_sb_bf, w_da_bf, w_out_bf, w_q_bf = _qkv_proj(
            h_bf, w_in_bf, col_scale, cos, sin, n_qkv, (3 * sb_w, 3 * sb_w + 2 * da_w), hd,
            (w_sb_branch[l], w_da_branch[l], w_out[l], peer_w_q[l]))
        gate, u_bf = _gate_proj(h_bf, w_in_bf, b_gate[l].reshape(1, -1), n_qkv, 2 * d, (peer_u[l],))
        o_sb = _sb_attention(qkv, batch, seq, sb_heads, hd)
        o_da = _da_attention(qkv, (lambda_q1[l], lambda_k1[l], lambda_q2[l], lambda_k2[l]), subln_g[l],
                             batch, seq, da_heads, hd, 3 * sb_w, lam_init)
        merged, v_bf = _merge(o_sb, o_da, w_sb_bf, w_da_bf, gate, (peer_v[l],))
        mix = _matmul(merged, w_out_bf, jnp.float32, "out_proj")
        h, h_bf_t, hq = _residual_ln_peer_query(h, mix, ln1_g[l], ln1_b[l], alpha, w_q_bf)

        heads, _, n_keys, half = peer_sub_keys[l].shape
        scores_t, exps_t, tau = _peer_topk(hq, peer_sub_keys[l].astype(bf))
        h = _peer_dense_ln(h, h_bf_t, u_bf, v_bf, scores_t, exps_t, tau, ln2_g[l], ln2_b[l], alpha, heads, n_keys)
        if l + 1 < depth:
            h_bf = _to_bf16(h)
    return h.reshape(batch, seq, d)
```
